```python
import jax
import jax.numpy as jnp
from jax import lax
import numpy as np

D_MODEL = 2048
BATCH = 4
SEQ = 4096
DEPTH = 4

GRID_W = 64
CTX_LEN = 256
N_MIXERS = 3
N_LAYERS_A = (DEPTH + 2) // 3
N_LAYERS_B = (DEPTH + 1) // 3
N_LAYERS_C = DEPTH // 3

HEAD_DIM = 128
Q_BLOCK = 128
ROPE_THETA = 10000.0
NORM_EPS = 1e-6
NEG_INF = -1e30

GQA_HEADS = D_MODEL // HEAD_DIM
GQA_KV_HEADS = GQA_HEADS // 4
GQA_GROUP = GQA_HEADS // GQA_KV_HEADS

NA_HEADS = D_MODEL // HEAD_DIM
NA_ROWS = 8
NA_COLS = 16

MLA_HEADS = D_MODEL // HEAD_DIM
MLA_Q_LORA = D_MODEL // 4
MLA_KV_LORA = D_MODEL // 8
MLA_NOPE = 128
MLA_ROPE = 64
MLA_V = 128
MLA_QK = MLA_NOPE + MLA_ROPE

MOE_GROUPS = 4
MOE_EXPERTS_PER_GROUP = 8
MOE_EXPERTS = MOE_GROUPS * MOE_EXPERTS_PER_GROUP
MOE_TOP_K = 2
MOE_D_EXPERT = D_MODEL // 4
MOE_BLOCK = 128

kernel_name = 'hybrid_interleaved_dit_gqa_na_mla_hmoe'


def rms_norm(x, g):
    xf = x.astype(jnp.float32)
    y = xf * lax.rsqrt(jnp.mean(xf * xf, axis=-1, keepdims=True) + NORM_EPS)
    return (y * g.astype(jnp.float32)).astype(x.dtype)


def modulate(x, g, shift, scale):
    return rms_norm(x, g) * (1 + scale) + shift


def adaln_params(cond, w, b):
    m = jnp.dot(cond, w) + b
    return jnp.split(m[..., None, :], 6, axis=-1)


def axial_rope_tables(n_tokens, dim):
    n_freq = dim // 4
    freqs = ROPE_THETA ** (-jnp.arange(n_freq, dtype=jnp.float32) / n_freq)
    t = jnp.arange(n_tokens, dtype=jnp.int32)
    row = (t // GRID_W).astype(jnp.float32)
    col = (t % GRID_W).astype(jnp.float32)
    ang = jnp.concatenate([row[:, None] * freqs, col[:, None] * freqs], axis=-1)
    return jnp.cos(ang), jnp.sin(ang)


def apply_rope(x, cos, sin):
    shape = (x.shape[1],) + (1,) * (x.ndim - 3) + (cos.shape[-1],)
    cos = cos.reshape(shape)
    sin = sin.reshape(shape)
    half = x.shape[-1] // 2
    x1 = x[..., :half].astype(jnp.float32)
    x2 = x[..., half:].astype(jnp.float32)
    return jnp.concatenate([x1 * cos - x2 * sin, x2 * cos + x1 * sin], axis=-1).astype(x.dtype)


def softmax_attend(q, k, v, scale):
    s = jnp.einsum('bqhgd,bkhd->bhgqk', q, k).astype(jnp.float32) * scale
    p = jax.nn.softmax(s, axis=-1).astype(v.dtype)
    return jnp.einsum('bhgqk,bkhd->bqhgd', p, v)


def blocked_attend(q, k, v, scale):
    b, s = q.shape[:2]
    nb = s // Q_BLOCK
    qb = jnp.moveaxis(q.reshape((b, nb, Q_BLOCK) + q.shape[2:]), 1, 0)
    ob = lax.map(lambda blk: softmax_attend(blk, k, v, scale), qb)
    return jnp.moveaxis(ob, 0, 1).reshape((b, s) + ob.shape[3:])


def qkv_heads(t, w_qkv, q_g, k_g, n_heads, n_kv_heads, head_dim):
    b, n = t.shape[:2]
    q, k, v = jnp.split(jnp.dot(t, w_qkv),
                        [n_heads * head_dim, (n_heads + n_kv_heads) * head_dim], axis=-1)
    q = rms_norm(q.reshape(b, n, n_kv_heads, n_heads // n_kv_heads, head_dim), q_g)
    k = rms_norm(k.reshape(b, n, n_kv_heads, head_dim), k_g)
    v = v.reshape(b, n, n_kv_heads, head_dim)
    return q, k, v


def gqa_mixer(h, hc, w_qkv, q_g, k_g, w_o, cos, sin, need_ctx):
    b, s = h.shape[:2]
    q, k, v = qkv_heads(h, w_qkv, q_g, k_g, GQA_HEADS, GQA_KV_HEADS, HEAD_DIM)
    q, k = apply_rope(q, cos, sin), apply_rope(k, cos, sin)
    qc, kc, vc = qkv_heads(hc, w_qkv, q_g, k_g, GQA_HEADS, GQA_KV_HEADS, HEAD_DIM)
    scale = HEAD_DIM ** -0.5
    o = blocked_attend(q, jnp.concatenate([kc, k], axis=1), jnp.concatenate([vc, v], axis=1), scale)
    y = jnp.dot(o.reshape(b, s, GQA_HEADS * HEAD_DIM), w_o)
    yc = None
    if need_ctx:
        oc = softmax_attend(qc, kc, vc, scale)
        yc = jnp.dot(oc.reshape(b, hc.shape[1], GQA_HEADS * HEAD_DIM), w_o)
    return y, yc


def na_mixer(h, hc, w_qkv, q_g, k_g, rpb, w_o, need_ctx):
    b, s = h.shape[:2]
    rows = s // GRID_W
    kr = min(NA_ROWS, rows)
    q, k, v = qkv_heads(h, w_qkv, q_g, k_g, NA_HEADS, NA_HEADS, HEAD_DIM)
    qc, kc, vc = qkv_heads(hc, w_qkv, q_g, k_g, NA_HEADS, NA_HEADS, HEAD_DIM)
    scale = HEAD_DIM ** -0.5
    grid = (b, rows, GRID_W, NA_HEADS, HEAD_DIM)
    qg = q[:, :, :, 0].reshape(grid)
    kg = k.reshape(grid)
    vg = v.reshape(grid)
    col = jnp.arange(GRID_W, dtype=jnp.int32)
    col_start = jnp.clip(col - NA_COLS // 2, 0, GRID_W - NA_COLS)
    col_in = (col[None, :] >= col_start[:, None]) & (col[None, :] < col_start[:, None] + NA_COLS)
    dc = jnp.clip(col[None, :] - col[:, None] + NA_COLS - 1, 0, 2 * NA_COLS - 2)
    row_start = jnp.clip(jnp.arange(rows, dtype=jnp.int32) - kr // 2, 0, rows - kr)

    def row_block(args):
        qr, r, rs = args
        kb = lax.dynamic_slice_in_dim(kg, rs, kr, axis=1)
        vb = lax.dynamic_slice_in_dim(vg, rs, kr, axis=1)
        dr = rs + jnp.arange(kr, dtype=jnp.int32) - r + NA_ROWS - 1
        bias = rpb[:, dr[None, :, None], dc[:, None, :]]
        s_win = jnp.einsum('bqhd,brkhd->bhqrk', qr, kb).astype(jnp.float32) * scale
        s_win = s_win + bias.astype(jnp.float32)[None]
        s_win = jnp.where(col_in[:, None, :], s_win, NEG_INF).reshape(b, NA_HEADS, GRID_W, kr * GRID_W)
        s_ctx = jnp.einsum('bqhd,bchd->bhqc', qr, kc).astype(jnp.float32) * scale
        p = jax.nn.softmax(jnp.concatenate([s_win, s_ctx], axis=-1), axis=-1).astype(vg.dtype)
        p_win, p_ctx = p[..., :kr * GRID_W], p[..., kr * GRID_W:]
        return (jnp.einsum('bhqn,bnhd->bqhd', p_win, vb.reshape(b, kr * GRID_W, NA_HEADS, HEAD_DIM))
                + jnp.einsum('bhqc,bchd->bqhd', p_ctx, vc))

    o = lax.map(row_block, (jnp.moveaxis(qg, 1, 0), jnp.arange(rows, dtype=jnp.int32), row_start))
    y = jnp.dot(jnp.moveaxis(o, 0, 1).reshape(b, s, NA_HEADS * HEAD_DIM), w_o)
    yc = None
    if need_ctx:
        oc = softmax_attend(qc, kc, vc, scale)
        yc = jnp.dot(oc.reshape(b, hc.shape[1], NA_HEADS * HEAD_DIM), w_o)
    return y, yc


def mla_heads(t, w_down, q_lora_g, w_uq, kv_lora_g, w_ukv, q_g, k_g):
    b, n = t.shape[:2]
    cq, ckv, k_rope = jnp.split(jnp.dot(t, w_down), [MLA_Q_LORA, MLA_Q_LORA + MLA_KV_LORA], axis=-1)
    q = jnp.dot(rms_norm(cq, q_lora_g), w_uq).reshape(b, n, MLA_HEADS, MLA_QK)
    kv = jnp.dot(rms_norm(ckv, kv_lora_g), w_ukv).reshape(b, n, MLA_HEADS, MLA_NOPE + MLA_V)
    k_nope, v = jnp.split(kv, [MLA_NOPE], axis=-1)
    k_rope = jnp.broadcast_to(k_rope[:, :, None, :], (b, n, MLA_HEADS, MLA_ROPE))
    q = rms_norm(q, q_g)
    k = rms_norm(jnp.concatenate([k_nope, k_rope], axis=-1), k_g)
    return q, k, v


def rope_tail(x, cos, sin):
    return jnp.concatenate([x[..., :MLA_NOPE], apply_rope(x[..., MLA_NOPE:], cos, sin)], axis=-1)


def mla_mixer(h, hc, w_down, q_lora_g, w_uq, kv_lora_g, w_ukv, q_g, k_g, w_o, cos, sin, need_ctx):
    b, s = h.shape[:2]
    q, k, v = mla_heads(h, w_down, q_lora_g, w_uq, kv_lora_g, w_ukv, q_g, k_g)
    q, k = rope_tail(q, cos, sin), rope_tail(k, cos, sin)
    qc, kc, vc = mla_heads(hc, w_down, q_lora_g, w_uq, kv_lora_g, w_ukv, q_g, k_g)
    scale = MLA_QK ** -0.5
    o = blocked_attend(q[:, :, :, None, :], jnp.concatenate([kc, k], axis=1),
                       jnp.concatenate([vc, v], axis=1), scale)
    y = jnp.dot(o.reshape(b, s, MLA_HEADS * MLA_V), w_o)
    yc = None
    if need_ctx:
        oc = softmax_attend(qc[:, :, :, None, :], kc, vc, scale)
        yc = jnp.dot(oc.reshape(b, hc.shape[1], MLA_HEADS * MLA_V), w_o)
    return y, yc


def hier_moe(h, w_group, b_group, w_expert, b_expert, w_in, w_out):
    n, d = h.shape
    g_logits = jnp.dot(h, w_group).astype(jnp.float32) + b_group.astype(jnp.float32)
    g_prob = jax.nn.softmax(g_logits, axis=-1)
    g_sel = jnp.argmax(g_logits, axis=-1).astype(jnp.int32)
    g_gate = g_prob[jnp.arange(n), g_sel][:, None]
    e_logits = jnp.einsum('nd,gde->nge', h, w_expert).astype(jnp.float32) + b_expert.astype(jnp.float32)
    e_logits = e_logits[jnp.arange(n), g_sel]
    top_val, top_idx = lax.top_k(e_logits, MOE_TOP_K)
    weights = (g_gate * jax.nn.softmax(top_val, axis=-1)).astype(h.dtype)
    expert_ids = g_sel[:, None] * MOE_EXPERTS_PER_GROUP + top_idx.astype(jnp.int32)

    nk = n * MOE_TOP_K
    flat_e = expert_ids.reshape(nk)
    flat_w = weights.reshape(nk)
    flat_tok = jnp.arange(nk, dtype=jnp.int32) // MOE_TOP_K
    order = jnp.argsort(flat_e)
    sorted_e = flat_e[order]
    counts = jnp.zeros((MOE_EXPERTS,), jnp.int32).at[flat_e].add(1)
    start = jnp.cumsum(counts) - counts
    padded = (counts + MOE_BLOCK - 1) // MOE_BLOCK * MOE_BLOCK
    pad_end = jnp.cumsum(padded)
    pad_start = pad_end - padded
    n_blocks = -(-(nk + MOE_EXPERTS * (MOE_BLOCK - 1)) // MOE_BLOCK)
    cap = n_blocks * MOE_BLOCK
    dest = pad_start[sorted_e] + (jnp.arange(nk, dtype=jnp.int32) - start[sorted_e])
    slot_tok = jnp.full((cap,), n, jnp.int32).at[dest].set(flat_tok[order])
    slot_w = jnp.zeros((cap,), h.dtype).at[dest].set(flat_w[order])
    block_expert = jnp.minimum(
        jnp.searchsorted(pad_end, jnp.arange(n_blocks, dtype=jnp.int32) * MOE_BLOCK, side='right'),
        MOE_EXPERTS - 1).astype(jnp.int32)
    h_pad = jnp.concatenate([h, jnp.zeros((1, d), h.dtype)], axis=0)
    xb = h_pad[slot_tok].reshape(n_blocks, MOE_BLOCK, d)

    def expert_block(args):
        xblk, e = args
        gate, up = jnp.split(jnp.dot(xblk, w_in[e]), 2, axis=-1)
        return jnp.dot(jax.nn.silu(gate) * up, w_out[e])

    yb = lax.map(expert_block, (xb, block_expert)).reshape(cap, d)
    y = jnp.zeros((n + 1, d), h.dtype).at[slot_tok].add(yb * slot_w[:, None])
    return y[:n]


def setup_inputs(seed: int = 0) -> dict:
    key = jax.random.key(seed)
    ks = iter(jax.random.split(key, 32))

    def nrm(shape, scale):
        return jax.random.normal(next(ks), shape, jnp.float32) * scale

    def gain(shape):
        return 1.0 + nrm(shape, 0.02)

    D = D_MODEL
    return {
        'x': nrm((BATCH, SEQ, D), 1.0),
        'c': nrm((BATCH, D), 1.0),
        'ctx': nrm((BATCH, CTX_LEN, D), 1.0),
        'c_ctx': nrm((D,), 1.0),
        'mod_w': nrm((DEPTH, D, 6 * D), 0.5 * D ** -0.5),
        'mod_b': nrm((DEPTH, 6 * D), 0.02),
        'norm_g': gain((DEPTH, 2, D)),
        'gqa_w_qkv': nrm((N_LAYERS_A, D, (GQA_HEADS + 2 * GQA_KV_HEADS) * HEAD_DIM), D ** -0.5),
        'gqa_q_norm': gain((N_LAYERS_A, HEAD_DIM)),
        'gqa_k_norm': gain((N_LAYERS_A, HEAD_DIM)),
        'gqa_w_o': nrm((N_LAYERS_A, GQA_HEADS * HEAD_DIM, D), (GQA_HEADS * HEAD_DIM) ** -0.5),
        'na_w_qkv': nrm((N_LAYERS_B, D, 3 * NA_HEADS * HEAD_DIM), D ** -0.5),
        'na_q_norm': gain((N_LAYERS_B, HEAD_DIM)),
        'na_k_norm': gain((N_LAYERS_B, HEAD_DIM)),
        'na_rpb': nrm((N_LAYERS_B, NA_HEADS, 2 * NA_ROWS - 1, 2 * NA_COLS - 1), 0.1),
        'na_w_o': nrm((N_LAYERS_B, NA_HEADS * HEAD_DIM, D), (NA_HEADS * HEAD_DIM) ** -0.5),
        'mla_w_down': nrm((N_LAYERS_C, D, MLA_Q_LORA + MLA_KV_LORA + MLA_ROPE), D ** -0.5),
        'mla_q_lora_norm': gain((N_LAYERS_C, MLA_Q_LORA)),
        'mla_w_uq': nrm((N_LAYERS_C, MLA_Q_LORA, MLA_HEADS * MLA_QK), MLA_Q_LORA ** -0.5),
        'mla_kv_lora_norm': gain((N_LAYERS_C, MLA_KV_LORA)),
        'mla_w_ukv': nrm((N_LAYERS_C, MLA_KV_LORA, MLA_HEADS * (MLA_NOPE + MLA_V)), MLA_KV_LORA ** -0.5),
        'mla_q_norm': gain((N_LAYERS_C, MLA_QK)),
        'mla_k_norm': gain((N_LAYERS_C, MLA_QK)),
        'mla_w_o': nrm((N_LAYERS_C, MLA_HEADS * MLA_V, D), (MLA_HEADS * MLA_V) ** -0.5),
        'moe_w_group': nrm((DEPTH, D, MOE_GROUPS), D ** -0.5),
        'moe_b_group': nrm((DEPTH, MOE_GROUPS), 0.01),
        'moe_w_expert': nrm((DEPTH, MOE_GROUPS, D, MOE_EXPERTS_PER_GROUP), D ** -0.5),
        'moe_b_expert': nrm((DEPTH, MOE_GROUPS, MOE_EXPERTS_PER_GROUP), 0.01),
        'moe_w_in': nrm((DEPTH, MOE_EXPERTS, D, 2 * MOE_D_EXPERT), D ** -0.5),
        'moe_w_out': nrm((DEPTH, MOE_EXPERTS, MOE_D_EXPERT, D), MOE_D_EXPERT ** -0.5),
    }


def reference(x, c, ctx, c_ctx, mod_w, mod_b, norm_g,
              gqa_w_qkv, gqa_q_norm, gqa_k_norm, gqa_w_o,
              na_w_qkv, na_q_norm, na_k_norm, na_rpb, na_w_o,
              mla_w_down, mla_q_lora_norm, mla_w_uq, mla_kv_lora_norm, mla_w_ukv,
              mla_q_norm, mla_k_norm, mla_w_o,
              moe_w_group, moe_b_group, moe_w_expert, moe_b_expert, moe_w_in, moe_w_out):
    b, s, d = x.shape
    cos_a, sin_a = axial_rope_tables(s, HEAD_DIM)
    cos_c, sin_c = axial_rope_tables(s, MLA_ROPE)
    silu_c = jax.nn.silu(c)
    silu_cc = jax.nn.silu(c_ctx)
    for i in range(DEPTH):
        last = i == DEPTH - 1
        sh1, sc1, g1, sh2, sc2, g2 = adaln_params(silu_c, mod_w[i], mod_b[i])
        csh1, csc1, cg1, csh2, csc2, cg2 = adaln_params(silu_cc, mod_w[i], mod_b[i])
        h = modulate(x, norm_g[i, 0], sh1, sc1)
        hc = modulate(ctx, norm_g[i, 0], csh1, csc1)
        kind, j = i % N_MIXERS, i // N_MIXERS
        if kind == 0:
            y, yc = gqa_mixer(h, hc, gqa_w_qkv[j], gqa_q_norm[j], gqa_k_norm[j], gqa_w_o[j],
                              cos_a, sin_a, not last)
        elif kind == 1:
            y, yc = na_mixer(h, hc, na_w_qkv[j], na_q_norm[j], na_k_norm[j], na_rpb[j], na_w_o[j],
                             not last)
        else:
            y, yc = mla_mixer(h, hc, mla_w_down[j], mla_q_lora_norm[j], mla_w_uq[j],
                              mla_kv_lora_norm[j], mla_w_ukv[j], mla_q_norm[j], mla_k_norm[j],
                              mla_w_o[j], cos_c, sin_c, not last)
        x = x + g1 * y
        h = modulate(x, norm_g[i, 1], sh2, sc2)
        if last:
            out = hier_moe(h.reshape(-1, d), moe_w_group[i], moe_b_group[i], moe_w_expert[i],
                           moe_b_expert[i], moe_w_in[i], moe_w_out[i])
            x = x + g2 * out.reshape(b, s, d)
        else:
            ctx = ctx + cg1 * yc
            hc = modulate(ctx, norm_g[i, 1], csh2, csc2)
            tokens = jnp.concatenate([h.reshape(-1, d), hc.reshape(-1, d)], axis=0)
            out = hier_moe(tokens, moe_w_group[i], moe_b_group[i], moe_w_expert[i],
                           moe_b_expert[i], moe_w_in[i], moe_w_out[i])
            x = x + g2 * out[:b * s].reshape(b, s, d)
            ctx = ctx + cg2 * out[b * s:].reshape(ctx.shape)
    return x
```

```python
import functools

import jax
import jax.numpy as jnp
from jax import lax
from jax.experimental import pallas as pl
from jax.experimental.pallas import tpu as pltpu

F32 = jnp.float32
BF16 = jnp.bfloat16

GRID_W = 64
HEAD_DIM = 128
ROPE_THETA = 10000.0
NORM_EPS = 1e-6
NEG_INF = -1e30
NA_ROWS = 8
NA_COLS = 16
MLA_NOPE = 128
MLA_ROPE = 64
MLA_V = 128
MLA_QK = MLA_NOPE + MLA_ROPE
MLA_HEAD_PAD = 256
MOE_TOP_K = 2
LANES = 128
MOD_ROWS = 8
MOE_BLOCK = 256
COMBINE_TM = 256
VMEM_LIMIT = 52 * 1024 * 1024


def _cparams(*sem):
    return pltpu.CompilerParams(dimension_semantics=sem, vmem_limit_bytes=VMEM_LIMIT)


def _silu(v):
    return v / (1.0 + jnp.exp(-v))


def _adaln_kernel(c_ref, w_ref, b_ref, o_ref):
    cond = _silu(c_ref[...]).astype(BF16)
    o_ref[0] = jnp.dot(cond, w_ref[0].astype(BF16), preferred_element_type=F32) + b_ref[0]


def adaln_all(cond, mod_w, mod_b):
    depth, d, n = mod_w.shape
    tn = 1024
    return pl.pallas_call(
        _adaln_kernel,
        grid=(depth, n // tn),
        in_specs=[pl.BlockSpec((MOD_ROWS, d), lambda l, j: (0, 0)),
                  pl.BlockSpec((1, d, tn), lambda l, j: (l, 0, j)),
                  pl.BlockSpec((1, 1, tn), lambda l, j: (l, 0, j))],
        out_specs=pl.BlockSpec((1, MOD_ROWS, tn), lambda l, j: (l, 0, j)),
        out_shape=jax.ShapeDtypeStruct((depth, MOD_ROWS, n), F32),
        compiler_params=_cparams("arbitrary", "arbitrary"),
        name="adaln",
    )(cond, mod_w, mod_b.reshape(depth, 1, n))


def _rms(x, g):
    ms = jnp.mean(x * x, axis=-1, keepdims=True)
    return x * lax.rsqrt(ms + NORM_EPS) * g


def _rope128(y, c, s):
    return y * c + pltpu.roll(y, 64, 1) * s


def _rope64(y, c, s1, s2):
    return y * c + pltpu.roll(y, 96, 1) * s1 + pltpu.roll(y, 32, 1) * s2


def _qkv_kernel(x_ref, g_ref, sh_ref, sc_ref, w_ref, gain_ref, c_ref, s_ref, o_ref, h_scr, *, n_norm_tiles, rope):
    j = pl.program_id(1)

    @pl.when(j == 0)
    def _():
        h = _rms(x_ref[...], g_ref[...]) * (1.0 + sc_ref[0]) + sh_ref[0]
        h_scr[...] = h.astype(BF16)

    y = jnp.dot(h_scr[...], w_ref[...], preferred_element_type=F32)
    n_heads = y.shape[1] // HEAD_DIM

    @pl.when(j < n_norm_tiles)
    def _():
        for hh in range(n_heads):
            sl = slice(hh * HEAD_DIM, (hh + 1) * HEAD_DIM)
            yh = _rms(y[:, sl], gain_ref[:, sl])
            if rope:
                yh = _rope128(yh, c_ref[...], s_ref[...])
            o_ref[:, sl] = yh.astype(o_ref.dtype)

    @pl.when(j >= n_norm_tiles)
    def _():
        o_ref[...] = y.astype(o_ref.dtype)


def _mod_specs(layer, which, tm, seq, batch, d):
    def seg(i):
        return jnp.minimum((i * tm) // seq, batch)
    return [pl.BlockSpec((1, 1, d), lambda i, j, w=w: ((layer * MOD_ROWS + seg(i)) * 6 + w, 0, 0)) for w in which]


def qkv_project(xs, mods, layer, norm_g, w_bf, gain, cos_t, sin_t, *, tm, seq, batch, n_norm_cols, tn, rope):
    t, d = xs.shape
    n = w_bf.shape[1]
    sh_spec, sc_spec = _mod_specs(layer, (0, 1), tm, seq, batch, d)
    kern = functools.partial(_qkv_kernel, n_norm_tiles=n_norm_cols // tn, rope=rope)
    return pl.pallas_call(
        kern,
        grid=(t // tm, n // tn),
        in_specs=[pl.BlockSpec((tm, d), lambda i, j: (i, 0)),
                  pl.BlockSpec((1, d), lambda i, j: (0, 0)),
                  sh_spec, sc_spec,
                  pl.BlockSpec((d, tn), lambda i, j: (0, j)),
                  pl.BlockSpec((1, tn), lambda i, j: (0, j)),
                  pl.BlockSpec((tm, HEAD_DIM), lambda i, j: (i, 0)),
                  pl.BlockSpec((tm, HEAD_DIM), lambda i, j: (i, 0))],
        out_specs=pl.BlockSpec((tm, tn), lambda i, j: (i, j)),
        out_shape=jax.ShapeDtypeStruct((t, n), BF16),
        scratch_shapes=[pltpu.VMEM((tm, d), BF16)],
        compiler_params=_cparams("arbitrary", "arbitrary"),
        name="qkv_project",
    )(xs, norm_g.reshape(1, d), mods, mods, w_bf, gain, cos_t, sin_t)


def _down_kernel(x_ref, g_ref, sh_ref, sc_ref, w_ref, o_ref):
    h = _rms(x_ref[...], g_ref[...]) * (1.0 + sc_ref[0]) + sh_ref[0]
    o_ref[...] = jnp.dot(h.astype(BF16), w_ref[...], preferred_element_type=F32)


def mla_down(xs, mods, layer, norm_g, w_bf, *, tm, seq, batch):
    t, d = xs.shape
    n = w_bf.shape[1]
    sh_spec, sc_spec = _mod_specs(layer, (0, 1), tm, seq, batch, d)
    return pl.pallas_call(
        _down_kernel,
        grid=(t // tm, 1),
        in_specs=[pl.BlockSpec((tm, d), lambda i, j: (i, 0)),
                  pl.BlockSpec((1, d), lambda i, j: (0, 0)),
                  sh_spec, sc_spec,
                  pl.BlockSpec((d, n), lambda i, j: (0, 0))],
        out_specs=pl.BlockSpec((tm, n), lambda i, j: (i, 0)),
        out_shape=jax.ShapeDtypeStruct((t, n), F32),
        compiler_params=_cparams("arbitrary", "arbitrary"),
        name="mla_down",
    )(xs, norm_g.reshape(1, d), mods, mods, w_bf)


def _mla_q_kernel(cq_ref, g_ref, w_ref, gain_ref, c_ref, s1_ref, s2_ref, o_ref):
    h = _rms(cq_ref[...], g_ref[...]).astype(BF16)
    y = jnp.dot(h, w_ref[...], preferred_element_type=F32)
    for hh in range(y.shape[1] // MLA_HEAD_PAD):
        lo = hh * MLA_HEAD_PAD
        yh = y[:, lo:lo + MLA_HEAD_PAD]
        ms = jnp.sum(yh * yh, axis=-1, keepdims=True) * (1.0 / MLA_QK)
        yh = yh * lax.rsqrt(ms + NORM_EPS) * gain_ref[:, lo:lo + MLA_HEAD_PAD]
        o_ref[:, lo:lo + MLA_NOPE] = yh[:, :MLA_NOPE].astype(o_ref.dtype)
        tail = _rope64(yh[:, MLA_NOPE:], c_ref[...], s1_ref[...], s2_ref[...])
        o_ref[:, lo + MLA_NOPE:lo + MLA_HEAD_PAD] = tail.astype(o_ref.dtype)


def mla_q_project(cfull, q_lora_g, w_bf, gain, c_t, s1_t, s2_t, *, tm, tn):
    t = cfull.shape[0]
    kq, n = w_bf.shape
    return pl.pallas_call(
        _mla_q_kernel,
        grid=(t // tm, n // tn),
        in_specs=[pl.BlockSpec((tm, kq), lambda i, j: (i, 0)),
                  pl.BlockSpec((1, kq), lambda i, j: (0, 0)),
                  pl.BlockSpec((kq, tn), lambda i, j: (0, j)),
                  pl.BlockSpec((1, tn), lambda i, j: (0, j)),
                  pl.BlockSpec((tm, LANES), lambda i, j: (i, 0)),
                  pl.BlockSpec((tm, LANES), lambda i, j: (i, 0)),
                  pl.BlockSpec((tm, LANES), lambda i, j: (i, 0))],
        out_specs=pl.BlockSpec((tm, tn), lambda i, j: (i, j)),
        out_shape=jax.ShapeDtypeStruct((t, n), BF16),
        compiler_params=_cparams("arbitrary", "arbitrary"),
        name="mla_q_project",
    )(cfull, q_lora_g.reshape(1, kq), w_bf, gain, c_t, s1_t, s2_t)


def _mla_kv_kernel(ckv_ref, g_ref, kr_ref, w_ref, gn_ref, gt_ref, c_ref, s1_ref, s2_ref, k_ref, v_ref):
    h = _rms(ckv_ref[...], g_ref[...]).astype(BF16)
    y = jnp.dot(h, w_ref[...], preferred_element_type=F32)
    kr = kr_ref[...]
    ss_rope = jnp.sum(kr * kr, axis=-1, keepdims=True)
    rot = _rope64(kr * gt_ref[...], c_ref[...], s1_ref[...], s2_ref[...])
    per_head = MLA_NOPE + MLA_V
    for hh in range(y.shape[1] // per_head):
        kn = y[:, hh * per_head:hh * per_head + MLA_NOPE]
        ms = (jnp.sum(kn * kn, axis=-1, keepdims=True) + ss_rope) * (1.0 / MLA_QK)
        rs = lax.rsqrt(ms + NORM_EPS)
        lo = hh * MLA_HEAD_PAD
        k_ref[:, lo:lo + MLA_NOPE] = (kn * rs * gn_ref[...]).astype(k_ref.dtype)
        k_ref[:, lo + MLA_NOPE:lo + MLA_HEAD_PAD] = (rot * rs).astype(k_ref.dtype)
        v_ref[:, hh * MLA_V:(hh + 1) * MLA_V] = y[:, hh * per_head + MLA_NOPE:(hh + 1) * per_head].astype(v_ref.dtype)


def mla_kv_project(cfull, kv_lora_g, w_bf, gain_n, gain_t, c_t, s1_t, s2_t, *, tm, q_lora, heads_per_step):
    t = cfull.shape[0]
    kkv, n = w_bf.shape
    per_head = MLA_NOPE + MLA_V
    tn = heads_per_step * per_head
    heads = n // per_head
    return pl.pallas_call(
        _mla_kv_kernel,
        grid=(t // tm, n // tn),
        in_specs=[pl.BlockSpec((tm, kkv), lambda i, j: (i, q_lora // kkv)),
                  pl.BlockSpec((1, kkv), lambda i, j: (0, 0)),
                  pl.BlockSpec((tm, LANES), lambda i, j: (i, (q_lora + kkv) // LANES)),
                  pl.BlockSpec((kkv, tn), lambda i, j: (0, j)),
                  pl.BlockSpec((1, LANES), lambda i, j: (0, 0)),
                  pl.BlockSpec((1, LANES), lambda i, j: (0, 0)),
                  pl.BlockSpec((tm, LANES), lambda i, j: (i, 0)),
                  pl.BlockSpec((tm, LANES), lambda i, j: (i, 0)),
                  pl.BlockSpec((tm, LANES), lambda i, j: (i, 0))],
        out_specs=[pl.BlockSpec((tm, heads_per_step * MLA_HEAD_PAD), lambda i, j: (i, j)),
                   pl.BlockSpec((tm, heads_per_step * MLA_V), lambda i, j: (i, j))],
        out_shape=[jax.ShapeDtypeStruct((t, heads * MLA_HEAD_PAD), BF16),
                   jax.ShapeDtypeStruct((t, heads * MLA_V), BF16)],
        compiler_params=_cparams("arbitrary", "arbitrary"),
        name="mla_kv_project",
    )(cfull, kv_lora_g.reshape(1, kkv), cfull, w_bf, gain_n, gain_t, c_t, s1_t, s2_t)


def _proj_res_kernel(o_ref, w_ref, x_ref, gate_ref, out_ref):
    y = jnp.dot(o_ref[...], w_ref[...], preferred_element_type=F32)
    out_ref[...] = x_ref[...] + gate_ref[0] * y


def proj_residual(o, w_bf, xs, mods, layer, *, tm, tn, seq, batch):
    t, d = xs.shape
    k = o.shape[1]

    def seg(i):
        return jnp.minimum((i * tm) // seq, batch)

    return pl.pallas_call(
        _proj_res_kernel,
        grid=(t // tm, d // tn),
        in_specs=[pl.BlockSpec((tm, k), lambda i, j: (i, 0)),
                  pl.BlockSpec((k, tn), lambda i, j: (0, j)),
                  pl.BlockSpec((tm, tn), lambda i, j: (i, j)),
                  pl.BlockSpec((1, 1, tn), lambda i, j: ((layer * MOD_ROWS + seg(i)) * 6 + 2, 0, j))],
        out_specs=pl.BlockSpec((tm, tn), lambda i, j: (i, j)),
        out_shape=jax.ShapeDtypeStruct((t, d), F32),
        compiler_params=_cparams("arbitrary", "arbitrary"),
        name="proj_residual",
    )(o, w_bf, xs, mods)


def _softmax_step(q, k, v, m, l, acc):
    s = lax.dot_general(q, k, (((1,), (1,)), ((), ())), preferred_element_type=F32)
    m_new = jnp.maximum(m, jnp.max(s, axis=-1, keepdims=True))
    alpha = jnp.exp(m - m_new)
    p = jnp.exp(s - m_new)
    l = alpha * l + jnp.sum(p, axis=-1, keepdims=True)
    acc = alpha * acc + jnp.dot(p.astype(v.dtype), v, preferred_element_type=F32)
    return m_new, l, acc


def _flash_kernel(*refs, n_lat_chunks, tk):
    if n_lat_chunks:
        q_ref, kc_ref, vc_ref, k_ref, v_ref, _, o_ref = refs
    else:
        q_ref, kc_ref, vc_ref, _, o_ref = refs
    q = q_ref[...]
    tq = q.shape[0]
    m = jnp.full((tq, 1), -jnp.inf, F32)
    l = jnp.zeros((tq, 1), F32)
    acc = jnp.zeros((tq, vc_ref.shape[1]), F32)
    m, l, acc = _softmax_step(q, kc_ref[...], vc_ref[...], m, l, acc)
    if n_lat_chunks:
        def body(c, carry):
            r0 = pl.multiple_of(c * tk, tk)
            return _softmax_step(q, k_ref[pl.ds(r0, tk), :], v_ref[pl.ds(r0, tk), :], *carry)
        m, l, acc = lax.fori_loop(0, n_lat_chunks, body, (m, l, acc))
    o_ref[...] = (acc / l).astype(o_ref.dtype)


def dense_attention(qa, ka, va, *, batch, seq, ctx, heads, dk, dv, q_col0, k_col0, v_col0, kv_group, tq, tk):
    t = qa.shape[0]
    ctx_blk0 = (batch * seq) // ctx
    nq = seq // tq
    o_shape = jax.ShapeDtypeStruct((t, heads * dv), BF16)
    o_ctx = pl.pallas_call(
        functools.partial(_flash_kernel, n_lat_chunks=0, tk=tk),
        grid=(batch, heads),
        in_specs=[pl.BlockSpec((ctx, dk), lambda b, h: (ctx_blk0 + b, q_col0 + h)),
                  pl.BlockSpec((ctx, dk), lambda b, h: (ctx_blk0 + b, k_col0 + h // kv_group)),
                  pl.BlockSpec((ctx, dv), lambda b, h: (ctx_blk0 + b, v_col0 + h // kv_group)),
                  pl.BlockSpec(memory_space=pl.ANY)],
        out_specs=pl.BlockSpec((ctx, dv), lambda b, h: (ctx_blk0 + b, h)),
        out_shape=o_shape,
        input_output_aliases={3: 0},
        compiler_params=_cparams("arbitrary", "arbitrary"),
        name="ctx_attention",
    )(qa, ka, va, jnp.zeros(o_shape.shape, o_shape.dtype))
    return pl.pallas_call(
        functools.partial(_flash_kernel, n_lat_chunks=seq // tk, tk=tk),
        grid=(batch, heads, nq),
        in_specs=[pl.BlockSpec((tq, dk), lambda b, h, i: (b * nq + i, q_col0 + h)),
                  pl.BlockSpec((ctx, dk), lambda b, h, i: (ctx_blk0 + b, k_col0 + h // kv_group)),
                  pl.BlockSpec((ctx, dv), lambda b, h, i: (ctx_blk0 + b, v_col0 + h // kv_group)),
                  pl.BlockSpec((seq, dk), lambda b, h, i: (b, k_col0 + h // kv_group)),
                  pl.BlockSpec((seq, dv), lambda b, h, i: (b, v_col0 + h // kv_group)),
                  pl.BlockSpec(memory_space=pl.ANY)],
        out_specs=pl.BlockSpec((tq, dv), lambda b, h, i: (b * nq + i, h)),
        out_shape=o_shape,
        input_output_aliases={5: 0},
        compiler_params=_cparams("arbitrary", "arbitrary", "arbitrary"),
        name="dense_attention",
    )(qa, ka, va, ka, va, o_ctx)


def _na_kernel(q_ref, k_ref, v_ref, kc_ref, vc_ref, tb_ref, _, o_ref, *, rows, kr):
    kc = kc_ref[...]
    vc = vc_ref[...]
    nt = (((1,), (1,)), ((), ()))

    def body(r, carry):
        rs = jnp.clip(r - kr // 2, 0, rows - kr)
        q0 = pl.multiple_of(r * GRID_W, GRID_W)
        k0 = pl.multiple_of(rs * GRID_W, GRID_W)
        q = q_ref[pl.ds(q0, GRID_W), :]
        kw = k_ref[pl.ds(k0, kr * GRID_W), :]
        vw = v_ref[pl.ds(k0, kr * GRID_W), :]
        s_win = lax.dot_general(q, kw, nt, preferred_element_type=F32) + tb_ref[0, r - rs]
        s_ctx = lax.dot_general(q, kc, nt, preferred_element_type=F32)
        m = jnp.maximum(jnp.max(s_win, axis=-1, keepdims=True), jnp.max(s_ctx, axis=-1, keepdims=True))
        p_win = jnp.exp(s_win - m)
        p_ctx = jnp.exp(s_ctx - m)
        l = jnp.sum(p_win, axis=-1, keepdims=True) + jnp.sum(p_ctx, axis=-1, keepdims=True)
        o = (jnp.dot(p_win.astype(vw.dtype), vw, preferred_element_type=F32)
             + jnp.dot(p_ctx.astype(vc.dtype), vc, preferred_element_type=F32))
        o_ref[pl.ds(q0, GRID_W), :] = (o / l).astype(o_ref.dtype)
        return carry

    lax.fori_loop(0, rows, body, 0)


def na_attention(qkv, tb, o_ctx, *, batch, seq, ctx, heads):
    t = qkv.shape[0]
    rows = seq // GRID_W
    kr = min(NA_ROWS, rows)
    ctx_blk0 = (batch * seq) // ctx
    d = HEAD_DIM
    return pl.pallas_call(
        functools.partial(_na_kernel, rows=rows, kr=kr),
        grid=(batch, heads),
        in_specs=[pl.BlockSpec((seq, d), lambda b, h: (b, h)),
                  pl.BlockSpec((seq, d), lambda b, h: (b, heads + h)),
                  pl.BlockSpec((seq, d), lambda b, h: (b, 2 * heads + h)),
                  pl.BlockSpec((ctx, d), lambda b, h: (ctx_blk0 + b, heads + h)),
                  pl.BlockSpec((ctx, d), lambda b, h: (ctx_blk0 + b, 2 * heads + h)),
                  pl.BlockSpec((1, kr, GRID_W, kr * GRID_W), lambda b, h: (h, 0, 0, 0)),
                  pl.BlockSpec(memory_space=pl.ANY)],
        out_specs=pl.BlockSpec((seq, d), lambda b, h: (b, h)),
        out_shape=jax.ShapeDtypeStruct((t, heads * d), BF16),
        input_output_aliases={6: 0},
        compiler_params=_cparams("arbitrary", "arbitrary"),
        name="na_attention",
    )(qkv, qkv, qkv, qkv, qkv, tb, o_ctx)


def ctx_only_attention(qa, *, batch, seq, ctx, heads, dk, dv, q_col0, k_col0, v_col0):
    t = qa.shape[0]
    ctx_blk0 = (batch * seq) // ctx
    o_shape = jax.ShapeDtypeStruct((t, heads * dv), BF16)
    return pl.pallas_call(
        functools.partial(_flash_kernel, n_lat_chunks=0, tk=0),
        grid=(batch, heads),
        in_specs=[pl.BlockSpec((ctx, dk), lambda b, h: (ctx_blk0 + b, q_col0 + h)),
                  pl.BlockSpec((ctx, dk), lambda b, h: (ctx_blk0 + b, k_col0 + h)),
                  pl.BlockSpec((ctx, dv), lambda b, h: (ctx_blk0 + b, v_col0 + h)),
                  pl.BlockSpec(memory_space=pl.ANY)],
        out_specs=pl.BlockSpec((ctx, dv), lambda b, h: (ctx_blk0 + b, h)),
        out_shape=o_shape,
        input_output_aliases={3: 0},
        compiler_params=_cparams("arbitrary", "arbitrary"),
        name="ctx_attention_na",
    )(qa, qa, qa, jnp.zeros(o_shape.shape, o_shape.dtype))


def _router_kernel(x_ref, g_ref, sh_ref, sc_ref, w_ref, b_ref, h_ref, route_ref, *, n_groups, epg):
    h = _rms(x_ref[...], g_ref[...]) * (1.0 + sc_ref[0]) + sh_ref[0]
    h_ref[...] = h
    w = w_ref[...]
    h_hi = h.astype(BF16)
    h_lo = (h - h_hi.astype(F32)).astype(BF16)
    w_hi = w.astype(BF16)
    w_lo = (w - w_hi.astype(F32)).astype(BF16)
    logits = (jnp.dot(h_hi, w_hi, preferred_element_type=F32)
              + jnp.dot(h_lo, w_hi, preferred_element_type=F32)
              + jnp.dot(h_hi, w_lo, preferred_element_type=F32)) + b_ref[...]
    lane = lax.broadcasted_iota(jnp.int32, logits.shape, 1).astype(F32)
    is_g = lane < n_groups
    g_max = jnp.max(jnp.where(is_g, logits, -jnp.inf), axis=-1, keepdims=True)
    g_sum = jnp.sum(jnp.where(is_g, jnp.exp(logits - g_max), 0.0), axis=-1, keepdims=True)
    g_sel = jnp.min(jnp.where(is_g & (logits == g_max), lane, float(LANES)), axis=-1, keepdims=True)
    lo = n_groups + epg * g_sel
    in_grp = (lane >= lo) & (lane < lo + epg)
    t1 = jnp.max(jnp.where(in_grp, logits, -jnp.inf), axis=-1, keepdims=True)
    i1 = jnp.min(jnp.where(in_grp & (logits == t1), lane, float(LANES)), axis=-1, keepdims=True)
    rest = in_grp & (lane != i1)
    t2 = jnp.max(jnp.where(rest, logits, -jnp.inf), axis=-1, keepdims=True)
    i2 = jnp.min(jnp.where(rest & (logits == t2), lane, float(LANES)), axis=-1, keepdims=True)
    d = jnp.exp(t2 - t1)
    gate = 1.0 / g_sum
    w1 = gate * (1.0 / (1.0 + d))
    w2 = gate * (d / (1.0 + d))
    route_ref[...] = jnp.where(lane == 0, i1 - n_groups,
                               jnp.where(lane == 1, i2 - n_groups,
                                         jnp.where(lane == 2, w1, jnp.where(lane == 3, w2, 0.0))))


def moe_router(xs, mods, layer, norm_g, w_r, b_r, *, tm, seq, batch, n_groups, epg):
    t, d = xs.shape
    sh_spec, sc_spec = _mod_specs(layer, (3, 4), tm, seq, batch, d)
    return pl.pallas_call(
        functools.partial(_router_kernel, n_groups=n_groups, epg=epg),
        grid=(t // tm, 1),
        in_specs=[pl.BlockSpec((tm, d), lambda i, j: (i, 0)),
                  pl.BlockSpec((1, d), lambda i, j: (0, 0)),
                  sh_spec, sc_spec,
                  pl.BlockSpec((d, LANES), lambda i, j: (0, 0)),
                  pl.BlockSpec((1, LANES), lambda i, j: (0, 0))],
        out_specs=[pl.BlockSpec((tm, d), lambda i, j: (i, 0)),
                   pl.BlockSpec((tm, LANES), lambda i, j: (i, 0))],
        out_shape=[jax.ShapeDtypeStruct((t, d), F32), jax.ShapeDtypeStruct((t, LANES), F32)],
        compiler_params=_cparams("arbitrary", "arbitrary"),
        name="moe_router",
    )(xs, norm_g.reshape(1, d), mods, mods, w_r, b_r)


def _issue_row_gather(idx_ref, src_hbm, dst, sem, n_rows):
    def body(r, carry):
        tok = idx_ref[0, 0, r]
        pltpu.make_async_copy(src_hbm.at[pl.ds(tok, 1)], dst.at[pl.ds(r, 1)], sem).start()
        return carry
    lax.fori_loop(0, n_rows, body, 0)


def _wait_row_gather(src_hbm, dst, sem, n_rows):
    pltpu.make_async_copy(src_hbm.at[pl.ds(0, n_rows)], dst, sem).wait()


def _expert_kernel(be_ref, idx_ref, idx_next_ref, h_hbm, win_ref, wout_ref, y_ref,
                   xbuf, sem, win_bf, wout_bf, *, cast_rows):
    i = pl.program_id(0)
    nb = pl.num_programs(0)
    slot = i % 2
    blk = xbuf.shape[1]

    @pl.when(i == 0)
    def _():
        _issue_row_gather(idx_ref, h_hbm, xbuf.at[0], sem.at[0], blk)

    @pl.when(i + 1 < nb)
    def _():
        _issue_row_gather(idx_next_ref, h_hbm, xbuf.at[1 - slot], sem.at[1 - slot], blk)

    expert_changed = (i == 0) | (be_ref[i] != be_ref[jnp.maximum(i - 1, 0)])

    @pl.when(expert_changed)
    def _():
        def cast_in(c, carry):
            r0 = pl.multiple_of(c * cast_rows, cast_rows)
            win_bf[pl.ds(r0, cast_rows), :] = win_ref[0, pl.ds(r0, cast_rows), :].astype(BF16)
            return carry
        lax.fori_loop(0, win_bf.shape[0] // cast_rows, cast_in, 0)

        def cast_out(c, carry):
            r0 = pl.multiple_of(c * cast_rows, cast_rows)
            wout_bf[pl.ds(r0, cast_rows), :] = wout_ref[0, pl.ds(r0, cast_rows), :].astype(BF16)
            return carry
        lax.fori_loop(0, wout_bf.shape[0] // cast_rows, cast_out, 0)

    _wait_row_gather(h_hbm, xbuf.at[slot], sem.at[slot], blk)
    x = xbuf[slot].astype(BF16)
    gu = jnp.dot(x, win_bf[...], preferred_element_type=F32)
    f = gu.shape[1] // 2
    act = _silu(gu[:, :f]) * gu[:, f:]
    y_ref[...] = jnp.dot(act.astype(BF16), wout_bf[...], preferred_element_type=F32)


def moe_experts(h, slot_tok, block_expert, w_in, w_out):
    t, d = h.shape
    n_exp, _, f2 = w_in.shape
    f = f2 // 2
    nb = block_expert.shape[0]
    blk = MOE_BLOCK
    idx = slot_tok.reshape(nb, 1, blk)
    grid_spec = pltpu.PrefetchScalarGridSpec(
        num_scalar_prefetch=1,
        grid=(nb,),
        in_specs=[pl.BlockSpec((1, 1, blk), lambda i, be: (i, 0, 0), memory_space=pltpu.SMEM),
                  pl.BlockSpec((1, 1, blk), lambda i, be: (jnp.minimum(i + 1, nb - 1), 0, 0),
                               memory_space=pltpu.SMEM),
                  pl.BlockSpec(memory_space=pl.ANY),
                  pl.BlockSpec((1, d, f2), lambda i, be: (be[i], 0, 0)),
                  pl.BlockSpec((1, f, d), lambda i, be: (be[i], 0, 0))],
        out_specs=pl.BlockSpec((blk, d), lambda i, be: (i, 0)),
        scratch_shapes=[pltpu.VMEM((2, blk, d), F32),
                        pltpu.SemaphoreType.DMA((2,)),
                        pltpu.VMEM((d, f2), BF16),
                        pltpu.VMEM((f, d), BF16)],
    )
    return pl.pallas_call(
        functools.partial(_expert_kernel, cast_rows=256),
        grid_spec=grid_spec,
        out_shape=jax.ShapeDtypeStruct((nb * blk, d), F32),
        compiler_params=_cparams("arbitrary"),
        name="moe_experts",
    )(block_expert, idx, idx, h, w_in, w_out)


def _combine_kernel(pos_ref, pos_next_ref, yb_hbm, x_ref, route_ref, gate_ref, o_ref, ybuf, sem):
    i = pl.program_id(0)
    nt = pl.num_programs(0)
    slot = i % 2
    n_rows = ybuf.shape[1]
    tm = n_rows // MOE_TOP_K

    @pl.when(i == 0)
    def _():
        _issue_row_gather(pos_ref, yb_hbm, ybuf.at[0], sem.at[0], n_rows)

    @pl.when(i + 1 < nt)
    def _():
        _issue_row_gather(pos_next_ref, yb_hbm, ybuf.at[1 - slot], sem.at[1 - slot], n_rows)

    _wait_row_gather(yb_hbm, ybuf.at[slot], sem.at[slot], n_rows)
    route = route_ref[...]
    moe = route[:, 2:3] * ybuf[slot, :tm, :] + route[:, 3:4] * ybuf[slot, tm:, :]
    o_ref[...] = x_ref[...] + gate_ref[0] * moe


def moe_combine(yb, pos, xs, route, mods, layer, *, tm, seq, batch):
    t, d = xs.shape
    nt = t // tm
    pos_t = pos.reshape(nt, tm, MOE_TOP_K).transpose(0, 2, 1).reshape(nt, 1, MOE_TOP_K * tm)

    def seg(i):
        return jnp.minimum((i * tm) // seq, batch)

    return pl.pallas_call(
        _combine_kernel,
        grid=(nt,),
        in_specs=[pl.BlockSpec((1, 1, MOE_TOP_K * tm), lambda i: (i, 0, 0), memory_space=pltpu.SMEM),
                  pl.BlockSpec((1, 1, MOE_TOP_K * tm), lambda i: (jnp.minimum(i + 1, nt - 1), 0, 0),
                               memory_space=pltpu.SMEM),
                  pl.BlockSpec(memory_space=pl.ANY),
                  pl.BlockSpec((tm, d), lambda i: (i, 0)),
                  pl.BlockSpec((tm, LANES), lambda i: (i, 0)),
                  pl.BlockSpec((1, 1, d), lambda i: ((layer * MOD_ROWS + seg(i)) * 6 + 5, 0, 0))],
        out_specs=pl.BlockSpec((tm, d), lambda i: (i, 0)),
        out_shape=jax.ShapeDtypeStruct((t, d), F32),
        scratch_shapes=[pltpu.VMEM((2, MOE_TOP_K * tm, d), F32), pltpu.SemaphoreType.DMA((2,))],
        compiler_params=_cparams("arbitrary"),
        name="moe_combine",
    )(pos_t, pos_t, yb, xs, route, mods)


def _dispatch_plan(route, n_experts, blk):
    t = route.shape[0]
    nk = t * MOE_TOP_K
    flat_e = route[:, :MOE_TOP_K].astype(jnp.int32).reshape(nk)
    onehot = (flat_e[:, None] == jnp.arange(n_experts, dtype=jnp.int32)[None, :]).astype(jnp.int32)
    cum = jnp.cumsum(onehot, axis=0)
    rank = jnp.take_along_axis(cum, flat_e[:, None], axis=1)[:, 0] - 1
    counts = cum[-1]
    padded = (counts + blk - 1) // blk * blk
    pad_end = jnp.cumsum(padded)
    pad_start = pad_end - padded
    nb = -(-(nk + n_experts * (blk - 1)) // blk)
    dest = pad_start[flat_e] + rank
    slot_tok = jnp.zeros((nb * blk,), jnp.int32).at[dest].set(jnp.arange(nk, dtype=jnp.int32) // MOE_TOP_K)
    block_expert = jnp.minimum(
        jnp.searchsorted(pad_end, jnp.arange(nb, dtype=jnp.int32) * blk, side='right'),
        n_experts - 1).astype(jnp.int32)
    return slot_tok, block_expert, dest.reshape(t, MOE_TOP_K)


def _axial_angles(seq, dim):
    n_freq = dim // 4
    freqs = ROPE_THETA ** (-jnp.arange(n_freq, dtype=F32) / n_freq)
    tok = jnp.arange(seq, dtype=jnp.int32)
    row = (tok // GRID_W).astype(F32)
    col = (tok % GRID_W).astype(F32)
    return jnp.concatenate([row[:, None] * freqs, col[:, None] * freqs], axis=-1)


def _stream_table(lat, ctx_fill, batch, n_ctx_rows):
    ctx_rows = jnp.broadcast_to(ctx_fill[None, :], (n_ctx_rows, lat.shape[1]))
    return jnp.concatenate([jnp.tile(lat, (batch, 1)), ctx_rows], axis=0)


def _rope_tables_128(seq, batch, n_ctx_rows):
    ang = _axial_angles(seq, HEAD_DIM)
    cos, sin = jnp.cos(ang), jnp.sin(ang)
    c = jnp.concatenate([cos, cos], axis=-1)
    s = jnp.concatenate([-sin, sin], axis=-1)
    return (_stream_table(c, jnp.ones((HEAD_DIM,), F32), batch, n_ctx_rows),
            _stream_table(s, jnp.zeros((HEAD_DIM,), F32), batch, n_ctx_rows))


def _rope_tables_64(seq, batch, n_ctx_rows):
    ang = _axial_angles(seq, MLA_ROPE)
    cos, sin = jnp.cos(ang), jnp.sin(ang)
    half = MLA_ROPE // 2
    z = jnp.zeros((seq, LANES - MLA_ROPE), F32)
    zh = jnp.zeros((seq, half), F32)
    c = jnp.concatenate([cos, cos, z], axis=-1)
    s1 = jnp.concatenate([-sin, zh, z], axis=-1)
    s2 = jnp.concatenate([zh, sin, z], axis=-1)
    ones_pad = jnp.concatenate([jnp.ones((MLA_ROPE,), F32), jnp.zeros((LANES - MLA_ROPE,), F32)])
    zeros = jnp.zeros((LANES,), F32)
    return (_stream_table(c, ones_pad, batch, n_ctx_rows),
            _stream_table(s1, zeros, batch, n_ctx_rows),
            _stream_table(s2, zeros, batch, n_ctx_rows))


def _na_bias_table(rpb, kr):
    col = jnp.arange(GRID_W, dtype=jnp.int32)
    col_start = jnp.clip(col - NA_COLS // 2, 0, GRID_W - NA_COLS)
    col_in = (col[None, :] >= col_start[:, None]) & (col[None, :] < col_start[:, None] + NA_COLS)
    dc = jnp.clip(col[None, :] - col[:, None] + NA_COLS - 1, 0, 2 * NA_COLS - 2)
    shift = jnp.arange(kr, dtype=jnp.int32)
    dr = jnp.arange(kr, dtype=jnp.int32)[None, :] - shift[:, None] + NA_ROWS - 1
    bias = rpb[:, dr[:, :, None, None], dc[None, None, :, :]]
    bias = jnp.where(col_in[None, None, None], bias, NEG_INF)
    h = rpb.shape[0]
    return bias.transpose(0, 1, 3, 2, 4).reshape(h, kr, GRID_W, kr * GRID_W).astype(F32)


def _pick_tile(*extents):
    for tm in (512, 256, 128):
        if all(e % tm == 0 for e in extents):
            return tm
    raise ValueError("row extents must be multiples of 128")


def kernel(x, c, ctx, c_ctx, mod_w, mod_b, norm_g, gqa_w_qkv, gqa_q_norm, gqa_k_norm, gqa_w_o, na_w_qkv, na_q_norm, na_k_norm, na_rpb, na_w_o, mla_w_down, mla_q_lora_norm, mla_w_uq, mla_kv_lora_norm, mla_w_ukv, mla_q_norm, mla_k_norm, mla_w_o, moe_w_group, moe_b_group, moe_w_expert, moe_b_expert, moe_w_in, moe_w_out):
    batch, seq, d = x.shape
    n_ctx = ctx.shape[1]
    depth = mod_w.shape[0]
    heads = d // HEAD_DIM
    kv_heads = heads // 4
    n_lat = batch * seq
    n_ctx_rows = batch * n_ctx
    assert batch + 1 <= MOD_ROWS and n_lat % n_ctx == 0 and seq % GRID_W == 0
    tm = _pick_tile(seq, n_ctx_rows)
    tq = _pick_tile(seq)
    tk = tq
    tn_d = min(512, d)
    dims = dict(tm=tm, seq=seq, batch=batch)

    xs = jnp.concatenate([x.reshape(n_lat, d), ctx.reshape(n_ctx_rows, d)], axis=0)

    cond = jnp.zeros((MOD_ROWS, d), F32).at[:batch].set(c).at[batch].set(c_ctx)
    mods = adaln_all(cond, mod_w, mod_b).reshape(depth * MOD_ROWS * 6, 1, d)

    cos_a, sin_a = _rope_tables_128(seq, batch, n_ctx_rows)
    c64, s64a, s64b = _rope_tables_64(seq, batch, n_ctx_rows)

    n_groups = moe_w_group.shape[-1]
    epg = moe_w_expert.shape[-1]
    n_experts = n_groups * epg
    assert n_groups + n_experts <= LANES

    for i in range(depth):
        kind, j = i % 3, i // 3
        if kind == 0:
            scale = HEAD_DIM ** -0.5
            nq, nkv = heads * HEAD_DIM, kv_heads * HEAD_DIM
            gain = jnp.concatenate([jnp.tile(gqa_q_norm[j] * scale, heads), jnp.tile(gqa_k_norm[j], kv_heads),
                                    jnp.ones((nkv,), F32)]).reshape(1, -1)
            qkv = qkv_project(xs, mods, i, norm_g[i, 0], gqa_w_qkv[j].astype(BF16), gain, cos_a, sin_a,
                              n_norm_cols=nq + nkv, tn=min(512, nkv), rope=True, **dims)
            o = dense_attention(qkv, qkv, qkv, batch=batch, seq=seq, ctx=n_ctx, heads=heads, dk=HEAD_DIM,
                                dv=HEAD_DIM, q_col0=0, k_col0=heads, v_col0=heads + kv_heads,
                                kv_group=heads // kv_heads, tq=tq, tk=tk)
            w_o = gqa_w_o[j]
        elif kind == 1:
            scale = HEAD_DIM ** -0.5
            nq = heads * HEAD_DIM
            gain = jnp.concatenate([jnp.tile(na_q_norm[j] * scale, heads), jnp.tile(na_k_norm[j], heads),
                                    jnp.ones((nq,), F32)]).reshape(1, -1)
            qkv = qkv_project(xs, mods, i, norm_g[i, 0], na_w_qkv[j].astype(BF16), gain, cos_a, sin_a,
                              n_norm_cols=2 * nq, tn=tn_d, rope=False, **dims)
            rows = seq // GRID_W
            tb = _na_bias_table(na_rpb[j], min(NA_ROWS, rows))
            o_ctx = ctx_only_attention(qkv, batch=batch, seq=seq, ctx=n_ctx, heads=heads, dk=HEAD_DIM, dv=HEAD_DIM,
                                       q_col0=0, k_col0=heads, v_col0=2 * heads)
            o = na_attention(qkv, tb, o_ctx, batch=batch, seq=seq, ctx=n_ctx, heads=heads)
            w_o = na_w_o[j]
        else:
            scale = MLA_QK ** -0.5
            q_lora = mla_q_lora_norm.shape[-1]
            kv_lora = mla_kv_lora_norm.shape[-1]
            assert q_lora % kv_lora == 0 and (q_lora + kv_lora) % LANES == 0
            n_down = q_lora + kv_lora + LANES
            w_down = jnp.pad(mla_w_down[j], ((0, 0), (0, n_down - mla_w_down.shape[-1]))).astype(BF16)
            cfull = mla_down(xs, mods, i, norm_g[i, 0], w_down, **dims)
            pad = MLA_HEAD_PAD - MLA_QK
            w_uq = jnp.pad(mla_w_uq[j].reshape(q_lora, heads, MLA_QK), ((0, 0), (0, 0), (0, pad)))
            w_uq = w_uq.reshape(q_lora, heads * MLA_HEAD_PAD).astype(BF16)
            q_gain = jnp.tile(jnp.pad(mla_q_norm[j] * scale, (0, pad)), heads).reshape(1, -1)
            qa = mla_q_project(cfull, mla_q_lora_norm[j], w_uq, q_gain, c64, s64a, s64b, tm=tm,
                               tn=min(512, heads * MLA_HEAD_PAD))
            gain_n = mla_k_norm[j, :MLA_NOPE].reshape(1, LANES)
            gain_t = jnp.pad(mla_k_norm[j, MLA_NOPE:], (0, LANES - MLA_ROPE)).reshape(1, LANES)
            ka, va = mla_kv_project(cfull, mla_kv_lora_norm[j], mla_w_ukv[j].astype(BF16), gain_n, gain_t,
                                    c64, s64a, s64b, tm=tm, q_lora=q_lora, heads_per_step=2)
            o = dense_attention(qa, ka, va, batch=batch, seq=seq, ctx=n_ctx, heads=heads, dk=MLA_HEAD_PAD,
                                dv=MLA_V, q_col0=0, k_col0=0, v_col0=0, kv_group=1, tq=tq, tk=tk)
            w_o = mla_w_o[j]

        xs = proj_residual(o, w_o.astype(BF16), xs, mods, i, tn=tn_d, **dims)

        w_r = jnp.concatenate([moe_w_group[i], moe_w_expert[i].transpose(1, 0, 2).reshape(d, n_experts)], axis=1)
        w_r = jnp.pad(w_r, ((0, 0), (0, LANES - w_r.shape[1])))
        b_r = jnp.pad(jnp.concatenate([moe_b_group[i], moe_b_expert[i].reshape(-1)]),
                      (0, LANES - n_groups - n_experts)).reshape(1, LANES)
        h2, route = moe_router(xs, mods, i, norm_g[i, 1], w_r, b_r, n_groups=n_groups, epg=epg, **dims)
        slot_tok, block_expert, pos = _dispatch_plan(route, n_experts, MOE_BLOCK)
        yb = moe_experts(h2, slot_tok, block_expert, moe_w_in[i], moe_w_out[i])
        xs = moe_combine(yb, pos, xs, route, mods, i, tm=min(COMBINE_TM, tm), seq=seq, batch=batch)

    return xs[:n_lat].reshape(batch, seq, d)
```

```python
import functools

import jax
import jax.numpy as jnp
from jax import lax
from jax.experimental import pallas as pl
from jax.experimental.pallas import tpu as pltpu

F32 = jnp.float32
BF16 = jnp.bfloat16

GRID_W = 64
HEAD_DIM = 128
ROPE_THETA = 10000.0
NORM_EPS = 1e-6
NEG_INF = -1e30
NA_ROWS = 8
NA_COLS = 16
MLA_NOPE = 128
MLA_ROPE = 64
MLA_V = 128
MLA_QK = MLA_NOPE + MLA_ROPE
MLA_HEAD_PAD = 256
MOE_TOP_K = 2
LANES = 128
MOD_ROWS = 8
MOE_BLOCK = 256
COMBINE_TM = 256
FLASH_TQ = 512
FLASH_TK = 512
FLASH_ROW_GROUP = 32
LOG2E = 1.4426950408889634
VMEM_LIMIT = 52 * 1024 * 1024


def _cparams(*sem):
    return pltpu.CompilerParams(dimension_semantics=sem, vmem_limit_bytes=VMEM_LIMIT)


def _silu(v):
    return v / (1.0 + jnp.exp(-v))


def _adaln_kernel(c_ref, w_ref, b_ref, o_ref):
    cond = _silu(c_ref[...]).astype(BF16)
    o_ref[0] = jnp.dot(cond, w_ref[0].astype(BF16), preferred_element_type=F32) + b_ref[0]


def adaln_all(cond, mod_w, mod_b):
    depth, d, n = mod_w.shape
    tn = 1024
    return pl.pallas_call(
        _adaln_kernel,
        grid=(depth, n // tn),
        in_specs=[pl.BlockSpec((MOD_ROWS, d), lambda l, j: (0, 0)),
                  pl.BlockSpec((1, d, tn), lambda l, j: (l, 0, j)),
                  pl.BlockSpec((1, 1, tn), lambda l, j: (l, 0, j))],
        out_specs=pl.BlockSpec((1, MOD_ROWS, tn), lambda l, j: (l, 0, j)),
        out_shape=jax.ShapeDtypeStruct((depth, MOD_ROWS, n), F32),
        compiler_params=_cparams("arbitrary", "arbitrary"),
        name="adaln",
    )(cond, mod_w, mod_b.reshape(depth, 1, n))


def _rms(x, g):
    ms = jnp.mean(x * x, axis=-1, keepdims=True)
    return x * lax.rsqrt(ms + NORM_EPS) * g


def _rope128(y, c, s):
    return y * c + pltpu.roll(y, 64, 1) * s


def _rope64(y, c, s1, s2):
    return y * c + pltpu.roll(y, 96, 1) * s1 + pltpu.roll(y, 32, 1) * s2


def _qkv_kernel(x_ref, g_ref, sh_ref, sc_ref, w_ref, gain_ref, c_ref, s_ref, o_ref, h_scr, *, n_norm_tiles, rope):
    j = pl.program_id(1)

    @pl.when(j == 0)
    def _():
        h = _rms(x_ref[...], g_ref[...]) * (1.0 + sc_ref[0]) + sh_ref[0]
        h_scr[...] = h.astype(BF16)

    y = jnp.dot(h_scr[...], w_ref[...], preferred_element_type=F32)
    n_heads = y.shape[1] // HEAD_DIM

    @pl.when(j < n_norm_tiles)
    def _():
        for hh in range(n_heads):
            sl = slice(hh * HEAD_DIM, (hh + 1) * HEAD_DIM)
            yh = _rms(y[:, sl], gain_ref[:, sl])
            if rope:
                yh = _rope128(yh, c_ref[...], s_ref[...])
            o_ref[:, sl] = yh.astype(o_ref.dtype)

    @pl.when(j >= n_norm_tiles)
    def _():
        o_ref[...] = y.astype(o_ref.dtype)


def _mod_specs(layer, which, tm, seq, batch, d):
    def seg(i):
        return jnp.minimum((i * tm) // seq, batch)
    return [pl.BlockSpec((1, 1, d), lambda i, j, w=w: ((layer * MOD_ROWS + seg(i)) * 6 + w, 0, 0)) for w in which]


def qkv_project(xs, mods, layer, norm_g, w_bf, gain, cos_t, sin_t, *, tm, seq, batch, n_norm_cols, tn, rope):
    t, d = xs.shape
    n = w_bf.shape[1]
    sh_spec, sc_spec = _mod_specs(layer, (0, 1), tm, seq, batch, d)
    kern = functools.partial(_qkv_kernel, n_norm_tiles=n_norm_cols // tn, rope=rope)
    return pl.pallas_call(
        kern,
        grid=(t // tm, n // tn),
        in_specs=[pl.BlockSpec((tm, d), lambda i, j: (i, 0)),
                  pl.BlockSpec((1, d), lambda i, j: (0, 0)),
                  sh_spec, sc_spec,
                  pl.BlockSpec((d, tn), lambda i, j: (0, j)),
                  pl.BlockSpec((1, tn), lambda i, j: (0, j)),
                  pl.BlockSpec((tm, HEAD_DIM), lambda i, j: (i, 0)),
                  pl.BlockSpec((tm, HEAD_DIM), lambda i, j: (i, 0))],
        out_specs=pl.BlockSpec((tm, tn), lambda i, j: (i, j)),
        out_shape=jax.ShapeDtypeStruct((t, n), BF16),
        scratch_shapes=[pltpu.VMEM((tm, d), BF16)],
        compiler_params=_cparams("arbitrary", "arbitrary"),
        name="qkv_project",
    )(xs, norm_g.reshape(1, d), mods, mods, w_bf, gain, cos_t, sin_t)


def _down_kernel(x_ref, g_ref, sh_ref, sc_ref, w_ref, o_ref):
    h = _rms(x_ref[...], g_ref[...]) * (1.0 + sc_ref[0]) + sh_ref[0]
    o_ref[...] = jnp.dot(h.astype(BF16), w_ref[...], preferred_element_type=F32)


def mla_down(xs, mods, layer, norm_g, w_bf, *, tm, seq, batch):
    t, d = xs.shape
    n = w_bf.shape[1]
    sh_spec, sc_spec = _mod_specs(layer, (0, 1), tm, seq, batch, d)
    return pl.pallas_call(
        _down_kernel,
        grid=(t // tm, 1),
        in_specs=[pl.BlockSpec((tm, d), lambda i, j: (i, 0)),
                  pl.BlockSpec((1, d), lambda i, j: (0, 0)),
                  sh_spec, sc_spec,
                  pl.BlockSpec((d, n), lambda i, j: (0, 0))],
        out_specs=pl.BlockSpec((tm, n), lambda i, j: (i, 0)),
        out_shape=jax.ShapeDtypeStruct((t, n), F32),
        compiler_params=_cparams("arbitrary", "arbitrary"),
        name="mla_down",
    )(xs, norm_g.reshape(1, d), mods, mods, w_bf)


def _mla_q_kernel(cq_ref, g_ref, w_ref, gain_ref, c_ref, s1_ref, s2_ref, o_ref):
    h = _rms(cq_ref[...], g_ref[...]).astype(BF16)
    y = jnp.dot(h, w_ref[...], preferred_element_type=F32)
    for hh in range(y.shape[1] // MLA_HEAD_PAD):
        lo = hh * MLA_HEAD_PAD
        yh = y[:, lo:lo + MLA_HEAD_PAD]
        ms = jnp.sum(yh * yh, axis=-1, keepdims=True) * (1.0 / MLA_QK)
        yh = yh * lax.rsqrt(ms + NORM_EPS) * gain_ref[:, lo:lo + MLA_HEAD_PAD]
        o_ref[:, lo:lo + MLA_NOPE] = yh[:, :MLA_NOPE].astype(o_ref.dtype)
        tail = _rope64(yh[:, MLA_NOPE:], c_ref[...], s1_ref[...], s2_ref[...])
        o_ref[:, lo + MLA_NOPE:lo + MLA_HEAD_PAD] = tail.astype(o_ref.dtype)


def mla_q_project(cfull, q_lora_g, w_bf, gain, c_t, s1_t, s2_t, *, tm, tn):
    t = cfull.shape[0]
    kq, n = w_bf.shape
    return pl.pallas_call(
        _mla_q_kernel,
        grid=(t // tm, n // tn),
        in_specs=[pl.BlockSpec((tm, kq), lambda i, j: (i, 0)),
                  pl.BlockSpec((1, kq), lambda i, j: (0, 0)),
                  pl.BlockSpec((kq, tn), lambda i, j: (0, j)),
                  pl.BlockSpec((1, tn), lambda i, j: (0, j)),
                  pl.BlockSpec((tm, LANES), lambda i, j: (i, 0)),
                  pl.BlockSpec((tm, LANES), lambda i, j: (i, 0)),
                  pl.BlockSpec((tm, LANES), lambda i, j: (i, 0))],
        out_specs=pl.BlockSpec((tm, tn), lambda i, j: (i, j)),
        out_shape=jax.ShapeDtypeStruct((t, n), BF16),
        compiler_params=_cparams("arbitrary", "arbitrary"),
        name="mla_q_project",
    )(cfull, q_lora_g.reshape(1, kq), w_bf, gain, c_t, s1_t, s2_t)


def _mla_kv_kernel(ckv_ref, g_ref, kr_ref, w_ref, gn_ref, gt_ref, c_ref, s1_ref, s2_ref, k_ref, v_ref):
    h = _rms(ckv_ref[...], g_ref[...]).astype(BF16)
    y = jnp.dot(h, w_ref[...], preferred_element_type=F32)
    kr = kr_ref[...]
    ss_rope = jnp.sum(kr * kr, axis=-1, keepdims=True)
    rot = _rope64(kr * gt_ref[...], c_ref[...], s1_ref[...], s2_ref[...])
    per_head = MLA_NOPE + MLA_V
    for hh in range(y.shape[1] // per_head):
        kn = y[:, hh * per_head:hh * per_head + MLA_NOPE]
        ms = (jnp.sum(kn * kn, axis=-1, keepdims=True) + ss_rope) * (1.0 / MLA_QK)
        rs = lax.rsqrt(ms + NORM_EPS)
        lo = hh * MLA_HEAD_PAD
        k_ref[:, lo:lo + MLA_NOPE] = (kn * rs * gn_ref[...]).astype(k_ref.dtype)
        k_ref[:, lo + MLA_NOPE:lo + MLA_HEAD_PAD] = (rot * rs).astype(k_ref.dtype)
        v_ref[:, hh * MLA_V:(hh + 1) * MLA_V] = y[:, hh * per_head + MLA_NOPE:(hh + 1) * per_head].astype(v_ref.dtype)


def mla_kv_project(cfull, kv_lora_g, w_bf, gain_n, gain_t, c_t, s1_t, s2_t, *, tm, q_lora, heads_per_step):
    t = cfull.shape[0]
    kkv, n = w_bf.shape
    per_head = MLA_NOPE + MLA_V
    tn = heads_per_step * per_head
    heads = n // per_head
    return pl.pallas_call(
        _mla_kv_kernel,
        grid=(t // tm, n // tn),
        in_specs=[pl.BlockSpec((tm, kkv), lambda i, j: (i, q_lora // kkv)),
                  pl.BlockSpec((1, kkv), lambda i, j: (0, 0)),
                  pl.BlockSpec((tm, LANES), lambda i, j: (i, (q_lora + kkv) // LANES)),
                  pl.BlockSpec((kkv, tn), lambda i, j: (0, j)),
                  pl.BlockSpec((1, LANES), lambda i, j: (0, 0)),
                  pl.BlockSpec((1, LANES), lambda i, j: (0, 0)),
                  pl.BlockSpec((tm, LANES), lambda i, j: (i, 0)),
                  pl.BlockSpec((tm, LANES), lambda i, j: (i, 0)),
                  pl.BlockSpec((tm, LANES), lambda i, j: (i, 0))],
        out_specs=[pl.BlockSpec((tm, heads_per_step * MLA_HEAD_PAD), lambda i, j: (i, j)),
                   pl.BlockSpec((tm, heads_per_step * MLA_V), lambda i, j: (i, j))],
        out_shape=[jax.ShapeDtypeStruct((t, heads * MLA_HEAD_PAD), BF16),
                   jax.ShapeDtypeStruct((t, heads * MLA_V), BF16)],
        compiler_params=_cparams("arbitrary", "arbitrary"),
        name="mla_kv_project",
    )(cfull, kv_lora_g.reshape(1, kkv), cfull, w_bf, gain_n, gain_t, c_t, s1_t, s2_t)


def _proj_res_kernel(o_ref, w_ref, x_ref, gate_ref, out_ref):
    y = jnp.dot(o_ref[...], w_ref[...], preferred_element_type=F32)
    out_ref[...] = x_ref[...] + gate_ref[0] * y


def proj_residual(o, w_bf, xs, mods, layer, *, tm, tn, seq, batch):
    t, d = xs.shape
    k = o.shape[1]

    def seg(i):
        return jnp.minimum((i * tm) // seq, batch)

    return pl.pallas_call(
        _proj_res_kernel,
        grid=(t // tm, d // tn),
        in_specs=[pl.BlockSpec((tm, k), lambda i, j: (i, 0)),
                  pl.BlockSpec((k, tn), lambda i, j: (0, j)),
                  pl.BlockSpec((tm, tn), lambda i, j: (i, j)),
                  pl.BlockSpec((1, 1, tn), lambda i, j: ((layer * MOD_ROWS + seg(i)) * 6 + 2, 0, j))],
        out_specs=pl.BlockSpec((tm, tn), lambda i, j: (i, j)),
        out_shape=jax.ShapeDtypeStruct((t, d), F32),
        compiler_params=_cparams("arbitrary", "arbitrary"),
        name="proj_residual",
    )(o, w_bf, xs, mods)


_NT = (((1,), (1,)), ((), ()))


def _flash_scores(q, k, s_buf):
    s_buf[:, :k.shape[0]] = lax.dot_general(q, k, _NT, preferred_element_type=F32)


def _flash_update(s_scr, v, p_scr, m_scr, l_scr, acc_scr, rg):
    tq = s_scr.shape[0]
    w = v.shape[0]
    n_tiles = w // LANES

    for g in range(tq // rg):
        rows_ = slice(g * rg, (g + 1) * rg)
        tiles = [s_scr[rows_, t * LANES:(t + 1) * LANES] for t in range(n_tiles)]
        mx = tiles[0]
        for st in tiles[1:]:
            mx = jnp.maximum(mx, st)
        m_prev = m_scr[rows_, :]
        m_next = jnp.maximum(m_prev, jnp.max(mx, axis=-1, keepdims=True))
        alpha = jnp.exp2(m_prev - m_next)
        psum = None
        for t, st in enumerate(tiles):
            p = jnp.exp2(st - m_next)
            p_scr[rows_, t * LANES:(t + 1) * LANES] = p.astype(p_scr.dtype)
            psum = p if psum is None else psum + p
        l_scr[rows_, :] = alpha * l_scr[rows_, :] + psum
        acc_scr[rows_, :] = alpha * acc_scr[rows_, :]
        m_scr[rows_, :] = m_next
    acc_scr[...] += jnp.dot(p_scr[:, :w], v, preferred_element_type=F32)


def _flash_kernel(*refs, n_lat_chunks, tk, rg):
    if n_lat_chunks:
        q_ref, kc_ref, vc_ref, k_ref, v_ref, _, o_ref, s0, s1, p0, p1, m_scr, l_scr, acc_scr = refs
    else:
        q_ref, kc_ref, vc_ref, _, o_ref, s0, s1, p0, p1, m_scr, l_scr, acc_scr = refs
    state = (m_scr, l_scr, acc_scr, rg)
    m_scr[...] = jnp.full(m_scr.shape, -jnp.inf, F32)
    l_scr[...] = jnp.zeros(l_scr.shape, F32)
    acc_scr[...] = jnp.zeros(acc_scr.shape, F32)
    q = q_ref[...]
    _flash_scores(q, kc_ref[...], s0)
    if n_lat_chunks:
        assert n_lat_chunks % 2 == 0

        def k_chunk(c):
            return k_ref[pl.ds(pl.multiple_of(c * tk, tk), tk), :]

        def v_chunk(c):
            return v_ref[pl.ds(pl.multiple_of(c * tk, tk), tk), :]

        _flash_scores(q, k_chunk(0), s1)
        _flash_update(s0, vc_ref[...], p0, *state)

        def body(j, carry):
            _flash_scores(q, k_chunk(2 * j + 1), s0)
            _flash_update(s1, v_chunk(2 * j), p1, *state)
            _flash_scores(q, k_chunk(2 * j + 2), s1)
            _flash_update(s0, v_chunk(2 * j + 1), p0, *state)
            return carry
        lax.fori_loop(0, n_lat_chunks // 2 - 1, body, 0)
        _flash_scores(q, k_chunk(n_lat_chunks - 1), s0)
        _flash_update(s1, v_chunk(n_lat_chunks - 2), p1, *state)
        _flash_update(s0, v_chunk(n_lat_chunks - 1), p0, *state)
    else:
        _flash_update(s0, vc_ref[...], p0, *state)
    l = jnp.sum(l_scr[...], axis=-1, keepdims=True)
    o_ref[...] = (acc_scr[...] / l).astype(o_ref.dtype)


def _flash_scratch(tq, w):
    return [pltpu.VMEM((tq, w), F32), pltpu.VMEM((tq, w), F32),
            pltpu.VMEM((tq, w), BF16), pltpu.VMEM((tq, w), BF16),
            pltpu.VMEM((tq, LANES), F32), pltpu.VMEM((tq, LANES), F32), pltpu.VMEM((tq, LANES), F32)]


def ctx_attention(qa, ka, va, *, batch, seq, ctx, heads, dk, q_col0, k_col0, v_col0, kv_group):
    t = qa.shape[0]
    ctx_blk0 = (batch * seq) // ctx
    o_shape = jax.ShapeDtypeStruct((t, heads * LANES), BF16)
    return pl.pallas_call(
        functools.partial(_flash_kernel, n_lat_chunks=0, tk=0, rg=FLASH_ROW_GROUP),
        grid=(batch, heads),
        in_specs=[pl.BlockSpec((ctx, dk), lambda b, h: (ctx_blk0 + b, q_col0 + h)),
                  pl.BlockSpec((ctx, dk), lambda b, h: (ctx_blk0 + b, k_col0 + h // kv_group)),
                  pl.BlockSpec((ctx, LANES), lambda b, h: (ctx_blk0 + b, v_col0 + h // kv_group)),
                  pl.BlockSpec(memory_space=pl.ANY)],
        out_specs=pl.BlockSpec((ctx, LANES), lambda b, h: (ctx_blk0 + b, h)),
        out_shape=o_shape,
        input_output_aliases={3: 0},
        scratch_shapes=_flash_scratch(ctx, ctx),
        compiler_params=_cparams("arbitrary", "arbitrary"),
        name="ctx_attention",
    )(qa, ka, va, jnp.zeros(o_shape.shape, o_shape.dtype))


def dense_attention(qa, ka, va, o_ctx, *, batch, seq, ctx, heads, dk, q_col0, k_col0, v_col0, kv_group, tq, tk):
    nq = seq // tq
    ctx_blk0 = (batch * seq) // ctx
    return pl.pallas_call(
        functools.partial(_flash_kernel, n_lat_chunks=seq // tk, tk=tk, rg=FLASH_ROW_GROUP),
        grid=(batch, heads, nq),
        in_specs=[pl.BlockSpec((tq, dk), lambda b, h, i: (b * nq + i, q_col0 + h)),
                  pl.BlockSpec((ctx, dk), lambda b, h, i: (ctx_blk0 + b, k_col0 + h // kv_group)),
                  pl.BlockSpec((ctx, LANES), lambda b, h, i: (ctx_blk0 + b, v_col0 + h // kv_group)),
                  pl.BlockSpec((seq, dk), lambda b, h, i: (b, k_col0 + h // kv_group)),
                  pl.BlockSpec((seq, LANES), lambda b, h, i: (b, v_col0 + h // kv_group)),
                  pl.BlockSpec(memory_space=pl.ANY)],
        out_specs=pl.BlockSpec((tq, LANES), lambda b, h, i: (b * nq + i, h)),
        out_shape=jax.ShapeDtypeStruct(o_ctx.shape, o_ctx.dtype),
        input_output_aliases={5: 0},
        scratch_shapes=_flash_scratch(tq, max(tk, ctx)),
        compiler_params=_cparams("arbitrary", "arbitrary", "arbitrary"),
        name="dense_attention",
    )(qa, ka, va, ka, va, o_ctx)


def _na_kernel(q_ref, k_ref, v_ref, kc_ref, vc_ref, tb_ref, _, o_ref, *, rows, kr):
    kc = kc_ref[...]
    vc = vc_ref[...]
    nt = (((1,), (1,)), ((), ()))

    def body(r, carry):
        rs = jnp.clip(r - kr // 2, 0, rows - kr)
        q0 = pl.multiple_of(r * GRID_W, GRID_W)
        k0 = pl.multiple_of(rs * GRID_W, GRID_W)
        q = q_ref[pl.ds(q0, GRID_W), :]
        kw = k_ref[pl.ds(k0, kr * GRID_W), :]
        vw = v_ref[pl.ds(k0, kr * GRID_W), :]
        s_win = lax.dot_general(q, kw, nt, preferred_element_type=F32) + tb_ref[0, r - rs]
        s_ctx = lax.dot_general(q, kc, nt, preferred_element_type=F32)
        m = jnp.maximum(jnp.max(s_win, axis=-1, keepdims=True), jnp.max(s_ctx, axis=-1, keepdims=True))
        p_win = jnp.exp2(s_win - m)
        p_ctx = jnp.exp2(s_ctx - m)
        l = jnp.sum(p_win, axis=-1, keepdims=True) + jnp.sum(p_ctx, axis=-1, keepdims=True)
        o = (jnp.dot(p_win.astype(vw.dtype), vw, preferred_element_type=F32)
             + jnp.dot(p_ctx.astype(vc.dtype), vc, preferred_element_type=F32))
        o_ref[pl.ds(q0, GRID_W), :] = (o / l).astype(o_ref.dtype)
        return carry

    lax.fori_loop(0, rows, body, 0, unroll=4)


def na_attention(qkv, tb, o_ctx, *, batch, seq, ctx, heads):
    t = qkv.shape[0]
    rows = seq // GRID_W
    kr = min(NA_ROWS, rows)
    ctx_blk0 = (batch * seq) // ctx
    d = HEAD_DIM
    return pl.pallas_call(
        functools.partial(_na_kernel, rows=rows, kr=kr),
        grid=(batch, heads),
        in_specs=[pl.BlockSpec((seq, d), lambda b, h: (b, h)),
                  pl.BlockSpec((seq, d), lambda b, h: (b, heads + h)),
                  pl.BlockSpec((seq, d), lambda b, h: (b, 2 * heads + h)),
                  pl.BlockSpec((ctx, d), lambda b, h: (ctx_blk0 + b, heads + h)),
                  pl.BlockSpec((ctx, d), lambda b, h: (ctx_blk0 + b, 2 * heads + h)),
                  pl.BlockSpec((1, kr, GRID_W, kr * GRID_W), lambda b, h: (h, 0, 0, 0)),
                  pl.BlockSpec(memory_space=pl.ANY)],
        out_specs=pl.BlockSpec((seq, d), lambda b, h: (b, h)),
        out_shape=jax.ShapeDtypeStruct((t, heads * d), BF16),
        input_output_aliases={6: 0},
        compiler_params=_cparams("arbitrary", "arbitrary"),
        name="na_attention",
    )(qkv, qkv, qkv, qkv, qkv, tb, o_ctx)


def _router_kernel(x_ref, g_ref, sh_ref, sc_ref, w_ref, b_ref, h_ref, route_ref, counts_ref, carry_scr,
                   *, n_groups, epg):
    @pl.when(pl.program_id(0) == 0)
    def _():
        carry_scr[...] = jnp.zeros(carry_scr.shape, F32)

    h = _rms(x_ref[...], g_ref[...]) * (1.0 + sc_ref[0]) + sh_ref[0]
    h_ref[...] = h
    w = w_ref[...]
    h_hi = h.astype(BF16)
    h_lo = (h - h_hi.astype(F32)).astype(BF16)
    w_hi = w.astype(BF16)
    w_lo = (w - w_hi.astype(F32)).astype(BF16)
    logits = (jnp.dot(h_hi, w_hi, preferred_element_type=F32)
              + jnp.dot(h_lo, w_hi, preferred_element_type=F32)
              + jnp.dot(h_hi, w_lo, preferred_element_type=F32)) + b_ref[...]
    lane = lax.broadcasted_iota(jnp.int32, logits.shape, 1).astype(F32)
    is_g = lane < n_groups
    g_max = jnp.max(jnp.where(is_g, logits, -jnp.inf), axis=-1, keepdims=True)
    g_sum = jnp.sum(jnp.where(is_g, jnp.exp(logits - g_max), 0.0), axis=-1, keepdims=True)
    g_sel = jnp.min(jnp.where(is_g & (logits == g_max), lane, float(LANES)), axis=-1, keepdims=True)
    lo = n_groups + epg * g_sel
    in_grp = (lane >= lo) & (lane < lo + epg)
    t1 = jnp.max(jnp.where(in_grp, logits, -jnp.inf), axis=-1, keepdims=True)
    i1 = jnp.min(jnp.where(in_grp & (logits == t1), lane, float(LANES)), axis=-1, keepdims=True)
    rest = in_grp & (lane != i1)
    t2 = jnp.max(jnp.where(rest, logits, -jnp.inf), axis=-1, keepdims=True)
    i2 = jnp.min(jnp.where(rest & (logits == t2), lane, float(LANES)), axis=-1, keepdims=True)
    d = jnp.exp(t2 - t1)
    gate = 1.0 / g_sum
    w1 = gate * (1.0 / (1.0 + d))
    w2 = gate * (d / (1.0 + d))
    tm = logits.shape[0]
    chosen = jnp.where((lane == i1) | (lane == i2), 1.0, 0.0)
    earlier = jnp.where(lax.broadcasted_iota(jnp.int32, (tm, tm), 1) < lax.broadcasted_iota(jnp.int32, (tm, tm), 0),
                        1.0, 0.0).astype(BF16)
    before = jnp.dot(earlier, chosen.astype(BF16), preferred_element_type=F32) + carry_scr[...]
    rank1 = jnp.sum(jnp.where(lane == i1, before, 0.0), axis=-1, keepdims=True)
    rank2 = jnp.sum(jnp.where(lane == i2, before, 0.0), axis=-1, keepdims=True)
    total = carry_scr[...] + jnp.sum(chosen, axis=0, keepdims=True)
    carry_scr[...] = total
    counts_ref[...] = jnp.broadcast_to(total, counts_ref.shape)
    vals = (i1 - n_groups, i2 - n_groups, w1, w2, rank1, rank2)
    route = jnp.zeros(logits.shape, F32)
    for pos, v in enumerate(vals):
        route = jnp.where(lane == pos, v, route)
    route_ref[...] = route


def moe_router(xs, mods, layer, norm_g, w_r, b_r, *, tm, seq, batch, n_groups, epg):
    t, d = xs.shape
    sh_spec, sc_spec = _mod_specs(layer, (3, 4), tm, seq, batch, d)
    return pl.pallas_call(
        functools.partial(_router_kernel, n_groups=n_groups, epg=epg),
        grid=(t // tm, 1),
        in_specs=[pl.BlockSpec((tm, d), lambda i, j: (i, 0)),
                  pl.BlockSpec((1, d), lambda i, j: (0, 0)),
                  sh_spec, sc_spec,
                  pl.BlockSpec((d, LANES), lambda i, j: (0, 0)),
                  pl.BlockSpec((1, LANES), lambda i, j: (0, 0))],
        out_specs=[pl.BlockSpec((tm, d), lambda i, j: (i, 0)),
                   pl.BlockSpec((tm, LANES), lambda i, j: (i, 0)),
                   pl.BlockSpec((8, LANES), lambda i, j: (0, 0))],
        out_shape=[jax.ShapeDtypeStruct((t, d), F32), jax.ShapeDtypeStruct((t, LANES), F32),
                   jax.ShapeDtypeStruct((8, LANES), F32)],
        scratch_shapes=[pltpu.VMEM((1, LANES), F32)],
        compiler_params=_cparams("arbitrary", "arbitrary"),
        name="moe_router",
    )(xs, norm_g.reshape(1, d), mods, mods, w_r, b_r)


def _issue_row_gather(idx_ref, src_hbm, dst, sem, n_rows):
    def body(r, carry):
        tok = idx_ref[0, 0, r]
        pltpu.make_async_copy(src_hbm.at[pl.ds(tok, 1)], dst.at[pl.ds(r, 1)], sem).start()
        return carry
    lax.fori_loop(0, n_rows, body, 0, unroll=8)


def _wait_row_gather(src_hbm, dst, sem, n_rows):
    pltpu.make_async_copy(src_hbm.at[pl.ds(0, n_rows)], dst, sem).wait()


def _expert_kernel(be_ref, idx_ref, idx_next_ref, h_hbm, win_ref, wout_ref, y_ref,
                   xbuf, sem, win_bf, wout_bf, *, cast_rows):
    i = pl.program_id(0)
    nb = pl.num_programs(0)
    slot = i % 2
    blk = xbuf.shape[1]

    @pl.when(i == 0)
    def _():
        _issue_row_gather(idx_ref, h_hbm, xbuf.at[0], sem.at[0], blk)

    @pl.when(i + 1 < nb)
    def _():
        _issue_row_gather(idx_next_ref, h_hbm, xbuf.at[1 - slot], sem.at[1 - slot], blk)

    expert_changed = (i == 0) | (be_ref[i] != be_ref[jnp.maximum(i - 1, 0)])

    @pl.when(expert_changed)
    def _():
        def cast_in(c, carry):
            r0 = pl.multiple_of(c * cast_rows, cast_rows)
            win_bf[pl.ds(r0, cast_rows), :] = win_ref[0, 0, pl.ds(r0, cast_rows), :].astype(BF16)
            return carry
        lax.fori_loop(0, win_bf.shape[0] // cast_rows, cast_in, 0)

        def cast_out(c, carry):
            r0 = pl.multiple_of(c * cast_rows, cast_rows)
            wout_bf[pl.ds(r0, cast_rows), :] = wout_ref[0, 0, pl.ds(r0, cast_rows), :].astype(BF16)
            return carry
        lax.fori_loop(0, wout_bf.shape[0] // cast_rows, cast_out, 0)

    _wait_row_gather(h_hbm, xbuf.at[slot], sem.at[slot], blk)
    x = xbuf[slot].astype(BF16)
    gu = jnp.dot(x, win_bf[...], preferred_element_type=F32)
    f = gu.shape[1] // 2
    act = _silu(gu[:, :f]) * gu[:, f:]
    y_ref[...] = jnp.dot(act.astype(BF16), wout_bf[...], preferred_element_type=F32)


def moe_experts(h, slot_tok, block_expert, w_in, w_out, layer):
    t, d = h.shape
    f2 = w_in.shape[-1]
    f = f2 // 2
    nb = block_expert.shape[0]
    blk = MOE_BLOCK
    idx = slot_tok.reshape(nb, 1, blk)
    grid_spec = pltpu.PrefetchScalarGridSpec(
        num_scalar_prefetch=1,
        grid=(nb,),
        in_specs=[pl.BlockSpec((1, 1, blk), lambda i, be: (i, 0, 0), memory_space=pltpu.SMEM),
                  pl.BlockSpec((1, 1, blk), lambda i, be: (jnp.minimum(i + 1, nb - 1), 0, 0),
                               memory_space=pltpu.SMEM),
                  pl.BlockSpec(memory_space=pl.ANY),
                  pl.BlockSpec((1, 1, d, f2), lambda i, be: (layer, be[i], 0, 0)),
                  pl.BlockSpec((1, 1, f, d), lambda i, be: (layer, be[i], 0, 0))],
        out_specs=pl.BlockSpec((blk, d), lambda i, be: (i, 0)),
        scratch_shapes=[pltpu.VMEM((2, blk, d), F32),
                        pltpu.SemaphoreType.DMA((2,)),
                        pltpu.VMEM((d, f2), BF16),
                        pltpu.VMEM((f, d), BF16)],
    )
    return pl.pallas_call(
        functools.partial(_expert_kernel, cast_rows=256),
        grid_spec=grid_spec,
        out_shape=jax.ShapeDtypeStruct((nb * blk, d), F32),
        compiler_params=_cparams("arbitrary"),
        name="moe_experts",
    )(block_expert, idx, idx, h, w_in, w_out)


def _combine_kernel(pos_ref, pos_next_ref, yb_hbm, x_ref, route_ref, gate_ref, o_ref, ybuf, sem):
    i = pl.program_id(0)
    nt = pl.num_programs(0)
    slot = i % 2
    n_rows = ybuf.shape[1]
    tm = n_rows // MOE_TOP_K

    @pl.when(i == 0)
    def _():
        _issue_row_gather(pos_ref, yb_hbm, ybuf.at[0], sem.at[0], n_rows)

    @pl.when(i + 1 < nt)
    def _():
        _issue_row_gather(pos_next_ref, yb_hbm, ybuf.at[1 - slot], sem.at[1 - slot], n_rows)

    _wait_row_gather(yb_hbm, ybuf.at[slot], sem.at[slot], n_rows)
    route = route_ref[...]
    moe = route[:, 2:3] * ybuf[slot, :tm, :] + route[:, 3:4] * ybuf[slot, tm:, :]
    o_ref[...] = x_ref[...] + gate_ref[0] * moe


def moe_combine(yb, pos, xs, route, mods, layer, *, tm, seq, batch):
    t, d = xs.shape
    nt = t // tm
    pos_t = pos.reshape(nt, tm, MOE_TOP_K).transpose(0, 2, 1).reshape(nt, 1, MOE_TOP_K * tm)

    def seg(i):
        return jnp.minimum((i * tm) // seq, batch)

    return pl.pallas_call(
        _combine_kernel,
        grid=(nt,),
        in_specs=[pl.BlockSpec((1, 1, MOE_TOP_K * tm), lambda i: (i, 0, 0), memory_space=pltpu.SMEM),
                  pl.BlockSpec((1, 1, MOE_TOP_K * tm), lambda i: (jnp.minimum(i + 1, nt - 1), 0, 0),
                               memory_space=pltpu.SMEM),
                  pl.BlockSpec(memory_space=pl.ANY),
                  pl.BlockSpec((tm, d), lambda i: (i, 0)),
                  pl.BlockSpec((tm, LANES), lambda i: (i, 0)),
                  pl.BlockSpec((1, 1, d), lambda i: ((layer * MOD_ROWS + seg(i)) * 6 + 5, 0, 0))],
        out_specs=pl.BlockSpec((tm, d), lambda i: (i, 0)),
        out_shape=jax.ShapeDtypeStruct((t, d), F32),
        scratch_shapes=[pltpu.VMEM((2, MOE_TOP_K * tm, d), F32), pltpu.SemaphoreType.DMA((2,))],
        compiler_params=_cparams("arbitrary"),
        name="moe_combine",
    )(pos_t, pos_t, yb, xs, route, mods)


def _dispatch_plan(route, counts, n_experts, blk):
    t = route.shape[0]
    nk = t * MOE_TOP_K
    padded = (counts + blk - 1) // blk * blk
    pad_end = jnp.cumsum(padded)
    pad_start = pad_end - padded
    nb = -(-(nk + n_experts * (blk - 1)) // blk)
    e = route[:, :MOE_TOP_K].astype(jnp.int32)
    rank = route[:, 2 * MOE_TOP_K:3 * MOE_TOP_K].astype(jnp.int32)
    onehot = e[..., None] == jnp.arange(n_experts, dtype=jnp.int32)
    dest = jnp.sum(jnp.where(onehot, pad_start, 0), axis=-1) + rank
    tok = jnp.arange(nk, dtype=jnp.int32) // MOE_TOP_K
    slot_tok = jnp.zeros((nb * blk,), jnp.int32).at[dest.reshape(nk)].set(tok, unique_indices=True)
    first_row = jnp.arange(nb, dtype=jnp.int32) * blk
    block_expert = jnp.minimum(jnp.sum(pad_end[None, :] <= first_row[:, None], axis=1), n_experts - 1)
    return slot_tok, block_expert.astype(jnp.int32), dest


def _axial_angles(seq, dim):
    n_freq = dim // 4
    freqs = ROPE_THETA ** (-jnp.arange(n_freq, dtype=F32) / n_freq)
    tok = jnp.arange(seq, dtype=jnp.int32)
    row = (tok // GRID_W).astype(F32)
    col = (tok % GRID_W).astype(F32)
    return jnp.concatenate([row[:, None] * freqs, col[:, None] * freqs], axis=-1)


def _stream_table(lat, ctx_fill, batch, n_ctx_rows):
    ctx_rows = jnp.broadcast_to(ctx_fill[None, :], (n_ctx_rows, lat.shape[1]))
    return jnp.concatenate([jnp.tile(lat, (batch, 1)), ctx_rows], axis=0)


def _rope_tables_128(seq, batch, n_ctx_rows):
    ang = _axial_angles(seq, HEAD_DIM)
    cos, sin = jnp.cos(ang), jnp.sin(ang)
    c = jnp.concatenate([cos, cos], axis=-1)
    s = jnp.concatenate([-sin, sin], axis=-1)
    return (_stream_table(c, jnp.ones((HEAD_DIM,), F32), batch, n_ctx_rows),
            _stream_table(s, jnp.zeros((HEAD_DIM,), F32), batch, n_ctx_rows))


def _rope_tables_64(seq, batch, n_ctx_rows):
    ang = _axial_angles(seq, MLA_ROPE)
    cos, sin = jnp.cos(ang), jnp.sin(ang)
    half = MLA_ROPE // 2
    z = jnp.zeros((seq, LANES - MLA_ROPE), F32)
    zh = jnp.zeros((seq, half), F32)
    c = jnp.concatenate([cos, cos, z], axis=-1)
    s1 = jnp.concatenate([-sin, zh, z], axis=-1)
    s2 = jnp.concatenate([zh, sin, z], axis=-1)
    ones_pad = jnp.concatenate([jnp.ones((MLA_ROPE,), F32), jnp.zeros((LANES - MLA_ROPE,), F32)])
    zeros = jnp.zeros((LANES,), F32)
    return (_stream_table(c, ones_pad, batch, n_ctx_rows),
            _stream_table(s1, zeros, batch, n_ctx_rows),
            _stream_table(s2, zeros, batch, n_ctx_rows))


def _na_bias_table(rpb, kr):
    col = jnp.arange(GRID_W, dtype=jnp.int32)
    col_start = jnp.clip(col - NA_COLS // 2, 0, GRID_W - NA_COLS)
    col_in = (col[None, :] >= col_start[:, None]) & (col[None, :] < col_start[:, None] + NA_COLS)
    dc = jnp.clip(col[None, :] - col[:, None] + NA_COLS - 1, 0, 2 * NA_COLS - 2)
    h = rpb.shape[0]
    by_col = jnp.zeros((h, 2 * NA_ROWS - 1, GRID_W, GRID_W), F32)
    for cc in range(2 * NA_COLS - 1):
        by_col = jnp.where(dc[None, None] == cc, rpb[:, :, cc, None, None].astype(F32), by_col)
    by_col = jnp.where(col_in[None, None], by_col, NEG_INF)
    per_shift = [by_col[:, NA_ROWS - 1 - s:NA_ROWS - 1 - s + kr] for s in range(kr)]
    bias = jnp.stack(per_shift, axis=1)
    return bias.transpose(0, 1, 3, 2, 4).reshape(h, kr, GRID_W, kr * GRID_W)


def _pick_tile(*extents):
    for tm in (512, 256, 128):
        if all(e % tm == 0 for e in extents):
            return tm
    raise ValueError("row extents must be multiples of 128")


def kernel(x, c, ctx, c_ctx, mod_w, mod_b, norm_g, gqa_w_qkv, gqa_q_norm, gqa_k_norm, gqa_w_o, na_w_qkv, na_q_norm, na_k_norm, na_rpb, na_w_o, mla_w_down, mla_q_lora_norm, mla_w_uq, mla_kv_lora_norm, mla_w_ukv, mla_q_norm, mla_k_norm, mla_w_o, moe_w_group, moe_b_group, moe_w_expert, moe_b_expert, moe_w_in, moe_w_out):
    batch, seq, d = x.shape
    n_ctx = ctx.shape[1]
    depth = mod_w.shape[0]
    heads = d // HEAD_DIM
    kv_heads = heads // 4
    n_lat = batch * seq
    n_ctx_rows = batch * n_ctx
    assert batch + 1 <= MOD_ROWS and n_lat % n_ctx == 0 and seq % GRID_W == 0
    tm = _pick_tile(seq, n_ctx_rows)
    tq = min(FLASH_TQ, seq)
    tk = min(FLASH_TK, seq)
    assert seq % tq == 0 and seq % (2 * tk) == 0
    tn_d = min(512, d)
    dims = dict(tm=tm, seq=seq, batch=batch)

    xs = jnp.concatenate([x.reshape(n_lat, d), ctx.reshape(n_ctx_rows, d)], axis=0)

    cond = jnp.zeros((MOD_ROWS, d), F32).at[:batch].set(c).at[batch].set(c_ctx)
    mods = adaln_all(cond, mod_w, mod_b).reshape(depth * MOD_ROWS * 6, 1, d)

    cos_a, sin_a = _rope_tables_128(seq, batch, n_ctx_rows)
    c64, s64a, s64b = _rope_tables_64(seq, batch, n_ctx_rows)

    n_groups = moe_w_group.shape[-1]
    epg = moe_w_expert.shape[-1]
    n_experts = n_groups * epg
    assert n_groups + n_experts <= LANES

    for i in range(depth):
        kind, j = i % 3, i // 3
        if kind == 0:
            scale = HEAD_DIM ** -0.5 * LOG2E
            nq, nkv = heads * HEAD_DIM, kv_heads * HEAD_DIM
            gain = jnp.concatenate([jnp.tile(gqa_q_norm[j] * scale, heads), jnp.tile(gqa_k_norm[j], kv_heads),
                                    jnp.ones((nkv,), F32)]).reshape(1, -1)
            qkv = qkv_project(xs, mods, i, norm_g[i, 0], gqa_w_qkv[j].astype(BF16), gain, cos_a, sin_a,
                              n_norm_cols=nq + nkv, tn=min(512, nkv), rope=True, **dims)
            cols = dict(dk=HEAD_DIM, q_col0=0, k_col0=heads, v_col0=heads + kv_heads, kv_group=heads // kv_heads)
            o_ctx = ctx_attention(qkv, qkv, qkv, batch=batch, seq=seq, ctx=n_ctx, heads=heads, **cols)
            o = dense_attention(qkv, qkv, qkv, o_ctx, batch=batch, seq=seq, ctx=n_ctx, heads=heads, tq=tq, tk=tk,
                                **cols)
            w_o = gqa_w_o[j]
        elif kind == 1:
            scale = HEAD_DIM ** -0.5 * LOG2E
            nq = heads * HEAD_DIM
            gain = jnp.concatenate([jnp.tile(na_q_norm[j] * scale, heads), jnp.tile(na_k_norm[j], heads),
                                    jnp.ones((nq,), F32)]).reshape(1, -1)
            qkv = qkv_project(xs, mods, i, norm_g[i, 0], na_w_qkv[j].astype(BF16), gain, cos_a, sin_a,
                              n_norm_cols=2 * nq, tn=tn_d, rope=False, **dims)
            rows = seq // GRID_W
            tb = _na_bias_table(na_rpb[j], min(NA_ROWS, rows)) * LOG2E
            o_ctx = ctx_attention(qkv, qkv, qkv, batch=batch, seq=seq, ctx=n_ctx, heads=heads, dk=HEAD_DIM,
                                  q_col0=0, k_col0=heads, v_col0=2 * heads, kv_group=1)
            o = na_attention(qkv, tb, o_ctx, batch=batch, seq=seq, ctx=n_ctx, heads=heads)
            w_o = na_w_o[j]
        else:
            scale = MLA_QK ** -0.5 * LOG2E
            q_lora = mla_q_lora_norm.shape[-1]
            kv_lora = mla_kv_lora_norm.shape[-1]
            assert q_lora % kv_lora == 0 and (q_lora + kv_lora) % LANES == 0
            n_down = q_lora + kv_lora + LANES
            w_down = jnp.pad(mla_w_down[j], ((0, 0), (0, n_down - mla_w_down.shape[-1]))).astype(BF16)
            cfull = mla_down(xs, mods, i, norm_g[i, 0], w_down, **dims)
            pad = MLA_HEAD_PAD - MLA_QK
            w_uq = jnp.pad(mla_w_uq[j].reshape(q_lora, heads, MLA_QK), ((0, 0), (0, 0), (0, pad)))
            w_uq = w_uq.reshape(q_lora, heads * MLA_HEAD_PAD).astype(BF16)
            q_gain = jnp.tile(jnp.pad(mla_q_norm[j] * scale, (0, pad)), heads).reshape(1, -1)
            qa = mla_q_project(cfull, mla_q_lora_norm[j], w_uq, q_gain, c64, s64a, s64b, tm=tm,
                               tn=min(512, heads * MLA_HEAD_PAD))
            gain_n = mla_k_norm[j, :MLA_NOPE].reshape(1, LANES)
            gain_t = jnp.pad(mla_k_norm[j, MLA_NOPE:], (0, LANES - MLA_ROPE)).reshape(1, LANES)
            ka, va = mla_kv_project(cfull, mla_kv_lora_norm[j], mla_w_ukv[j].astype(BF16), gain_n, gain_t,
                                    c64, s64a, s64b, tm=tm, q_lora=q_lora, heads_per_step=2)
            cols = dict(dk=MLA_HEAD_PAD, q_col0=0, k_col0=0, v_col0=0, kv_group=1)
            o_ctx = ctx_attention(qa, ka, va, batch=batch, seq=seq, ctx=n_ctx, heads=heads, **cols)
            o = dense_attention(qa, ka, va, o_ctx, batch=batch, seq=seq, ctx=n_ctx, heads=heads, tq=tq, tk=tk, **cols)
            w_o = mla_w_o[j]

        xs = proj_residual(o, w_o.astype(BF16), xs, mods, i, tn=tn_d, **dims)

        w_r = jnp.concatenate([moe_w_group[i], moe_w_expert[i].transpose(1, 0, 2).reshape(d, n_experts)], axis=1)
        w_r = jnp.pad(w_r, ((0, 0), (0, LANES - w_r.shape[1])))
        b_r = jnp.pad(jnp.concatenate([moe_b_group[i], moe_b_expert[i].reshape(-1)]),
                      (0, LANES - n_groups - n_experts)).reshape(1, LANES)
        h2, route, lane_counts = moe_router(xs, mods, i, norm_g[i, 1], w_r, b_r, n_groups=n_groups, epg=epg, **dims)
        counts = lane_counts[0, n_groups:n_groups + n_experts].astype(jnp.int32)
        slot_tok, block_expert, pos = _dispatch_plan(route, counts, n_experts, MOE_BLOCK)
        yb = moe_experts(h2, slot_tok, block_expert, moe_w_in, moe_w_out, i)
        xs = moe_combine(yb, pos, xs, route, mods, i, tm=min(COMBINE_TM, tm), seq=seq, batch=batch)

    return xs[:n_lat].reshape(batch, seq, d)
```

```python
import functools

import jax
import jax.numpy as jnp
from jax import lax
from jax.experimental import pallas as pl
from jax.experimental.pallas import tpu as pltpu

F32 = jnp.float32
BF16 = jnp.bfloat16

GRID_W = 64
HEAD_DIM = 128
ROPE_THETA = 10000.0
NORM_EPS = 1e-6
NEG_INF = -1e30
NA_ROWS = 8
NA_COLS = 16
MLA_NOPE = 128
MLA_ROPE = 64
MLA_V = 128
MLA_QK = MLA_NOPE + MLA_ROPE
MLA_HEAD_PAD = 256
MOE_TOP_K = 2
LANES = 128
MOD_ROWS = 8
MOE_BLOCK = 256
COMBINE_TM = 256
FLASH_TQ = 512
FLASH_TK = 512
FLASH_ROW_GROUP = 32
LOG2E = 1.4426950408889634
VMEM_LIMIT = 52 * 1024 * 1024


def _cparams(*sem):
    return pltpu.CompilerParams(dimension_semantics=sem, vmem_limit_bytes=VMEM_LIMIT)


def _silu(v):
    return v / (1.0 + jnp.exp(-v))


def _adaln_kernel(c_ref, w_ref, b_ref, o_ref):
    cond = _silu(c_ref[...]).astype(BF16)
    o_ref[0] = jnp.dot(cond, w_ref[0].astype(BF16), preferred_element_type=F32) + b_ref[0]


def adaln_all(cond, mod_w, mod_b):
    depth, d, n = mod_w.shape
    tn = 1024
    return pl.pallas_call(
        _adaln_kernel,
        grid=(depth, n // tn),
        in_specs=[pl.BlockSpec((MOD_ROWS, d), lambda l, j: (0, 0)),
                  pl.BlockSpec((1, d, tn), lambda l, j: (l, 0, j)),
                  pl.BlockSpec((1, 1, tn), lambda l, j: (l, 0, j))],
        out_specs=pl.BlockSpec((1, MOD_ROWS, tn), lambda l, j: (l, 0, j)),
        out_shape=jax.ShapeDtypeStruct((depth, MOD_ROWS, n), F32),
        compiler_params=_cparams("arbitrary", "arbitrary"),
        name="adaln",
    )(cond, mod_w, mod_b.reshape(depth, 1, n))


def _rms(x, g):
    ms = jnp.mean(x * x, axis=-1, keepdims=True)
    return x * lax.rsqrt(ms + NORM_EPS) * g


def _rope128(y, c, s):
    return y * c + pltpu.roll(y, 64, 1) * s


def _rope64(y, c, s1, s2):
    return y * c + pltpu.roll(y, 96, 1) * s1 + pltpu.roll(y, 32, 1) * s2


def _qkv_kernel(x_ref, g_ref, sh_ref, sc_ref, w_ref, gain_ref, c_ref, s_ref, o_ref, h_scr, *, n_norm_tiles, rope):
    j = pl.program_id(1)

    @pl.when(j == 0)
    def _():
        h = _rms(x_ref[...], g_ref[...]) * (1.0 + sc_ref[0]) + sh_ref[0]
        h_scr[...] = h.astype(BF16)

    y = jnp.dot(h_scr[...], w_ref[...], preferred_element_type=F32)
    n_heads = y.shape[1] // HEAD_DIM

    @pl.when(j < n_norm_tiles)
    def _():
        for hh in range(n_heads):
            sl = slice(hh * HEAD_DIM, (hh + 1) * HEAD_DIM)
            yh = _rms(y[:, sl], gain_ref[:, sl])
            if rope:
                yh = _rope128(yh, c_ref[...], s_ref[...])
            o_ref[:, sl] = yh.astype(o_ref.dtype)

    @pl.when(j >= n_norm_tiles)
    def _():
        o_ref[...] = y.astype(o_ref.dtype)


def _mod_specs(layer, which, tm, seq, batch, d):
    def seg(i):
        return jnp.minimum((i * tm) // seq, batch)
    return [pl.BlockSpec((1, 1, d), lambda i, j, w=w: ((layer * MOD_ROWS + seg(i)) * 6 + w, 0, 0)) for w in which]


def qkv_project(xs, mods, layer, norm_g, w_bf, gain, cos_t, sin_t, *, tm, seq, batch, n_norm_cols, tn, rope):
    t, d = xs.shape
    n = w_bf.shape[1]
    sh_spec, sc_spec = _mod_specs(layer, (0, 1), tm, seq, batch, d)
    kern = functools.partial(_qkv_kernel, n_norm_tiles=n_norm_cols // tn, rope=rope)
    return pl.pallas_call(
        kern,
        grid=(t // tm, n // tn),
        in_specs=[pl.BlockSpec((tm, d), lambda i, j: (i, 0)),
                  pl.BlockSpec((1, d), lambda i, j: (0, 0)),
                  sh_spec, sc_spec,
                  pl.BlockSpec((d, tn), lambda i, j: (0, j)),
                  pl.BlockSpec((1, tn), lambda i, j: (0, j)),
                  pl.BlockSpec((tm, HEAD_DIM), lambda i, j: (i, 0)),
                  pl.BlockSpec((tm, HEAD_DIM), lambda i, j: (i, 0))],
        out_specs=pl.BlockSpec((tm, tn), lambda i, j: (i, j)),
        out_shape=jax.ShapeDtypeStruct((t, n), BF16),
        scratch_shapes=[pltpu.VMEM((tm, d), BF16)],
        compiler_params=_cparams("arbitrary", "arbitrary"),
        name="qkv_project",
    )(xs, norm_g.reshape(1, d), mods, mods, w_bf, gain, cos_t, sin_t)


def _down_kernel(x_ref, g_ref, sh_ref, sc_ref, w_ref, o_ref):
    h = _rms(x_ref[...], g_ref[...]) * (1.0 + sc_ref[0]) + sh_ref[0]
    o_ref[...] = jnp.dot(h.astype(BF16), w_ref[...], preferred_element_type=F32)


def mla_down(xs, mods, layer, norm_g, w_bf, *, tm, seq, batch):
    t, d = xs.shape
    n = w_bf.shape[1]
    sh_spec, sc_spec = _mod_specs(layer, (0, 1), tm, seq, batch, d)
    return pl.pallas_call(
        _down_kernel,
        grid=(t // tm, 1),
        in_specs=[pl.BlockSpec((tm, d), lambda i, j: (i, 0)),
                  pl.BlockSpec((1, d), lambda i, j: (0, 0)),
                  sh_spec, sc_spec,
                  pl.BlockSpec((d, n), lambda i, j: (0, 0))],
        out_specs=pl.BlockSpec((tm, n), lambda i, j: (i, 0)),
        out_shape=jax.ShapeDtypeStruct((t, n), F32),
        compiler_params=_cparams("arbitrary", "arbitrary"),
        name="mla_down",
    )(xs, norm_g.reshape(1, d), mods, mods, w_bf)


def _mla_q_kernel(cq_ref, g_ref, w_ref, gain_ref, c_ref, s1_ref, s2_ref, o_ref):
    h = _rms(cq_ref[...], g_ref[...]).astype(BF16)
    y = jnp.dot(h, w_ref[...], preferred_element_type=F32)
    for hh in range(y.shape[1] // MLA_HEAD_PAD):
        lo = hh * MLA_HEAD_PAD
        yh = y[:, lo:lo + MLA_HEAD_PAD]
        ms = jnp.sum(yh * yh, axis=-1, keepdims=True) * (1.0 / MLA_QK)
        yh = yh * lax.rsqrt(ms + NORM_EPS) * gain_ref[:, lo:lo + MLA_HEAD_PAD]
        o_ref[:, lo:lo + MLA_NOPE] = yh[:, :MLA_NOPE].astype(o_ref.dtype)
        tail = _rope64(yh[:, MLA_NOPE:], c_ref[...], s1_ref[...], s2_ref[...])
        o_ref[:, lo + MLA_NOPE:lo + MLA_HEAD_PAD] = tail.astype(o_ref.dtype)


def mla_q_project(cfull, q_lora_g, w_bf, gain, c_t, s1_t, s2_t, *, tm, tn):
    t = cfull.shape[0]
    kq, n = w_bf.shape
    return pl.pallas_call(
        _mla_q_kernel,
        grid=(t // tm, n // tn),
        in_specs=[pl.BlockSpec((tm, kq), lambda i, j: (i, 0)),
                  pl.BlockSpec((1, kq), lambda i, j: (0, 0)),
                  pl.BlockSpec((kq, tn), lambda i, j: (0, j)),
                  pl.BlockSpec((1, tn), lambda i, j: (0, j)),
                  pl.BlockSpec((tm, LANES), lambda i, j: (i, 0)),
                  pl.BlockSpec((tm, LANES), lambda i, j: (i, 0)),
                  pl.BlockSpec((tm, LANES), lambda i, j: (i, 0))],
        out_specs=pl.BlockSpec((tm, tn), lambda i, j: (i, j)),
        out_shape=jax.ShapeDtypeStruct((t, n), BF16),
        compiler_params=_cparams("arbitrary", "arbitrary"),
        name="mla_q_project",
    )(cfull, q_lora_g.reshape(1, kq), w_bf, gain, c_t, s1_t, s2_t)


def _mla_kv_kernel(ckv_ref, g_ref, kr_ref, w_ref, gn_ref, gt_ref, c_ref, s1_ref, s2_ref, k_ref, v_ref):
    h = _rms(ckv_ref[...], g_ref[...]).astype(BF16)
    y = jnp.dot(h, w_ref[...], preferred_element_type=F32)
    kr = kr_ref[...]
    ss_rope = jnp.sum(kr * kr, axis=-1, keepdims=True)
    rot = _rope64(kr * gt_ref[...], c_ref[...], s1_ref[...], s2_ref[...])
    per_head = MLA_NOPE + MLA_V
    for hh in range(y.shape[1] // per_head):
        kn = y[:, hh * per_head:hh * per_head + MLA_NOPE]
        ms = (jnp.sum(kn * kn, axis=-1, keepdims=True) + ss_rope) * (1.0 / MLA_QK)
        rs = lax.rsqrt(ms + NORM_EPS)
        lo = hh * MLA_HEAD_PAD
        k_ref[:, lo:lo + MLA_NOPE] = (kn * rs * gn_ref[...]).astype(k_ref.dtype)
        k_ref[:, lo + MLA_NOPE:lo + MLA_HEAD_PAD] = (rot * rs).astype(k_ref.dtype)
        v_ref[:, hh * MLA_V:(hh + 1) * MLA_V] = y[:, hh * per_head + MLA_NOPE:(hh + 1) * per_head].astype(v_ref.dtype)


def mla_kv_project(cfull, kv_lora_g, w_bf, gain_n, gain_t, c_t, s1_t, s2_t, *, tm, q_lora, heads_per_step):
    t = cfull.shape[0]
    kkv, n = w_bf.shape
    per_head = MLA_NOPE + MLA_V
    tn = heads_per_step * per_head
    heads = n // per_head
    return pl.pallas_call(
        _mla_kv_kernel,
        grid=(t // tm, n // tn),
        in_specs=[pl.BlockSpec((tm, kkv), lambda i, j: (i, q_lora // kkv)),
                  pl.BlockSpec((1, kkv), lambda i, j: (0, 0)),
                  pl.BlockSpec((tm, LANES), lambda i, j: (i, (q_lora + kkv) // LANES)),
                  pl.BlockSpec((kkv, tn), lambda i, j: (0, j)),
                  pl.BlockSpec((1, LANES), lambda i, j: (0, 0)),
                  pl.BlockSpec((1, LANES), lambda i, j: (0, 0)),
                  pl.BlockSpec((tm, LANES), lambda i, j: (i, 0)),
                  pl.BlockSpec((tm, LANES), lambda i, j: (i, 0)),
                  pl.BlockSpec((tm, LANES), lambda i, j: (i, 0))],
        out_specs=[pl.BlockSpec((tm, heads_per_step * MLA_HEAD_PAD), lambda i, j: (i, j)),
                   pl.BlockSpec((tm, heads_per_step * MLA_V), lambda i, j: (i, j))],
        out_shape=[jax.ShapeDtypeStruct((t, heads * MLA_HEAD_PAD), BF16),
                   jax.ShapeDtypeStruct((t, heads * MLA_V), BF16)],
        compiler_params=_cparams("arbitrary", "arbitrary"),
        name="mla_kv_project",
    )(cfull, kv_lora_g.reshape(1, kkv), cfull, w_bf, gain_n, gain_t, c_t, s1_t, s2_t)


def _proj_res_kernel(o_ref, w_ref, x_ref, gate_ref, out_ref):
    y = jnp.dot(o_ref[...], w_ref[...], preferred_element_type=F32)
    out_ref[...] = x_ref[...] + gate_ref[0] * y


def proj_residual(o, w_bf, xs, mods, layer, *, tm, tn, seq, batch):
    t, d = xs.shape
    k = o.shape[1]

    def seg(i):
        return jnp.minimum((i * tm) // seq, batch)

    return pl.pallas_call(
        _proj_res_kernel,
        grid=(t // tm, d // tn),
        in_specs=[pl.BlockSpec((tm, k), lambda i, j: (i, 0)),
                  pl.BlockSpec((k, tn), lambda i, j: (0, j)),
                  pl.BlockSpec((tm, tn), lambda i, j: (i, j)),
                  pl.BlockSpec((1, 1, tn), lambda i, j: ((layer * MOD_ROWS + seg(i)) * 6 + 2, 0, j))],
        out_specs=pl.BlockSpec((tm, tn), lambda i, j: (i, j)),
        out_shape=jax.ShapeDtypeStruct((t, d), F32),
        compiler_params=_cparams("arbitrary", "arbitrary"),
        name="proj_residual",
    )(o, w_bf, xs, mods)


_NT = (((1,), (1,)), ((), ()))


def _flash_scores(q, k, s_buf):
    s_buf[:, :k.shape[0]] = lax.dot_general(q, k, _NT, preferred_element_type=F32)


def _flash_update(s_scr, v, p_scr, m_scr, l_scr, acc_scr, rg):
    tq = s_scr.shape[0]
    w = v.shape[0]
    n_tiles = w // LANES

    for g in range(tq // rg):
        rows_ = slice(g * rg, (g + 1) * rg)
        tiles = [s_scr[rows_, t * LANES:(t + 1) * LANES] for t in range(n_tiles)]
        mx = tiles[0]
        for st in tiles[1:]:
            mx = jnp.maximum(mx, st)
        m_prev = m_scr[rows_, :]
        m_next = jnp.maximum(m_prev, jnp.max(mx, axis=-1, keepdims=True))
        alpha = jnp.exp2(m_prev - m_next)
        psum = None
        for t, st in enumerate(tiles):
            p = jnp.exp2(st - m_next)
            p_scr[rows_, t * LANES:(t + 1) * LANES] = p.astype(p_scr.dtype)
            psum = p if psum is None else psum + p
        l_scr[rows_, :] = alpha * l_scr[rows_, :] + psum
        acc_scr[rows_, :] = alpha * acc_scr[rows_, :]
        m_scr[rows_, :] = m_next
    acc_scr[...] += jnp.dot(p_scr[:, :w], v, preferred_element_type=F32)


def _flash_kernel(*refs, n_lat_chunks, tk, rg):
    if n_lat_chunks:
        q_ref, kc_ref, vc_ref, k_ref, v_ref, _, o_ref, s0, s1, p0, p1, m_scr, l_scr, acc_scr = refs
    else:
        q_ref, kc_ref, vc_ref, _, o_ref, s0, s1, p0, p1, m_scr, l_scr, acc_scr = refs
    state = (m_scr, l_scr, acc_scr, rg)
    m_scr[...] = jnp.full(m_scr.shape, -jnp.inf, F32)
    l_scr[...] = jnp.zeros(l_scr.shape, F32)
    acc_scr[...] = jnp.zeros(acc_scr.shape, F32)
    q = q_ref[...]
    _flash_scores(q, kc_ref[...], s0)
    if n_lat_chunks:
        assert n_lat_chunks % 2 == 0

        def k_chunk(c):
            return k_ref[pl.ds(pl.multiple_of(c * tk, tk), tk), :]

        def v_chunk(c):
            return v_ref[pl.ds(pl.multiple_of(c * tk, tk), tk), :]

        _flash_scores(q, k_chunk(0), s1)
        _flash_update(s0, vc_ref[...], p0, *state)

        def body(j, carry):
            _flash_scores(q, k_chunk(2 * j + 1), s0)
            _flash_update(s1, v_chunk(2 * j), p1, *state)
            _flash_scores(q, k_chunk(2 * j + 2), s1)
            _flash_update(s0, v_chunk(2 * j + 1), p0, *state)
            return carry
        lax.fori_loop(0, n_lat_chunks // 2 - 1, body, 0)
        _flash_scores(q, k_chunk(n_lat_chunks - 1), s0)
        _flash_update(s1, v_chunk(n_lat_chunks - 2), p1, *state)
        _flash_update(s0, v_chunk(n_lat_chunks - 1), p0, *state)
    else:
        _flash_update(s0, vc_ref[...], p0, *state)
    l = jnp.sum(l_scr[...], axis=-1, keepdims=True)
    o_ref[...] = (acc_scr[...] / l).astype(o_ref.dtype)


def _flash_scratch(tq, w):
    return [pltpu.VMEM((tq, w), F32), pltpu.VMEM((tq, w), F32),
            pltpu.VMEM((tq, w), BF16), pltpu.VMEM((tq, w), BF16),
            pltpu.VMEM((tq, LANES), F32), pltpu.VMEM((tq, LANES), F32), pltpu.VMEM((tq, LANES), F32)]


def ctx_attention(qa, ka, va, *, batch, seq, ctx, heads, dk, q_col0, k_col0, v_col0, kv_group):
    t = qa.shape[0]
    ctx_blk0 = (batch * seq) // ctx
    o_shape = jax.ShapeDtypeStruct((t, heads * LANES), BF16)
    return pl.pallas_call(
        functools.partial(_flash_kernel, n_lat_chunks=0, tk=0, rg=FLASH_ROW_GROUP),
        grid=(batch, heads),
        in_specs=[pl.BlockSpec((ctx, dk), lambda b, h: (ctx_blk0 + b, q_col0 + h)),
                  pl.BlockSpec((ctx, dk), lambda b, h: (ctx_blk0 + b, k_col0 + h // kv_group)),
                  pl.BlockSpec((ctx, LANES), lambda b, h: (ctx_blk0 + b, v_col0 + h // kv_group)),
                  pl.BlockSpec(memory_space=pl.ANY)],
        out_specs=pl.BlockSpec((ctx, LANES), lambda b, h: (ctx_blk0 + b, h)),
        out_shape=o_shape,
        input_output_aliases={3: 0},
        scratch_shapes=_flash_scratch(ctx, ctx),
        compiler_params=_cparams("arbitrary", "arbitrary"),
        name="ctx_attention",
    )(qa, ka, va, jnp.zeros(o_shape.shape, o_shape.dtype))


def dense_attention(qa, ka, va, o_ctx, *, batch, seq, ctx, heads, dk, q_col0, k_col0, v_col0, kv_group, tq, tk):
    nq = seq // tq
    ctx_blk0 = (batch * seq) // ctx
    return pl.pallas_call(
        functools.partial(_flash_kernel, n_lat_chunks=seq // tk, tk=tk, rg=FLASH_ROW_GROUP),
        grid=(batch, heads, nq),
        in_specs=[pl.BlockSpec((tq, dk), lambda b, h, i: (b * nq + i, q_col0 + h)),
                  pl.BlockSpec((ctx, dk), lambda b, h, i: (ctx_blk0 + b, k_col0 + h // kv_group)),
                  pl.BlockSpec((ctx, LANES), lambda b, h, i: (ctx_blk0 + b, v_col0 + h // kv_group)),
                  pl.BlockSpec((seq, dk), lambda b, h, i: (b, k_col0 + h // kv_group)),
                  pl.BlockSpec((seq, LANES), lambda b, h, i: (b, v_col0 + h // kv_group)),
                  pl.BlockSpec(memory_space=pl.ANY)],
        out_specs=pl.BlockSpec((tq, LANES), lambda b, h, i: (b * nq + i, h)),
        out_shape=jax.ShapeDtypeStruct(o_ctx.shape, o_ctx.dtype),
        input_output_aliases={5: 0},
        scratch_shapes=_flash_scratch(tq, max(tk, ctx)),
        compiler_params=_cparams("arbitrary", "arbitrary", "arbitrary"),
        name="dense_attention",
    )(qa, ka, va, ka, va, o_ctx)


def _na_kernel(q_ref, k_ref, v_ref, kc_ref, vc_ref, tb_ref, _, o_ref, *, rows, kr):
    kc = kc_ref[...]
    vc = vc_ref[...]
    nt = (((1,), (1,)), ((), ()))

    def body(r, carry):
        rs = jnp.clip(r - kr // 2, 0, rows - kr)
        q0 = pl.multiple_of(r * GRID_W, GRID_W)
        k0 = pl.multiple_of(rs * GRID_W, GRID_W)
        q = q_ref[pl.ds(q0, GRID_W), :]
        kw = k_ref[pl.ds(k0, kr * GRID_W), :]
        vw = v_ref[pl.ds(k0, kr * GRID_W), :]
        s_win = lax.dot_general(q, kw, nt, preferred_element_type=F32) + tb_ref[0, r - rs]
        s_ctx = lax.dot_general(q, kc, nt, preferred_element_type=F32)
        m = jnp.maximum(jnp.max(s_win, axis=-1, keepdims=True), jnp.max(s_ctx, axis=-1, keepdims=True))
        p_win = jnp.exp2(s_win - m)
        p_ctx = jnp.exp2(s_ctx - m)
        l = jnp.sum(p_win, axis=-1, keepdims=True) + jnp.sum(p_ctx, axis=-1, keepdims=True)
        o = (jnp.dot(p_win.astype(vw.dtype), vw, preferred_element_type=F32)
             + jnp.dot(p_ctx.astype(vc.dtype), vc, preferred_element_type=F32))
        o_ref[pl.ds(q0, GRID_W), :] = (o / l).astype(o_ref.dtype)
        return carry

    lax.fori_loop(0, rows, body, 0, unroll=4)


def na_attention(qkv, tb, o_ctx, *, batch, seq, ctx, heads):
    t = qkv.shape[0]
    rows = seq // GRID_W
    kr = min(NA_ROWS, rows)
    ctx_blk0 = (batch * seq) // ctx
    d = HEAD_DIM
    return pl.pallas_call(
        functools.partial(_na_kernel, rows=rows, kr=kr),
        grid=(batch, heads),
        in_specs=[pl.BlockSpec((seq, d), lambda b, h: (b, h)),
                  pl.BlockSpec((seq, d), lambda b, h: (b, heads + h)),
                  pl.BlockSpec((seq, d), lambda b, h: (b, 2 * heads + h)),
                  pl.BlockSpec((ctx, d), lambda b, h: (ctx_blk0 + b, heads + h)),
                  pl.BlockSpec((ctx, d), lambda b, h: (ctx_blk0 + b, 2 * heads + h)),
                  pl.BlockSpec((1, kr, GRID_W, kr * GRID_W), lambda b, h: (h, 0, 0, 0)),
                  pl.BlockSpec(memory_space=pl.ANY)],
        out_specs=pl.BlockSpec((seq, d), lambda b, h: (b, h)),
        out_shape=jax.ShapeDtypeStruct((t, heads * d), BF16),
        input_output_aliases={6: 0},
        compiler_params=_cparams("arbitrary", "arbitrary"),
        name="na_attention",
    )(qkv, qkv, qkv, qkv, qkv, tb, o_ctx)


def _router_kernel(x_ref, g_ref, sh_ref, sc_ref, w_ref, b_ref, h_ref, route_ref, counts_ref, carry_scr,
                   *, n_groups, epg):
    @pl.when(pl.program_id(0) == 0)
    def _():
        carry_scr[...] = jnp.zeros(carry_scr.shape, F32)

    h = _rms(x_ref[...], g_ref[...]) * (1.0 + sc_ref[0]) + sh_ref[0]
    h_ref[...] = h
    w = w_ref[...]
    h_hi = h.astype(BF16)
    h_lo = (h - h_hi.astype(F32)).astype(BF16)
    w_hi = w.astype(BF16)
    w_lo = (w - w_hi.astype(F32)).astype(BF16)
    logits = (jnp.dot(h_hi, w_hi, preferred_element_type=F32)
              + jnp.dot(h_lo, w_hi, preferred_element_type=F32)
              + jnp.dot(h_hi, w_lo, preferred_element_type=F32)) + b_ref[...]
    lane = lax.broadcasted_iota(jnp.int32, logits.shape, 1).astype(F32)
    is_g = lane < n_groups
    g_max = jnp.max(jnp.where(is_g, logits, -jnp.inf), axis=-1, keepdims=True)
    g_sum = jnp.sum(jnp.where(is_g, jnp.exp(logits - g_max), 0.0), axis=-1, keepdims=True)
    g_sel = jnp.min(jnp.where(is_g & (logits == g_max), lane, float(LANES)), axis=-1, keepdims=True)
    lo = n_groups + epg * g_sel
    in_grp = (lane >= lo) & (lane < lo + epg)
    t1 = jnp.max(jnp.where(in_grp, logits, -jnp.inf), axis=-1, keepdims=True)
    i1 = jnp.min(jnp.where(in_grp & (logits == t1), lane, float(LANES)), axis=-1, keepdims=True)
    rest = in_grp & (lane != i1)
    t2 = jnp.max(jnp.where(rest, logits, -jnp.inf), axis=-1, keepdims=True)
    i2 = jnp.min(jnp.where(rest & (logits == t2), lane, float(LANES)), axis=-1, keepdims=True)
    d = jnp.exp(t2 - t1)
    gate = 1.0 / g_sum
    w1 = gate * (1.0 / (1.0 + d))
    w2 = gate * (d / (1.0 + d))
    tm = logits.shape[0]
    chosen = jnp.where((lane == i1) | (lane == i2), 1.0, 0.0)
    earlier = jnp.where(lax.broadcasted_iota(jnp.int32, (tm, tm), 1) < lax.broadcasted_iota(jnp.int32, (tm, tm), 0),
                        1.0, 0.0).astype(BF16)
    before = jnp.dot(earlier, chosen.astype(BF16), preferred_element_type=F32) + carry_scr[...]
    rank1 = jnp.sum(jnp.where(lane == i1, before, 0.0), axis=-1, keepdims=True)
    rank2 = jnp.sum(jnp.where(lane == i2, before, 0.0), axis=-1, keepdims=True)
    total = carry_scr[...] + jnp.sum(chosen, axis=0, keepdims=True)
    carry_scr[...] = total
    counts_ref[...] = jnp.broadcast_to(total, counts_ref.shape)
    vals = (i1 - n_groups, i2 - n_groups, w1, w2, rank1, rank2)
    route = jnp.zeros(logits.shape, F32)
    for pos, v in enumerate(vals):
        route = jnp.where(lane == pos, v, route)
    route_ref[...] = route


def moe_router(xs, mods, layer, norm_g, w_r, b_r, *, tm, seq, batch, n_groups, epg):
    t, d = xs.shape
    sh_spec, sc_spec = _mod_specs(layer, (3, 4), tm, seq, batch, d)
    return pl.pallas_call(
        functools.partial(_router_kernel, n_groups=n_groups, epg=epg),
        grid=(t // tm, 1),
        in_specs=[pl.BlockSpec((tm, d), lambda i, j: (i, 0)),
                  pl.BlockSpec((1, d), lambda i, j: (0, 0)),
                  sh_spec, sc_spec,
                  pl.BlockSpec((d, LANES), lambda i, j: (0, 0)),
                  pl.BlockSpec((1, LANES), lambda i, j: (0, 0))],
        out_specs=[pl.BlockSpec((tm, d), lambda i, j: (i, 0)),
                   pl.BlockSpec((tm, LANES), lambda i, j: (i, 0)),
                   pl.BlockSpec((8, LANES), lambda i, j: (0, 0))],
        out_shape=[jax.ShapeDtypeStruct((t, d), F32), jax.ShapeDtypeStruct((t, LANES), F32),
                   jax.ShapeDtypeStruct((8, LANES), F32)],
        scratch_shapes=[pltpu.VMEM((1, LANES), F32)],
        compiler_params=_cparams("arbitrary", "arbitrary"),
        name="moe_router",
    )(xs, norm_g.reshape(1, d), mods, mods, w_r, b_r)


def _issue_row_gather(idx_ref, src_hbm, dst, sem, n_rows):
    def body(r, carry):
        tok = idx_ref[0, 0, r]
        pltpu.make_async_copy(src_hbm.at[pl.ds(tok, 1)], dst.at[pl.ds(r, 1)], sem).start()
        return carry
    lax.fori_loop(0, n_rows, body, 0, unroll=8)


def _wait_row_gather(src_hbm, dst, sem, n_rows):
    pltpu.make_async_copy(src_hbm.at[pl.ds(0, n_rows)], dst, sem).wait()


def _expert_kernel(be_ref, idx_ref, idx_next_ref, h_hbm, win_ref, wout_ref, y_ref,
                   xbuf_a, xbuf_b, sem, win_bf, wout_bf, *, nb, cast_rows, k_chunk):
    i = pl.program_id(0)
    blk, d = xbuf_a.shape
    n_k = d // k_chunk
    rows_per_chunk = blk // n_k

    @pl.when(i == 0)
    def _():
        _issue_row_gather(idx_ref, h_hbm, xbuf_a, sem.at[0], blk)

    expert_changed = (i == 0) | (be_ref[i] != be_ref[jnp.maximum(i - 1, 0)])

    @pl.when(expert_changed)
    def _():
        def cast_in(c, carry):
            r0 = pl.multiple_of(c * cast_rows, cast_rows)
            win_bf[pl.ds(r0, cast_rows), :] = win_ref[0, 0, pl.ds(r0, cast_rows), :].astype(BF16)
            return carry
        lax.fori_loop(0, win_bf.shape[0] // cast_rows, cast_in, 0)

        def cast_out(c, carry):
            r0 = pl.multiple_of(c * cast_rows, cast_rows)
            wout_bf[pl.ds(r0, cast_rows), :] = wout_ref[0, 0, pl.ds(r0, cast_rows), :].astype(BF16)
            return carry
        lax.fori_loop(0, wout_bf.shape[0] // cast_rows, cast_out, 0)

    def run(x_cur, sem_cur, x_next, sem_next):
        _wait_row_gather(h_hbm, x_cur, sem_cur, blk)
        gu = None
        for kk in range(n_k):
            for r in range(kk * rows_per_chunk, (kk + 1) * rows_per_chunk):
                pltpu.make_async_copy(h_hbm.at[pl.ds(idx_next_ref[0, 0, r], 1)], x_next.at[pl.ds(r, 1)],
                                      sem_next).start()
            cols = slice(kk * k_chunk, (kk + 1) * k_chunk)
            part = jnp.dot(x_cur[:, cols].astype(BF16), win_bf[cols, :], preferred_element_type=F32)
            gu = part if gu is None else gu + part
        f = gu.shape[1] // 2
        act = _silu(gu[:, :f]) * gu[:, f:]
        y_ref[...] = jnp.dot(act.astype(BF16), wout_bf[...], preferred_element_type=F32)

        @pl.when(i == nb - 1)
        def _():
            _wait_row_gather(h_hbm, x_next, sem_next, blk)

    @pl.when(i % 2 == 0)
    def _():
        run(xbuf_a, sem.at[0], xbuf_b, sem.at[1])

    @pl.when(i % 2 == 1)
    def _():
        run(xbuf_b, sem.at[1], xbuf_a, sem.at[0])


def moe_experts(h, slot_tok, block_expert, w_in, w_out, layer):
    t, d = h.shape
    f2 = w_in.shape[-1]
    f = f2 // 2
    nb = block_expert.shape[0]
    blk = MOE_BLOCK
    idx = slot_tok.reshape(nb, 1, blk)
    grid_spec = pltpu.PrefetchScalarGridSpec(
        num_scalar_prefetch=1,
        grid=(nb,),
        in_specs=[pl.BlockSpec((1, 1, blk), lambda i, be: (i, 0, 0), memory_space=pltpu.SMEM),
                  pl.BlockSpec((1, 1, blk), lambda i, be: (jnp.minimum(i + 1, nb - 1), 0, 0),
                               memory_space=pltpu.SMEM),
                  pl.BlockSpec(memory_space=pl.ANY),
                  pl.BlockSpec((1, 1, d, f2), lambda i, be: (layer, be[i], 0, 0)),
                  pl.BlockSpec((1, 1, f, d), lambda i, be: (layer, be[i], 0, 0))],
        out_specs=pl.BlockSpec((blk, d), lambda i, be: (i, 0)),
        scratch_shapes=[pltpu.VMEM((blk, d), F32),
                        pltpu.VMEM((blk, d), F32),
                        pltpu.SemaphoreType.DMA((2,)),
                        pltpu.VMEM((d, f2), BF16),
                        pltpu.VMEM((f, d), BF16)],
    )
    return pl.pallas_call(
        functools.partial(_expert_kernel, nb=nb, cast_rows=256, k_chunk=256),
        grid_spec=grid_spec,
        out_shape=jax.ShapeDtypeStruct((nb * blk, d), F32),
        compiler_params=_cparams("arbitrary"),
        name="moe_experts",
    )(block_expert, idx, idx, h, w_in, w_out)


def _combine_kernel(pos_ref, pos_next_ref, yb_hbm, x_ref, route_ref, gate_ref, o_ref, ybuf, sem, *, nt):
    i = pl.program_id(0)
    slot = i % 2
    n_rows = ybuf.shape[1]
    tm = n_rows // MOE_TOP_K

    @pl.when(i == 0)
    def _():
        _issue_row_gather(pos_ref, yb_hbm, ybuf.at[0], sem.at[0], n_rows)

    @pl.when(i + 1 < nt)
    def _():
        _issue_row_gather(pos_next_ref, yb_hbm, ybuf.at[1 - slot], sem.at[1 - slot], n_rows)

    _wait_row_gather(yb_hbm, ybuf.at[slot], sem.at[slot], n_rows)
    route = route_ref[...]
    moe = route[:, 2:3] * ybuf[slot, :tm, :] + route[:, 3:4] * ybuf[slot, tm:, :]
    o_ref[...] = x_ref[...] + gate_ref[0] * moe


def moe_combine(yb, pos, xs, route, mods, layer, *, tm, seq, batch, out_rows):
    t, d = xs.shape
    pos_t = pos.reshape(t // tm, tm, MOE_TOP_K).transpose(0, 2, 1).reshape(t // tm, 1, MOE_TOP_K * tm)
    nt = out_rows // tm

    def seg(i):
        return jnp.minimum((i * tm) // seq, batch)

    return pl.pallas_call(
        functools.partial(_combine_kernel, nt=nt),
        grid=(nt,),
        in_specs=[pl.BlockSpec((1, 1, MOE_TOP_K * tm), lambda i: (i, 0, 0), memory_space=pltpu.SMEM),
                  pl.BlockSpec((1, 1, MOE_TOP_K * tm), lambda i: (jnp.minimum(i + 1, nt - 1), 0, 0),
                               memory_space=pltpu.SMEM),
                  pl.BlockSpec(memory_space=pl.ANY),
                  pl.BlockSpec((tm, d), lambda i: (i, 0)),
                  pl.BlockSpec((tm, LANES), lambda i: (i, 0)),
                  pl.BlockSpec((1, 1, d), lambda i: ((layer * MOD_ROWS + seg(i)) * 6 + 5, 0, 0))],
        out_specs=pl.BlockSpec((tm, d), lambda i: (i, 0)),
        out_shape=jax.ShapeDtypeStruct((out_rows, d), F32),
        scratch_shapes=[pltpu.VMEM((2, MOE_TOP_K * tm, d), F32), pltpu.SemaphoreType.DMA((2,))],
        compiler_params=_cparams("arbitrary"),
        name="moe_combine",
    )(pos_t, pos_t, yb, xs, route, mods)


def _dispatch_plan(route, counts, n_experts, blk):
    t = route.shape[0]
    nk = t * MOE_TOP_K
    padded = (counts + blk - 1) // blk * blk
    pad_end = jnp.cumsum(padded)
    pad_start = pad_end - padded
    nb = -(-(nk + n_experts * (blk - 1)) // blk)
    e = route[:, :MOE_TOP_K].astype(jnp.int32)
    rank = route[:, 2 * MOE_TOP_K:3 * MOE_TOP_K].astype(jnp.int32)
    onehot = e[..., None] == jnp.arange(n_experts, dtype=jnp.int32)
    dest = jnp.sum(jnp.where(onehot, pad_start, 0), axis=-1) + rank
    tok = jnp.arange(nk, dtype=jnp.int32) // MOE_TOP_K
    slot_tok = jnp.zeros((nb * blk,), jnp.int32).at[dest.reshape(nk)].set(tok, unique_indices=True)
    first_row = jnp.arange(nb, dtype=jnp.int32) * blk
    block_expert = jnp.minimum(jnp.sum(pad_end[None, :] <= first_row[:, None], axis=1), n_experts - 1)
    return slot_tok, block_expert.astype(jnp.int32), dest


def _axial_angles(seq, dim):
    n_freq = dim // 4
    freqs = ROPE_THETA ** (-jnp.arange(n_freq, dtype=F32) / n_freq)
    tok = jnp.arange(seq, dtype=jnp.int32)
    row = (tok // GRID_W).astype(F32)
    col = (tok % GRID_W).astype(F32)
    return jnp.concatenate([row[:, None] * freqs, col[:, None] * freqs], axis=-1)


def _stream_table(lat, ctx_fill, batch, n_ctx_rows):
    ctx_rows = jnp.broadcast_to(ctx_fill[None, :], (n_ctx_rows, lat.shape[1]))
    return jnp.concatenate([jnp.tile(lat, (batch, 1)), ctx_rows], axis=0)


def _rope_tables_128(seq, batch, n_ctx_rows):
    ang = _axial_angles(seq, HEAD_DIM)
    cos, sin = jnp.cos(ang), jnp.sin(ang)
    c = jnp.concatenate([cos, cos], axis=-1)
    s = jnp.concatenate([-sin, sin], axis=-1)
    return (_stream_table(c, jnp.ones((HEAD_DIM,), F32), batch, n_ctx_rows),
            _stream_table(s, jnp.zeros((HEAD_DIM,), F32), batch, n_ctx_rows))


def _rope_tables_64(seq, batch, n_ctx_rows):
    ang = _axial_angles(seq, MLA_ROPE)
    cos, sin = jnp.cos(ang), jnp.sin(ang)
    half = MLA_ROPE // 2
    z = jnp.zeros((seq, LANES - MLA_ROPE), F32)
    zh = jnp.zeros((seq, half), F32)
    c = jnp.concatenate([cos, cos, z], axis=-1)
    s1 = jnp.concatenate([-sin, zh, z], axis=-1)
    s2 = jnp.concatenate([zh, sin, z], axis=-1)
    ones_pad = jnp.concatenate([jnp.ones((MLA_ROPE,), F32), jnp.zeros((LANES - MLA_ROPE,), F32)])
    zeros = jnp.zeros((LANES,), F32)
    return (_stream_table(c, ones_pad, batch, n_ctx_rows),
            _stream_table(s1, zeros, batch, n_ctx_rows),
            _stream_table(s2, zeros, batch, n_ctx_rows))


def _na_bias_table(rpb, kr):
    col = jnp.arange(GRID_W, dtype=jnp.int32)
    col_start = jnp.clip(col - NA_COLS // 2, 0, GRID_W - NA_COLS)
    col_in = (col[None, :] >= col_start[:, None]) & (col[None, :] < col_start[:, None] + NA_COLS)
    dc = jnp.clip(col[None, :] - col[:, None] + NA_COLS - 1, 0, 2 * NA_COLS - 2)
    h = rpb.shape[0]
    by_col = jnp.zeros((h, 2 * NA_ROWS - 1, GRID_W, GRID_W), F32)
    for cc in range(2 * NA_COLS - 1):
        by_col = jnp.where(dc[None, None] == cc, rpb[:, :, cc, None, None].astype(F32), by_col)
    by_col = jnp.where(col_in[None, None], by_col, NEG_INF)
    per_shift = [by_col[:, NA_ROWS - 1 - s:NA_ROWS - 1 - s + kr] for s in range(kr)]
    bias = jnp.stack(per_shift, axis=1)
    return bias.transpose(0, 1, 3, 2, 4).reshape(h, kr, GRID_W, kr * GRID_W)


def _pick_tile(*extents):
    for tm in (512, 256, 128):
        if all(e % tm == 0 for e in extents):
            return tm
    raise ValueError("row extents must be multiples of 128")


def kernel(x, c, ctx, c_ctx, mod_w, mod_b, norm_g, gqa_w_qkv, gqa_q_norm, gqa_k_norm, gqa_w_o, na_w_qkv, na_q_norm, na_k_norm, na_rpb, na_w_o, mla_w_down, mla_q_lora_norm, mla_w_uq, mla_kv_lora_norm, mla_w_ukv, mla_q_norm, mla_k_norm, mla_w_o, moe_w_group, moe_b_group, moe_w_expert, moe_b_expert, moe_w_in, moe_w_out):
    batch, seq, d = x.shape
    n_ctx = ctx.shape[1]
    depth = mod_w.shape[0]
    heads = d // HEAD_DIM
    kv_heads = heads // 4
    n_lat = batch * seq
    n_ctx_rows = batch * n_ctx
    assert batch + 1 <= MOD_ROWS and n_lat % n_ctx == 0 and seq % GRID_W == 0
    tm = _pick_tile(seq, n_ctx_rows)
    tq = min(FLASH_TQ, seq)
    tk = min(FLASH_TK, seq)
    assert seq % tq == 0 and seq % (2 * tk) == 0
    tn_d = min(512, d)
    dims = dict(tm=tm, seq=seq, batch=batch)

    xs = jnp.concatenate([x.reshape(n_lat, d), ctx.reshape(n_ctx_rows, d)], axis=0)

    cond = jnp.zeros((MOD_ROWS, d), F32).at[:batch].set(c).at[batch].set(c_ctx)
    mods = adaln_all(cond, mod_w, mod_b).reshape(depth * MOD_ROWS * 6, 1, d)

    cos_a, sin_a = _rope_tables_128(seq, batch, n_ctx_rows)
    c64, s64a, s64b = _rope_tables_64(seq, batch, n_ctx_rows)

    n_groups = moe_w_group.shape[-1]
    epg = moe_w_expert.shape[-1]
    n_experts = n_groups * epg
    assert n_groups + n_experts <= LANES

    for i in range(depth):
        kind, j = i % 3, i // 3
        if kind == 0:
            scale = HEAD_DIM ** -0.5 * LOG2E
            nq, nkv = heads * HEAD_DIM, kv_heads * HEAD_DIM
            gain = jnp.concatenate([jnp.tile(gqa_q_norm[j] * scale, heads), jnp.tile(gqa_k_norm[j], kv_heads),
                                    jnp.ones((nkv,), F32)]).reshape(1, -1)
            big = j == 1 and (n_lat + n_ctx_rows) % 1024 == 0 and seq % 1024 == 0 and n_ctx_rows % 1024 == 0
            qkv = qkv_project(xs, mods, i, norm_g[i, 0], gqa_w_qkv[j].astype(BF16), gain, cos_a, sin_a,
                              n_norm_cols=nq + nkv, tn=min(512, nkv), rope=True,
                              **dict(dims, tm=1024 if big else tm))
            cols = dict(dk=HEAD_DIM, q_col0=0, k_col0=heads, v_col0=heads + kv_heads, kv_group=heads // kv_heads)
            o_ctx = ctx_attention(qkv, qkv, qkv, batch=batch, seq=seq, ctx=n_ctx, heads=heads, **cols)
            o = dense_attention(qkv, qkv, qkv, o_ctx, batch=batch, seq=seq, ctx=n_ctx, heads=heads,
                                tq=1024 if big else tq, tk=tk, **cols)
            w_o = gqa_w_o[j]
        elif kind == 1:
            scale = HEAD_DIM ** -0.5 * LOG2E
            nq = heads * HEAD_DIM
            gain = jnp.concatenate([jnp.tile(na_q_norm[j] * scale, heads), jnp.tile(na_k_norm[j], heads),
                                    jnp.ones((nq,), F32)]).reshape(1, -1)
            qkv = qkv_project(xs, mods, i, norm_g[i, 0], na_w_qkv[j].astype(BF16), gain, cos_a, sin_a,
                              n_norm_cols=2 * nq, tn=tn_d, rope=False, **dims)
            rows = seq // GRID_W
            tb = _na_bias_table(na_rpb[j], min(NA_ROWS, rows)) * LOG2E
            o_ctx = ctx_attention(qkv, qkv, qkv, batch=batch, seq=seq, ctx=n_ctx, heads=heads, dk=HEAD_DIM,
                                  q_col0=0, k_col0=heads, v_col0=2 * heads, kv_group=1)
            o = na_attention(qkv, tb, o_ctx, batch=batch, seq=seq, ctx=n_ctx, heads=heads)
            w_o = na_w_o[j]
        else:
            scale = MLA_QK ** -0.5 * LOG2E
            q_lora = mla_q_lora_norm.shape[-1]
            kv_lora = mla_kv_lora_norm.shape[-1]
            assert q_lora % kv_lora == 0 and (q_lora + kv_lora) % LANES == 0
            n_down = q_lora + kv_lora + LANES
            w_down = jnp.pad(mla_w_down[j], ((0, 0), (0, n_down - mla_w_down.shape[-1]))).astype(BF16)
            cfull = mla_down(xs, mods, i, norm_g[i, 0], w_down, **dims)
            pad = MLA_HEAD_PAD - MLA_QK
            w_uq = jnp.pad(mla_w_uq[j].reshape(q_lora, heads, MLA_QK), ((0, 0), (0, 0), (0, pad)))
            w_uq = w_uq.reshape(q_lora, heads * MLA_HEAD_PAD).astype(BF16)
            q_gain = jnp.tile(jnp.pad(mla_q_norm[j] * scale, (0, pad)), heads).reshape(1, -1)
            qa = mla_q_project(cfull, mla_q_lora_norm[j], w_uq, q_gain, c64, s64a, s64b, tm=tm,
                               tn=min(512, heads * MLA_HEAD_PAD))
            gain_n = mla_k_norm[j, :MLA_NOPE].reshape(1, LANES)
            gain_t = jnp.pad(mla_k_norm[j, MLA_NOPE:], (0, LANES - MLA_ROPE)).reshape(1, LANES)
            ka, va = mla_kv_project(cfull, mla_kv_lora_norm[j], mla_w_ukv[j].astype(BF16), gain_n, gain_t,
                                    c64, s64a, s64b, tm=tm, q_lora=q_lora, heads_per_step=2)
            cols = dict(dk=MLA_HEAD_PAD, q_col0=0, k_col0=0, v_col0=0, kv_group=1)
            o_ctx = ctx_attention(qa, ka, va, batch=batch, seq=seq, ctx=n_ctx, heads=heads, **cols)
            o = dense_attention(qa, ka, va, o_ctx, batch=batch, seq=seq, ctx=n_ctx, heads=heads, tq=tq, tk=tk, **cols)
            w_o = mla_w_o[j]

        ptm, ptn = {0: (512, 512), 1: (512, 2048), 2: (1024, 512), 3: (256, 2048)}[i % 4]
        if (n_lat + n_ctx_rows) % ptm or seq % ptm or n_ctx_rows % ptm or d % ptn:
            ptm, ptn = tm, tn_d
        xs = proj_residual(o, w_o.astype(BF16), xs, mods, i, tn=ptn, **dict(dims, tm=ptm))

        w_r = jnp.concatenate([moe_w_group[i], moe_w_expert[i].transpose(1, 0, 2).reshape(d, n_experts)], axis=1)
        w_r = jnp.pad(w_r, ((0, 0), (0, LANES - w_r.shape[1])))
        b_r = jnp.pad(jnp.concatenate([moe_b_group[i], moe_b_expert[i].reshape(-1)]),
                      (0, LANES - n_groups - n_experts)).reshape(1, LANES)
        h2, route, lane_counts = moe_router(xs, mods, i, norm_g[i, 1], w_r, b_r, n_groups=n_groups, epg=epg, **dims)
        counts = lane_counts[0, n_groups:n_groups + n_experts].astype(jnp.int32)
        slot_tok, block_expert, pos = _dispatch_plan(route, counts, n_experts, MOE_BLOCK)
        yb = moe_experts(h2, slot_tok, block_expert, moe_w_in, moe_w_out, i)
        out_rows = n_lat if i == depth - 1 else n_lat + n_ctx_rows
        xs = moe_combine(yb, pos, xs, route, mods, i, tm=min(COMBINE_TM, tm), seq=seq, batch=batch,
                         out_rows=out_rows)

    return xs.reshape(batch, seq, d)
```

```python
import functools

import jax
import jax.numpy as jnp
from jax import lax
from jax.experimental import pallas as pl
from jax.experimental.pallas import tpu as pltpu

F32 = jnp.float32
BF16 = jnp.bfloat16

GRID_W = 64
HEAD_DIM = 128
ROPE_THETA = 10000.0
NORM_EPS = 1e-6
NEG_INF = -1e30
NA_ROWS = 8
NA_COLS = 16
MLA_NOPE = 128
MLA_ROPE = 64
MLA_V = 128
MLA_QK = MLA_NOPE + MLA_ROPE
MLA_HEAD_PAD = 256
MOE_TOP_K = 2
LANES = 128
MOD_ROWS = 8
MOE_BLOCK = 256
COMBINE_TM = 256
FLASH_TQ = 1024
FLASH_TK = 512
FLASH_ROW_GROUP = 32
LOG2E = 1.4426950408889634
VMEM_LIMIT = 52 * 1024 * 1024


def _cparams(*sem):
    return pltpu.CompilerParams(dimension_semantics=sem, vmem_limit_bytes=VMEM_LIMIT)


def _silu(v):
    return v / (1.0 + jnp.exp(-v))


def _adaln_kernel(c_ref, w_ref, b_ref, o_ref):
    cond = _silu(c_ref[...]).astype(BF16)
    o_ref[0] = jnp.dot(cond, w_ref[0].astype(BF16), preferred_element_type=F32) + b_ref[0]


def adaln_all(cond, mod_w, mod_b):
    depth, d, n = mod_w.shape
    tn = 1024
    return pl.pallas_call(
        _adaln_kernel,
        grid=(depth, n // tn),
        in_specs=[pl.BlockSpec((MOD_ROWS, d), lambda l, j: (0, 0)),
                  pl.BlockSpec((1, d, tn), lambda l, j: (l, 0, j)),
                  pl.BlockSpec((1, 1, tn), lambda l, j: (l, 0, j))],
        out_specs=pl.BlockSpec((1, MOD_ROWS, tn), lambda l, j: (l, 0, j)),
        out_shape=jax.ShapeDtypeStruct((depth, MOD_ROWS, n), F32),
        compiler_params=_cparams("arbitrary", "arbitrary"),
        name="adaln",
    )(cond, mod_w, mod_b.reshape(depth, 1, n))


def _rms(x, g):
    ms = jnp.mean(x * x, axis=-1, keepdims=True)
    return x * lax.rsqrt(ms + NORM_EPS) * g


def _rope128(y, c, s):
    return y * c + pltpu.roll(y, 64, 1) * s


def _rope64(y, c, s1, s2):
    return y * c + pltpu.roll(y, 96, 1) * s1 + pltpu.roll(y, 32, 1) * s2


def _qkv_kernel(x_ref, g_ref, sh_ref, sc_ref, w_ref, gain_ref, c_ref, s_ref, o_ref, h_scr, *, n_norm_tiles, rope):
    j = pl.program_id(1)

    @pl.when(j == 0)
    def _():
        h = _rms(x_ref[...], g_ref[...]) * (1.0 + sc_ref[0]) + sh_ref[0]
        h_scr[...] = h.astype(BF16)

    y = jnp.dot(h_scr[...], w_ref[j], preferred_element_type=F32)
    n_heads = y.shape[1] // HEAD_DIM

    @pl.when(j < n_norm_tiles)
    def _():
        for hh in range(n_heads):
            sl = slice(hh * HEAD_DIM, (hh + 1) * HEAD_DIM)
            yh = _rms(y[:, sl], gain_ref[:, sl])
            if rope:
                yh = _rope128(yh, c_ref[...], s_ref[...])
            o_ref[:, sl] = yh.astype(o_ref.dtype)

    @pl.when(j >= n_norm_tiles)
    def _():
        o_ref[...] = y.astype(o_ref.dtype)


def _mod_specs(layer, which, tm, seq, batch, d):
    def seg(i):
        return jnp.minimum((i * tm) // seq, batch)
    return [pl.BlockSpec((1, 1, d), lambda i, j, w=w: ((layer * MOD_ROWS + seg(i)) * 6 + w, 0, 0)) for w in which]


def qkv_project(xs, mods, layer, norm_g, w_bf, gain, cos_t, sin_t, *, tm, seq, batch, n_norm_cols, tn, rope):
    t, d = xs.shape
    n = w_bf.shape[1]
    sh_spec, sc_spec = _mod_specs(layer, (0, 1), tm, seq, batch, d)
    kern = functools.partial(_qkv_kernel, n_norm_tiles=n_norm_cols // tn, rope=rope)
    w_tiles = w_bf.reshape(d, n // tn, tn).transpose(1, 0, 2)
    return pl.pallas_call(
        kern,
        grid=(t // tm, n // tn),
        in_specs=[pl.BlockSpec((tm, d), lambda i, j: (i, 0)),
                  pl.BlockSpec((1, d), lambda i, j: (0, 0)),
                  sh_spec, sc_spec,
                  pl.BlockSpec((n // tn, d, tn), lambda i, j: (0, 0, 0), pipeline_mode=pl.Buffered(1)),
                  pl.BlockSpec((1, tn), lambda i, j: (0, j)),
                  pl.BlockSpec((tm, HEAD_DIM), lambda i, j: (i, 0)),
                  pl.BlockSpec((tm, HEAD_DIM), lambda i, j: (i, 0))],
        out_specs=pl.BlockSpec((tm, tn), lambda i, j: (i, j)),
        out_shape=jax.ShapeDtypeStruct((t, n), BF16),
        scratch_shapes=[pltpu.VMEM((tm, d), BF16)],
        compiler_params=_cparams("arbitrary", "arbitrary"),
        name="qkv_project",
    )(xs, norm_g.reshape(1, d), mods, mods, w_tiles, gain, cos_t, sin_t)


def _down_kernel(x_ref, g_ref, sh_ref, sc_ref, w_ref, o_ref):
    h = _rms(x_ref[...], g_ref[...]) * (1.0 + sc_ref[0]) + sh_ref[0]
    o_ref[...] = jnp.dot(h.astype(BF16), w_ref[...], preferred_element_type=F32)


def mla_down(xs, mods, layer, norm_g, w_bf, *, tm, seq, batch):
    t, d = xs.shape
    n = w_bf.shape[1]
    sh_spec, sc_spec = _mod_specs(layer, (0, 1), tm, seq, batch, d)
    return pl.pallas_call(
        _down_kernel,
        grid=(t // tm, 1),
        in_specs=[pl.BlockSpec((tm, d), lambda i, j: (i, 0)),
                  pl.BlockSpec((1, d), lambda i, j: (0, 0)),
                  sh_spec, sc_spec,
                  pl.BlockSpec((d, n), lambda i, j: (0, 0))],
        out_specs=pl.BlockSpec((tm, n), lambda i, j: (i, 0)),
        out_shape=jax.ShapeDtypeStruct((t, n), F32),
        compiler_params=_cparams("arbitrary", "arbitrary"),
        name="mla_down",
    )(xs, norm_g.reshape(1, d), mods, mods, w_bf)


def _mla_q_kernel(cq_ref, g_ref, w_ref, gain_ref, c_ref, s1_ref, s2_ref, o_ref):
    h = _rms(cq_ref[...], g_ref[...]).astype(BF16)
    y = jnp.dot(h, w_ref[...], preferred_element_type=F32)
    for hh in range(y.shape[1] // MLA_HEAD_PAD):
        lo = hh * MLA_HEAD_PAD
        yh = y[:, lo:lo + MLA_HEAD_PAD]
        ms = jnp.sum(yh * yh, axis=-1, keepdims=True) * (1.0 / MLA_QK)
        yh = yh * lax.rsqrt(ms + NORM_EPS) * gain_ref[:, lo:lo + MLA_HEAD_PAD]
        o_ref[:, lo:lo + MLA_NOPE] = yh[:, :MLA_NOPE].astype(o_ref.dtype)
        tail = _rope64(yh[:, MLA_NOPE:], c_ref[...], s1_ref[...], s2_ref[...])
        o_ref[:, lo + MLA_NOPE:lo + MLA_HEAD_PAD] = tail.astype(o_ref.dtype)


def mla_q_project(cfull, q_lora_g, w_bf, gain, c_t, s1_t, s2_t, *, tm, tn):
    t = cfull.shape[0]
    kq, n = w_bf.shape
    return pl.pallas_call(
        _mla_q_kernel,
        grid=(t // tm, n // tn),
        in_specs=[pl.BlockSpec((tm, kq), lambda i, j: (i, 0)),
                  pl.BlockSpec((1, kq), lambda i, j: (0, 0)),
                  pl.BlockSpec((kq, tn), lambda i, j: (0, j)),
                  pl.BlockSpec((1, tn), lambda i, j: (0, j)),
                  pl.BlockSpec((tm, LANES), lambda i, j: (i, 0)),
                  pl.BlockSpec((tm, LANES), lambda i, j: (i, 0)),
                  pl.BlockSpec((tm, LANES), lambda i, j: (i, 0))],
        out_specs=pl.BlockSpec((tm, tn), lambda i, j: (i, j)),
        out_shape=jax.ShapeDtypeStruct((t, n), BF16),
        compiler_params=_cparams("arbitrary", "arbitrary"),
        name="mla_q_project",
    )(cfull, q_lora_g.reshape(1, kq), w_bf, gain, c_t, s1_t, s2_t)


def _mla_kv_kernel(ckv_ref, g_ref, kr_ref, w_ref, gn_ref, gt_ref, c_ref, s1_ref, s2_ref, k_ref, v_ref):
    h = _rms(ckv_ref[...], g_ref[...]).astype(BF16)
    y = jnp.dot(h, w_ref[...], preferred_element_type=F32)
    kr = kr_ref[...]
    ss_rope = jnp.sum(kr * kr, axis=-1, keepdims=True)
    rot = _rope64(kr * gt_ref[...], c_ref[...], s1_ref[...], s2_ref[...])
    per_head = MLA_NOPE + MLA_V
    for hh in range(y.shape[1] // per_head):
        kn = y[:, hh * per_head:hh * per_head + MLA_NOPE]
        ms = (jnp.sum(kn * kn, axis=-1, keepdims=True) + ss_rope) * (1.0 / MLA_QK)
        rs = lax.rsqrt(ms + NORM_EPS)
        lo = hh * MLA_HEAD_PAD
        k_ref[:, lo:lo + MLA_NOPE] = (kn * rs * gn_ref[...]).astype(k_ref.dtype)
        k_ref[:, lo + MLA_NOPE:lo + MLA_HEAD_PAD] = (rot * rs).astype(k_ref.dtype)
        v_ref[:, hh * MLA_V:(hh + 1) * MLA_V] = y[:, hh * per_head + MLA_NOPE:(hh + 1) * per_head].astype(v_ref.dtype)


def mla_kv_project(cfull, kv_lora_g, w_bf, gain_n, gain_t, c_t, s1_t, s2_t, *, tm, q_lora, heads_per_step):
    t = cfull.shape[0]
    kkv, n = w_bf.shape
    per_head = MLA_NOPE + MLA_V
    tn = heads_per_step * per_head
    heads = n // per_head
    return pl.pallas_call(
        _mla_kv_kernel,
        grid=(t // tm, n // tn),
        in_specs=[pl.BlockSpec((tm, kkv), lambda i, j: (i, q_lora // kkv)),
                  pl.BlockSpec((1, kkv), lambda i, j: (0, 0)),
                  pl.BlockSpec((tm, LANES), lambda i, j: (i, (q_lora + kkv) // LANES)),
                  pl.BlockSpec((kkv, tn), lambda i, j: (0, j)),
                  pl.BlockSpec((1, LANES), lambda i, j: (0, 0)),
                  pl.BlockSpec((1, LANES), lambda i, j: (0, 0)),
                  pl.BlockSpec((tm, LANES), lambda i, j: (i, 0)),
                  pl.BlockSpec((tm, LANES), lambda i, j: (i, 0)),
                  pl.BlockSpec((tm, LANES), lambda i, j: (i, 0))],
        out_specs=[pl.BlockSpec((tm, heads_per_step * MLA_HEAD_PAD), lambda i, j: (i, j)),
                   pl.BlockSpec((tm, heads_per_step * MLA_V), lambda i, j: (i, j))],
        out_shape=[jax.ShapeDtypeStruct((t, heads * MLA_HEAD_PAD), BF16),
                   jax.ShapeDtypeStruct((t, heads * MLA_V), BF16)],
        compiler_params=_cparams("arbitrary", "arbitrary"),
        name="mla_kv_project",
    )(cfull, kv_lora_g.reshape(1, kkv), cfull, w_bf, gain_n, gain_t, c_t, s1_t, s2_t)


_NT = (((1,), (1,)), ((), ()))


def _flash_scores(q, k, s_buf):
    s_buf[:, :k.shape[0]] = lax.dot_general(q, k, _NT, preferred_element_type=F32)


def _flash_update(s_scr, v, p_scr, m_scr, l_scr, acc_scr, rg):
    tq = s_scr.shape[0]
    w = v.shape[0]
    n_tiles = w // LANES

    for g in range(tq // rg):
        rows_ = slice(g * rg, (g + 1) * rg)
        tiles = [s_scr[rows_, t * LANES:(t + 1) * LANES] for t in range(n_tiles)]
        mx = tiles[0]
        for st in tiles[1:]:
            mx = jnp.maximum(mx, st)
        m_prev = m_scr[rows_, :]
        m_next = jnp.maximum(m_prev, jnp.max(mx, axis=-1, keepdims=True))
        alpha = jnp.exp2(m_prev - m_next)
        psum = None
        for t, st in enumerate(tiles):
            p = jnp.exp2(st - m_next)
            p_scr[rows_, t * LANES:(t + 1) * LANES] = p.astype(p_scr.dtype)
            psum = p if psum is None else psum + p
        l_scr[rows_, :] = alpha * l_scr[rows_, :] + psum
        acc_scr[rows_, :] = alpha * acc_scr[rows_, :]
        m_scr[rows_, :] = m_next
    acc_scr[...] += jnp.dot(p_scr[:, :w], v, preferred_element_type=F32)


def _flash_kernel(*refs, n_lat_chunks, tk, rg):
    if n_lat_chunks:
        q_ref, kc_ref, vc_ref, k_ref, v_ref, _, o_ref, s0, s1, p0, p1, m_scr, l_scr, acc_scr = refs
    else:
        q_ref, kc_ref, vc_ref, _, o_ref, s0, s1, p0, p1, m_scr, l_scr, acc_scr = refs
    state = (m_scr, l_scr, acc_scr, rg)
    m_scr[...] = jnp.full(m_scr.shape, -jnp.inf, F32)
    l_scr[...] = jnp.zeros(l_scr.shape, F32)
    acc_scr[...] = jnp.zeros(acc_scr.shape, F32)
    q = q_ref[...]
    _flash_scores(q, kc_ref[...], s0)
    if n_lat_chunks:
        assert n_lat_chunks % 2 == 0

        def k_chunk(c):
            return k_ref[pl.ds(pl.multiple_of(c * tk, tk), tk), :]

        def v_chunk(c):
            return v_ref[pl.ds(pl.multiple_of(c * tk, tk), tk), :]

        _flash_scores(q, k_chunk(0), s1)
        _flash_update(s0, vc_ref[...], p0, *state)

        def body(j, carry):
            _flash_scores(q, k_chunk(2 * j + 1), s0)
            _flash_update(s1, v_chunk(2 * j), p1, *state)
            _flash_scores(q, k_chunk(2 * j + 2), s1)
            _flash_update(s0, v_chunk(2 * j + 1), p0, *state)
            return carry
        lax.fori_loop(0, n_lat_chunks // 2 - 1, body, 0)
        _flash_scores(q, k_chunk(n_lat_chunks - 1), s0)
        _flash_update(s1, v_chunk(n_lat_chunks - 2), p1, *state)
        _flash_update(s0, v_chunk(n_lat_chunks - 1), p0, *state)
    else:
        _flash_update(s0, vc_ref[...], p0, *state)
    l = jnp.sum(l_scr[...], axis=-1, keepdims=True)
    o_ref[...] = (acc_scr[...] / l).astype(o_ref.dtype)


def _flash_scratch(tq, w):
    return [pltpu.VMEM((tq, w), F32), pltpu.VMEM((tq, w), F32),
            pltpu.VMEM((tq, w), BF16), pltpu.VMEM((tq, w), BF16),
            pltpu.VMEM((tq, LANES), F32), pltpu.VMEM((tq, LANES), F32), pltpu.VMEM((tq, LANES), F32)]


def ctx_attention(qa, ka, va, *, batch, seq, ctx, heads, dk, q_col0, k_col0, v_col0, kv_group):
    t = qa.shape[0]
    ctx_blk0 = (batch * seq) // ctx
    o_shape = jax.ShapeDtypeStruct((t, heads * LANES), BF16)
    return pl.pallas_call(
        functools.partial(_flash_kernel, n_lat_chunks=0, tk=0, rg=FLASH_ROW_GROUP),
        grid=(batch, heads),
        in_specs=[pl.BlockSpec((ctx, dk), lambda b, h: (ctx_blk0 + b, q_col0 + h)),
                  pl.BlockSpec((ctx, dk), lambda b, h: (ctx_blk0 + b, k_col0 + h // kv_group)),
                  pl.BlockSpec((ctx, LANES), lambda b, h: (ctx_blk0 + b, v_col0 + h // kv_group)),
                  pl.BlockSpec(memory_space=pl.ANY)],
        out_specs=pl.BlockSpec((ctx, LANES), lambda b, h: (ctx_blk0 + b, h)),
        out_shape=o_shape,
        input_output_aliases={3: 0},
        scratch_shapes=_flash_scratch(ctx, ctx),
        compiler_params=_cparams("arbitrary", "arbitrary"),
        name="ctx_attention",
    )(qa, ka, va, jnp.zeros(o_shape.shape, o_shape.dtype))


def dense_attention(qa, ka, va, o_ctx, *, batch, seq, ctx, heads, dk, q_col0, k_col0, v_col0, kv_group, tq, tk):
    nq = seq // tq
    ctx_blk0 = (batch * seq) // ctx
    return pl.pallas_call(
        functools.partial(_flash_kernel, n_lat_chunks=seq // tk, tk=tk, rg=FLASH_ROW_GROUP),
        grid=(batch, heads, nq),
        in_specs=[pl.BlockSpec((tq, dk), lambda b, h, i: (b * nq + i, q_col0 + h)),
                  pl.BlockSpec((ctx, dk), lambda b, h, i: (ctx_blk0 + b, k_col0 + h // kv_group)),
                  pl.BlockSpec((ctx, LANES), lambda b, h, i: (ctx_blk0 + b, v_col0 + h // kv_group)),
                  pl.BlockSpec((seq, dk), lambda b, h, i: (b, k_col0 + h // kv_group)),
                  pl.BlockSpec((seq, LANES), lambda b, h, i: (b, v_col0 + h // kv_group)),
                  pl.BlockSpec(memory_space=pl.ANY)],
        out_specs=pl.BlockSpec((tq, LANES), lambda b, h, i: (b * nq + i, h)),
        out_shape=jax.ShapeDtypeStruct(o_ctx.shape, o_ctx.dtype),
        input_output_aliases={5: 0},
        scratch_shapes=_flash_scratch(tq, max(tk, ctx)),
        compiler_params=_cparams("arbitrary", "arbitrary", "arbitrary"),
        name="dense_attention",
    )(qa, ka, va, ka, va, o_ctx)


def _na_kernel(q_ref, k_ref, v_ref, kc_ref, vc_ref, tb_ref, _, o_ref, *, rows, kr):
    kc = kc_ref[...]
    vc = vc_ref[...]
    nt = (((1,), (1,)), ((), ()))

    def body(r, carry):
        rs = jnp.clip(r - kr // 2, 0, rows - kr)
        q0 = pl.multiple_of(r * GRID_W, GRID_W)
        k0 = pl.multiple_of(rs * GRID_W, GRID_W)
        q = q_ref[pl.ds(q0, GRID_W), :]
        kw = k_ref[pl.ds(k0, kr * GRID_W), :]
        vw = v_ref[pl.ds(k0, kr * GRID_W), :]
        s_win = lax.dot_general(q, kw, nt, preferred_element_type=F32) + tb_ref[0, r - rs]
        s_ctx = lax.dot_general(q, kc, nt, preferred_element_type=F32)
        m = jnp.maximum(jnp.max(s_win, axis=-1, keepdims=True), jnp.max(s_ctx, axis=-1, keepdims=True))
        p_win = jnp.exp2(s_win - m)
        p_ctx = jnp.exp2(s_ctx - m)
        l = jnp.sum(p_win, axis=-1, keepdims=True) + jnp.sum(p_ctx, axis=-1, keepdims=True)
        o = (jnp.dot(p_win.astype(vw.dtype), vw, preferred_element_type=F32)
             + jnp.dot(p_ctx.astype(vc.dtype), vc, preferred_element_type=F32))
        o_ref[pl.ds(q0, GRID_W), :] = (o / l).astype(o_ref.dtype)
        return carry

    lax.fori_loop(0, rows, body, 0, unroll=4)


def na_attention(qkv, tb, o_ctx, *, batch, seq, ctx, heads):
    t = qkv.shape[0]
    rows = seq // GRID_W
    kr = min(NA_ROWS, rows)
    ctx_blk0 = (batch * seq) // ctx
    d = HEAD_DIM
    return pl.pallas_call(
        functools.partial(_na_kernel, rows=rows, kr=kr),
        grid=(batch, heads),
        in_specs=[pl.BlockSpec((seq, d), lambda b, h: (b, h)),
                  pl.BlockSpec((seq, d), lambda b, h: (b, heads + h)),
                  pl.BlockSpec((seq, d), lambda b, h: (b, 2 * heads + h)),
                  pl.BlockSpec((ctx, d), lambda b, h: (ctx_blk0 + b, heads + h)),
                  pl.BlockSpec((ctx, d), lambda b, h: (ctx_blk0 + b, 2 * heads + h)),
                  pl.BlockSpec((1, kr, GRID_W, kr * GRID_W), lambda b, h: (h, 0, 0, 0)),
                  pl.BlockSpec(memory_space=pl.ANY)],
        out_specs=pl.BlockSpec((seq, d), lambda b, h: (b, h)),
        out_shape=jax.ShapeDtypeStruct((t, heads * d), BF16),
        input_output_aliases={6: 0},
        compiler_params=_cparams("arbitrary", "arbitrary"),
        name="na_attention",
    )(qkv, qkv, qkv, qkv, qkv, tb, o_ctx)


def _pack_bf16_pairs(x):
    n = x.shape[1] // 2
    lo = lax.bitcast_convert_type(x[:, :n].astype(BF16).astype(F32), jnp.uint32)
    hi = lax.bitcast_convert_type(x[:, n:].astype(BF16).astype(F32), jnp.uint32)
    return (lo >> 16) | (hi & jnp.uint32(0xFFFF0000))


def _unpack_lo(words):
    return lax.bitcast_convert_type(words << 16, F32)


def _unpack_hi(words):
    return lax.bitcast_convert_type(words & jnp.uint32(0xFFFF0000), F32)


def _proj_router_kernel(o_ref, wo_ref, x_ref, gate_ref, g_ref, sh_ref, sc_ref, w_ref, b_ref,
                        xo_ref, h_ref, route_ref, counts_ref, carry_scr, *, n_groups, epg):
    @pl.when(pl.program_id(0) == 0)
    def _():
        carry_scr[...] = jnp.zeros(carry_scr.shape, F32)

    x_new = x_ref[...] + gate_ref[0] * jnp.dot(o_ref[...], wo_ref[...], preferred_element_type=F32)
    xo_ref[...] = x_new
    h = _rms(x_new, g_ref[...]) * (1.0 + sc_ref[0]) + sh_ref[0]
    h_ref[...] = _pack_bf16_pairs(h)
    w = w_ref[...]
    h_hi = h.astype(BF16)
    h_lo = (h - h_hi.astype(F32)).astype(BF16)
    w_hi = w.astype(BF16)
    w_lo = (w - w_hi.astype(F32)).astype(BF16)
    logits = (jnp.dot(h_hi, w_hi, preferred_element_type=F32)
              + jnp.dot(h_lo, w_hi, preferred_element_type=F32)
              + jnp.dot(h_hi, w_lo, preferred_element_type=F32)) + b_ref[...]
    lane = lax.broadcasted_iota(jnp.int32, logits.shape, 1).astype(F32)
    is_g = lane < n_groups
    g_max = jnp.max(jnp.where(is_g, logits, -jnp.inf), axis=-1, keepdims=True)
    g_sum = jnp.sum(jnp.where(is_g, jnp.exp(logits - g_max), 0.0), axis=-1, keepdims=True)
    g_sel = jnp.min(jnp.where(is_g & (logits == g_max), lane, float(LANES)), axis=-1, keepdims=True)
    lo = n_groups + epg * g_sel
    in_grp = (lane >= lo) & (lane < lo + epg)
    t1 = jnp.max(jnp.where(in_grp, logits, -jnp.inf), axis=-1, keepdims=True)
    i1 = jnp.min(jnp.where(in_grp & (logits == t1), lane, float(LANES)), axis=-1, keepdims=True)
    rest = in_grp & (lane != i1)
    t2 = jnp.max(jnp.where(rest, logits, -jnp.inf), axis=-1, keepdims=True)
    i2 = jnp.min(jnp.where(rest & (logits == t2), lane, float(LANES)), axis=-1, keepdims=True)
    d = jnp.exp(t2 - t1)
    gate = 1.0 / g_sum
    w1 = gate * (1.0 / (1.0 + d))
    w2 = gate * (d / (1.0 + d))
    tm = logits.shape[0]
    chosen = jnp.where((lane == i1) | (lane == i2), 1.0, 0.0)
    earlier = jnp.where(lax.broadcasted_iota(jnp.int32, (tm, tm), 1) < lax.broadcasted_iota(jnp.int32, (tm, tm), 0),
                        1.0, 0.0).astype(BF16)
    before = jnp.dot(earlier, chosen.astype(BF16), preferred_element_type=F32) + carry_scr[...]
    rank1 = jnp.sum(jnp.where(lane == i1, before, 0.0), axis=-1, keepdims=True)
    rank2 = jnp.sum(jnp.where(lane == i2, before, 0.0), axis=-1, keepdims=True)
    total = carry_scr[...] + jnp.sum(chosen, axis=0, keepdims=True)
    carry_scr[...] = total
    counts_ref[...] = jnp.broadcast_to(total, counts_ref.shape)
    vals = (i1 - n_groups, i2 - n_groups, w1, w2, rank1, rank2)
    route = jnp.zeros(logits.shape, F32)
    for pos, v in enumerate(vals):
        route = jnp.where(lane == pos, v, route)
    route_ref[...] = route


def proj_residual_router(o, wo_bf, xs, mods, layer, norm_g, w_r, b_r, *, tm, seq, batch, n_groups, epg):
    t, d = xs.shape
    k = o.shape[1]
    gate_spec, sh_spec, sc_spec = _mod_specs(layer, (2, 3, 4), tm, seq, batch, d)
    return pl.pallas_call(
        functools.partial(_proj_router_kernel, n_groups=n_groups, epg=epg),
        grid=(t // tm, 1),
        in_specs=[pl.BlockSpec((tm, k), lambda i, j: (i, 0)),
                  pl.BlockSpec((k, d), lambda i, j: (0, 0), pipeline_mode=pl.Buffered(1)),
                  pl.BlockSpec((tm, d), lambda i, j: (i, 0)),
                  gate_spec,
                  pl.BlockSpec((1, d), lambda i, j: (0, 0)),
                  sh_spec, sc_spec,
                  pl.BlockSpec((d, LANES), lambda i, j: (0, 0)),
                  pl.BlockSpec((1, LANES), lambda i, j: (0, 0))],
        out_specs=[pl.BlockSpec((tm, d), lambda i, j: (i, 0)),
                   pl.BlockSpec((tm, d // 2), lambda i, j: (i, 0)),
                   pl.BlockSpec((tm, LANES), lambda i, j: (i, 0)),
                   pl.BlockSpec((8, LANES), lambda i, j: (0, 0))],
        out_shape=[jax.ShapeDtypeStruct((t, d), F32), jax.ShapeDtypeStruct((t, d // 2), jnp.uint32),
                   jax.ShapeDtypeStruct((t, LANES), F32), jax.ShapeDtypeStruct((8, LANES), F32)],
        scratch_shapes=[pltpu.VMEM((1, LANES), F32)],
        compiler_params=_cparams("arbitrary", "arbitrary"),
        name="proj_residual_router",
    )(o, wo_bf, xs, mods, norm_g.reshape(1, d), mods, mods, w_r, b_r)


def _issue_row_gather(idx_ref, src_hbm, dst, sem, n_rows):
    def body(r, carry):
        tok = idx_ref[0, 0, r]
        pltpu.make_async_copy(src_hbm.at[pl.ds(tok, 1)], dst.at[pl.ds(r, 1)], sem).start()
        return carry
    lax.fori_loop(0, n_rows, body, 0, unroll=8)


def _wait_row_gather(src_hbm, dst, sem, n_rows):
    pltpu.make_async_copy(src_hbm.at[pl.ds(0, n_rows)], dst, sem).wait()


def _expert_kernel(be_ref, idx_ref, idx_next_ref, h_hbm, win_ref, wout_ref, y_ref,
                   xbuf_a, xbuf_b, sem, win_bf, wout_bf, *, nb, cast_rows, k_chunk):
    i = pl.program_id(0)
    blk, half = xbuf_a.shape
    n_k = 2 * half // k_chunk
    rows_per_chunk = blk // n_k

    @pl.when(i == 0)
    def _():
        _issue_row_gather(idx_ref, h_hbm, xbuf_a, sem.at[0], blk)

    expert_changed = (i == 0) | (be_ref[i] != be_ref[jnp.maximum(i - 1, 0)])

    @pl.when(expert_changed)
    def _():
        def cast_in(c, carry):
            r0 = pl.multiple_of(c * cast_rows, cast_rows)
            win_bf[pl.ds(r0, cast_rows), :] = win_ref[0, 0, pl.ds(r0, cast_rows), :].astype(BF16)
            return carry
        lax.fori_loop(0, win_bf.shape[0] // cast_rows, cast_in, 0)

        def cast_out(c, carry):
            r0 = pl.multiple_of(c * cast_rows, cast_rows)
            wout_bf[pl.ds(r0, cast_rows), :] = wout_ref[0, 0, pl.ds(r0, cast_rows), :].astype(BF16)
            return carry
        lax.fori_loop(0, wout_bf.shape[0] // cast_rows, cast_out, 0)

    def run(x_cur, sem_cur, x_next, sem_next):
        _wait_row_gather(h_hbm, x_cur, sem_cur, blk)
        gu = None
        for kk in range(n_k):
            for r in range(kk * rows_per_chunk, (kk + 1) * rows_per_chunk):
                pltpu.make_async_copy(h_hbm.at[pl.ds(idx_next_ref[0, 0, r], 1)], x_next.at[pl.ds(r, 1)],
                                      sem_next).start()
            f0 = kk * k_chunk
            words = x_cur[:, f0 % half:f0 % half + k_chunk]
            xk = _unpack_lo(words) if f0 < half else _unpack_hi(words)
            part = jnp.dot(xk.astype(BF16), win_bf[f0:f0 + k_chunk, :], preferred_element_type=F32)
            gu = part if gu is None else gu + part
        f = gu.shape[1] // 2
        act = _silu(gu[:, :f]) * gu[:, f:]
        y_ref[...] = _pack_bf16_pairs(jnp.dot(act.astype(BF16), wout_bf[...], preferred_element_type=F32))

        @pl.when(i == nb - 1)
        def _():
            _wait_row_gather(h_hbm, x_next, sem_next, blk)

    @pl.when(i % 2 == 0)
    def _():
        run(xbuf_a, sem.at[0], xbuf_b, sem.at[1])

    @pl.when(i % 2 == 1)
    def _():
        run(xbuf_b, sem.at[1], xbuf_a, sem.at[0])


def moe_experts(h, slot_tok, block_expert, w_in, w_out, layer):
    half = h.shape[1]
    d = 2 * half
    f2 = w_in.shape[-1]
    f = f2 // 2
    nb = block_expert.shape[0]
    blk = MOE_BLOCK
    idx = slot_tok.reshape(nb, 1, blk)
    grid_spec = pltpu.PrefetchScalarGridSpec(
        num_scalar_prefetch=1,
        grid=(nb,),
        in_specs=[pl.BlockSpec((1, 1, blk), lambda i, be: (i, 0, 0), memory_space=pltpu.SMEM),
                  pl.BlockSpec((1, 1, blk), lambda i, be: (jnp.minimum(i + 1, nb - 1), 0, 0),
                               memory_space=pltpu.SMEM),
                  pl.BlockSpec(memory_space=pl.ANY),
                  pl.BlockSpec((1, 1, d, f2), lambda i, be: (layer, be[i], 0, 0)),
                  pl.BlockSpec((1, 1, f, d), lambda i, be: (layer, be[i], 0, 0))],
        out_specs=pl.BlockSpec((blk, half), lambda i, be: (i, 0)),
        scratch_shapes=[pltpu.VMEM((blk, half), jnp.uint32),
                        pltpu.VMEM((blk, half), jnp.uint32),
                        pltpu.SemaphoreType.DMA((2,)),
                        pltpu.VMEM((d, f2), BF16),
                        pltpu.VMEM((f, d), BF16)],
    )
    return pl.pallas_call(
        functools.partial(_expert_kernel, nb=nb, cast_rows=256, k_chunk=256),
        grid_spec=grid_spec,
        out_shape=jax.ShapeDtypeStruct((nb * blk, half), jnp.uint32),
        compiler_params=_cparams("arbitrary"),
        name="moe_experts",
    )(block_expert, idx, idx, h, w_in, w_out)


def _combine_kernel(pos_ref, pos_next_ref, yb_hbm, x_ref, route_ref, gate_ref, o_ref, ybuf, sem, *, nt):
    i = pl.program_id(0)
    slot = i % 2
    n_rows = ybuf.shape[1]
    tm = n_rows // MOE_TOP_K

    @pl.when(i == 0)
    def _():
        _issue_row_gather(pos_ref, yb_hbm, ybuf.at[0], sem.at[0], n_rows)

    @pl.when(i + 1 < nt)
    def _():
        _issue_row_gather(pos_next_ref, yb_hbm, ybuf.at[1 - slot], sem.at[1 - slot], n_rows)

    _wait_row_gather(yb_hbm, ybuf.at[slot], sem.at[slot], n_rows)
    route = route_ref[...]
    w1, w2 = route[:, 2:3], route[:, 3:4]
    y1, y2 = ybuf[slot, :tm, :], ybuf[slot, tm:, :]
    half = y1.shape[1]
    gate = gate_ref[0]
    o_ref[:, :half] = x_ref[:, :half] + gate[:, :half] * (w1 * _unpack_lo(y1) + w2 * _unpack_lo(y2))
    o_ref[:, half:] = x_ref[:, half:] + gate[:, half:] * (w1 * _unpack_hi(y1) + w2 * _unpack_hi(y2))


def moe_combine(yb, pos, xs, route, mods, layer, *, tm, seq, batch, out_rows):
    t, d = xs.shape
    pos_t = pos.reshape(t // tm, tm, MOE_TOP_K).transpose(0, 2, 1).reshape(t // tm, 1, MOE_TOP_K * tm)
    nt = out_rows // tm

    def seg(i):
        return jnp.minimum((i * tm) // seq, batch)

    return pl.pallas_call(
        functools.partial(_combine_kernel, nt=nt),
        grid=(nt,),
        in_specs=[pl.BlockSpec((1, 1, MOE_TOP_K * tm), lambda i: (i, 0, 0), memory_space=pltpu.SMEM),
                  pl.BlockSpec((1, 1, MOE_TOP_K * tm), lambda i: (jnp.minimum(i + 1, nt - 1), 0, 0),
                               memory_space=pltpu.SMEM),
                  pl.BlockSpec(memory_space=pl.ANY),
                  pl.BlockSpec((tm, d), lambda i: (i, 0)),
                  pl.BlockSpec((tm, LANES), lambda i: (i, 0)),
                  pl.BlockSpec((1, 1, d), lambda i: ((layer * MOD_ROWS + seg(i)) * 6 + 5, 0, 0))],
        out_specs=pl.BlockSpec((tm, d), lambda i: (i, 0)),
        out_shape=jax.ShapeDtypeStruct((out_rows, d), F32),
        scratch_shapes=[pltpu.VMEM((2, MOE_TOP_K * tm, d // 2), jnp.uint32), pltpu.SemaphoreType.DMA((2,))],
        compiler_params=_cparams("arbitrary"),
        name="moe_combine",
    )(pos_t, pos_t, yb, xs, route, mods)


def _dispatch_plan(route, counts, n_experts, blk):
    t = route.shape[0]
    nk = t * MOE_TOP_K
    padded = (counts + blk - 1) // blk * blk
    pad_end = jnp.cumsum(padded)
    pad_start = pad_end - padded
    nb = -(-(nk + n_experts * (blk - 1)) // blk)
    e = route[:, :MOE_TOP_K].astype(jnp.int32)
    rank = route[:, 2 * MOE_TOP_K:3 * MOE_TOP_K].astype(jnp.int32)
    onehot = e[..., None] == jnp.arange(n_experts, dtype=jnp.int32)
    dest = jnp.sum(jnp.where(onehot, pad_start, 0), axis=-1) + rank
    tok = jnp.arange(nk, dtype=jnp.int32) // MOE_TOP_K
    slot_tok = jnp.zeros((nb * blk,), jnp.int32).at[dest.reshape(nk)].set(tok, unique_indices=True)
    first_row = jnp.arange(nb, dtype=jnp.int32) * blk
    block_expert = jnp.minimum(jnp.sum(pad_end[None, :] <= first_row[:, None], axis=1), n_experts - 1)
    return slot_tok, block_expert.astype(jnp.int32), dest


def _axial_angles(seq, dim):
    n_freq = dim // 4
    freqs = ROPE_THETA ** (-jnp.arange(n_freq, dtype=F32) / n_freq)
    tok = jnp.arange(seq, dtype=jnp.int32)
    row = (tok // GRID_W).astype(F32)
    col = (tok % GRID_W).astype(F32)
    return jnp.concatenate([row[:, None] * freqs, col[:, None] * freqs], axis=-1)


def _stream_table(lat, ctx_fill, batch, n_ctx_rows):
    ctx_rows = jnp.broadcast_to(ctx_fill[None, :], (n_ctx_rows, lat.shape[1]))
    return jnp.concatenate([jnp.tile(lat, (batch, 1)), ctx_rows], axis=0)


def _rope_tables_128(seq, batch, n_ctx_rows):
    ang = _axial_angles(seq, HEAD_DIM)
    cos, sin = jnp.cos(ang), jnp.sin(ang)
    c = jnp.concatenate([cos, cos], axis=-1)
    s = jnp.concatenate([-sin, sin], axis=-1)
    return (_stream_table(c, jnp.ones((HEAD_DIM,), F32), batch, n_ctx_rows),
            _stream_table(s, jnp.zeros((HEAD_DIM,), F32), batch, n_ctx_rows))


def _rope_tables_64(seq, batch, n_ctx_rows):
    ang = _axial_angles(seq, MLA_ROPE)
    cos, sin = jnp.cos(ang), jnp.sin(ang)
    half = MLA_ROPE // 2
    z = jnp.zeros((seq, LANES - MLA_ROPE), F32)
    zh = jnp.zeros((seq, half), F32)
    c = jnp.concatenate([cos, cos, z], axis=-1)
    s1 = jnp.concatenate([-sin, zh, z], axis=-1)
    s2 = jnp.concatenate([zh, sin, z], axis=-1)
    ones_pad = jnp.concatenate([jnp.ones((MLA_ROPE,), F32), jnp.zeros((LANES - MLA_ROPE,), F32)])
    zeros = jnp.zeros((LANES,), F32)
    return (_stream_table(c, ones_pad, batch, n_ctx_rows),
            _stream_table(s1, zeros, batch, n_ctx_rows),
            _stream_table(s2, zeros, batch, n_ctx_rows))


def _na_bias_table(rpb, kr):
    col = jnp.arange(GRID_W, dtype=jnp.int32)
    col_start = jnp.clip(col - NA_COLS // 2, 0, GRID_W - NA_COLS)
    col_in = (col[None, :] >= col_start[:, None]) & (col[None, :] < col_start[:, None] + NA_COLS)
    dc = jnp.clip(col[None, :] - col[:, None] + NA_COLS - 1, 0, 2 * NA_COLS - 2)
    h = rpb.shape[0]
    by_col = jnp.zeros((h, 2 * NA_ROWS - 1, GRID_W, GRID_W), F32)
    for cc in range(2 * NA_COLS - 1):
        by_col = jnp.where(dc[None, None] == cc, rpb[:, :, cc, None, None].astype(F32), by_col)
    by_col = jnp.where(col_in[None, None], by_col, NEG_INF)
    per_shift = [by_col[:, NA_ROWS - 1 - s:NA_ROWS - 1 - s + kr] for s in range(kr)]
    bias = jnp.stack(per_shift, axis=1)
    return bias.transpose(0, 1, 3, 2, 4).reshape(h, kr, GRID_W, kr * GRID_W)


def _pick_tile(*extents):
    for tm in (512, 256, 128):
        if all(e % tm == 0 for e in extents):
            return tm
    raise ValueError("row extents must be multiples of 128")


def kernel(x, c, ctx, c_ctx, mod_w, mod_b, norm_g, gqa_w_qkv, gqa_q_norm, gqa_k_norm, gqa_w_o, na_w_qkv, na_q_norm, na_k_norm, na_rpb, na_w_o, mla_w_down, mla_q_lora_norm, mla_w_uq, mla_kv_lora_norm, mla_w_ukv, mla_q_norm, mla_k_norm, mla_w_o, moe_w_group, moe_b_group, moe_w_expert, moe_b_expert, moe_w_in, moe_w_out):
    batch, seq, d = x.shape
    n_ctx = ctx.shape[1]
    depth = mod_w.shape[0]
    heads = d // HEAD_DIM
    kv_heads = heads // 4
    n_lat = batch * seq
    n_ctx_rows = batch * n_ctx
    assert batch + 1 <= MOD_ROWS and n_lat % n_ctx == 0 and seq % GRID_W == 0
    tm = _pick_tile(seq, n_ctx_rows)
    tq = min(FLASH_TQ, seq)
    tk = min(FLASH_TK, seq)
    assert seq % tq == 0 and seq % (2 * tk) == 0
    tn_d = min(512, d)
    dims = dict(tm=tm, seq=seq, batch=batch)

    xs = jnp.concatenate([x.reshape(n_lat, d), ctx.reshape(n_ctx_rows, d)], axis=0)

    cond = jnp.zeros((MOD_ROWS, d), F32).at[:batch].set(c).at[batch].set(c_ctx)
    mods = adaln_all(cond, mod_w, mod_b).reshape(depth * MOD_ROWS * 6, 1, d)

    cos_a, sin_a = _rope_tables_128(seq, batch, n_ctx_rows)
    c64, s64a, s64b = _rope_tables_64(seq, batch, n_ctx_rows)

    n_groups = moe_w_group.shape[-1]
    epg = moe_w_expert.shape[-1]
    n_experts = n_groups * epg
    assert n_groups + n_experts <= LANES

    for i in range(depth):
        kind, j = i % 3, i // 3
        if kind == 0:
            scale = HEAD_DIM ** -0.5 * LOG2E
            nq, nkv = heads * HEAD_DIM, kv_heads * HEAD_DIM
            gain = jnp.concatenate([jnp.tile(gqa_q_norm[j] * scale, heads), jnp.tile(gqa_k_norm[j], kv_heads),
                                    jnp.ones((nkv,), F32)]).reshape(1, -1)
            qkv = qkv_project(xs, mods, i, norm_g[i, 0], gqa_w_qkv[j].astype(BF16), gain, cos_a, sin_a,
                              n_norm_cols=nq + nkv, tn=min(512, nkv), rope=True, **dims)
            cols = dict(dk=HEAD_DIM, q_col0=0, k_col0=heads, v_col0=heads + kv_heads, kv_group=heads // kv_heads)
            o_ctx = ctx_attention(qkv, qkv, qkv, batch=batch, seq=seq, ctx=n_ctx, heads=heads, **cols)
            o = dense_attention(qkv, qkv, qkv, o_ctx, batch=batch, seq=seq, ctx=n_ctx, heads=heads, tq=tq, tk=tk,
                                **cols)
            w_o = gqa_w_o[j]
        elif kind == 1:
            scale = HEAD_DIM ** -0.5 * LOG2E
            nq = heads * HEAD_DIM
            gain = jnp.concatenate([jnp.tile(na_q_norm[j] * scale, heads), jnp.tile(na_k_norm[j], heads),
                                    jnp.ones((nq,), F32)]).reshape(1, -1)
            qkv = qkv_project(xs, mods, i, norm_g[i, 0], na_w_qkv[j].astype(BF16), gain, cos_a, sin_a,
                              n_norm_cols=2 * nq, tn=tn_d, rope=False, **dims)
            rows = seq // GRID_W
            tb = _na_bias_table(na_rpb[j], min(NA_ROWS, rows)) * LOG2E
            o_ctx = ctx_attention(qkv, qkv, qkv, batch=batch, seq=seq, ctx=n_ctx, heads=heads, dk=HEAD_DIM,
                                  q_col0=0, k_col0=heads, v_col0=2 * heads, kv_group=1)
            o = na_attention(qkv, tb, o_ctx, batch=batch, seq=seq, ctx=n_ctx, heads=heads)
            w_o = na_w_o[j]
        else:
            scale = MLA_QK ** -0.5 * LOG2E
            q_lora = mla_q_lora_norm.shape[-1]
            kv_lora = mla_kv_lora_norm.shape[-1]
            assert q_lora % kv_lora == 0 and (q_lora + kv_lora) % LANES == 0
            n_down = q_lora + kv_lora + LANES
            w_down = jnp.pad(mla_w_down[j], ((0, 0), (0, n_down - mla_w_down.shape[-1]))).astype(BF16)
            cfull = mla_down(xs, mods, i, norm_g[i, 0], w_down, **dims)
            pad = MLA_HEAD_PAD - MLA_QK
            w_uq = jnp.pad(mla_w_uq[j].reshape(q_lora, heads, MLA_QK), ((0, 0), (0, 0), (0, pad)))
            w_uq = w_uq.reshape(q_lora, heads * MLA_HEAD_PAD).astype(BF16)
            q_gain = jnp.tile(jnp.pad(mla_q_norm[j] * scale, (0, pad)), heads).reshape(1, -1)
            qa = mla_q_project(cfull, mla_q_lora_norm[j], w_uq, q_gain, c64, s64a, s64b, tm=tm,
                               tn=min(512, heads * MLA_HEAD_PAD))
            gain_n = mla_k_norm[j, :MLA_NOPE].reshape(1, LANES)
            gain_t = jnp.pad(mla_k_norm[j, MLA_NOPE:], (0, LANES - MLA_ROPE)).reshape(1, LANES)
            ka, va = mla_kv_project(cfull, mla_kv_lora_norm[j], mla_w_ukv[j].astype(BF16), gain_n, gain_t,
                                    c64, s64a, s64b, tm=tm, q_lora=q_lora, heads_per_step=2)
            cols = dict(dk=MLA_HEAD_PAD, q_col0=0, k_col0=0, v_col0=0, kv_group=1)
            o_ctx = ctx_attention(qa, ka, va, batch=batch, seq=seq, ctx=n_ctx, heads=heads, **cols)
            o = dense_attention(qa, ka, va, o_ctx, batch=batch, seq=seq, ctx=n_ctx, heads=heads, tq=tq, tk=tk, **cols)
            w_o = mla_w_o[j]

        w_r = jnp.concatenate([moe_w_group[i], moe_w_expert[i].transpose(1, 0, 2).reshape(d, n_experts)], axis=1)
        w_r = jnp.pad(w_r, ((0, 0), (0, LANES - w_r.shape[1])))
        b_r = jnp.pad(jnp.concatenate([moe_b_group[i], moe_b_expert[i].reshape(-1)]),
                      (0, LANES - n_groups - n_experts)).reshape(1, LANES)
        xs, h2, route, lane_counts = proj_residual_router(o, w_o.astype(BF16), xs, mods, i, norm_g[i, 1], w_r, b_r,
                                                          n_groups=n_groups, epg=epg, **dims)
        counts = lane_counts[0, n_groups:n_groups + n_experts].astype(jnp.int32)
        slot_tok, block_expert, pos = _dispatch_plan(route, counts, n_experts, MOE_BLOCK)
        yb = moe_experts(h2, slot_tok, block_expert, moe_w_in, moe_w_out, i)
        out_rows = n_lat if i == depth - 1 else n_lat + n_ctx_rows
        xs = moe_combine(yb, pos, xs, route, mods, i, tm=min(COMBINE_TM, tm), seq=seq, batch=batch,
                         out_rows=out_rows)

    return xs.reshape(batch, seq, d)
```

```python
import functools

import jax
import jax.numpy as jnp
from jax import lax
from jax.experimental import pallas as pl
from jax.experimental.pallas import tpu as pltpu

F32 = jnp.float32
BF16 = jnp.bfloat16

GRID_W = 64
HEAD_DIM = 128
ROPE_THETA = 10000.0
NORM_EPS = 1e-6
NEG_INF = -1e30
NA_ROWS = 8
NA_COLS = 16
MLA_NOPE = 128
MLA_ROPE = 64
MLA_V = 128
MLA_QK = MLA_NOPE + MLA_ROPE
MLA_HEAD_PAD = 256
MOE_TOP_K = 2
LANES = 128
MOD_ROWS = 8
MOE_BLOCK = 512
COMBINE_TM = 256
FLASH_TQ = 1024
FLASH_TK = 512
FLASH_ROW_GROUP = 32
LOG2E = 1.4426950408889634
VMEM_LIMIT = 52 * 1024 * 1024


def _cparams(*sem):
    return pltpu.CompilerParams(dimension_semantics=sem, vmem_limit_bytes=VMEM_LIMIT)


def _silu(v):
    return v / (1.0 + jnp.exp(-v))


def _adaln_kernel(c_ref, w_ref, b_ref, o_ref):
    cond = _silu(c_ref[...]).astype(BF16)
    o_ref[0] = jnp.dot(cond, w_ref[0].astype(BF16), preferred_element_type=F32) + b_ref[0]


def adaln_all(cond, mod_w, mod_b):
    depth, d, n = mod_w.shape
    tn = 1024
    return pl.pallas_call(
        _adaln_kernel,
        grid=(depth, n // tn),
        in_specs=[pl.BlockSpec((MOD_ROWS, d), lambda l, j: (0, 0)),
                  pl.BlockSpec((1, d, tn), lambda l, j: (l, 0, j)),
                  pl.BlockSpec((1, 1, tn), lambda l, j: (l, 0, j))],
        out_specs=pl.BlockSpec((1, MOD_ROWS, tn), lambda l, j: (l, 0, j)),
        out_shape=jax.ShapeDtypeStruct((depth, MOD_ROWS, n), F32),
        compiler_params=_cparams("arbitrary", "arbitrary"),
        name="adaln",
    )(cond, mod_w, mod_b.reshape(depth, 1, n))


def _rms(x, g):
    ms = jnp.mean(x * x, axis=-1, keepdims=True)
    return x * lax.rsqrt(ms + NORM_EPS) * g


def _rope128(y, c, s):
    return y * c + pltpu.roll(y, 64, 1) * s


def _rope64(y, c, s1, s2):
    return y * c + pltpu.roll(y, 96, 1) * s1 + pltpu.roll(y, 32, 1) * s2


def _qkv_kernel(x_ref, g_ref, sh_ref, sc_ref, w_ref, gain_ref, c_ref, s_ref, o_ref, *, tn, n_norm_cols, rope):
    h = (_rms(x_ref[...], g_ref[...]) * (1.0 + sc_ref[0]) + sh_ref[0]).astype(BF16)
    for c0 in range(0, w_ref.shape[1], tn):
        y = jnp.dot(h, w_ref[:, c0:c0 + tn], preferred_element_type=F32)
        if c0 >= n_norm_cols:
            o_ref[:, c0:c0 + tn] = y.astype(o_ref.dtype)
            continue
        for hh in range(tn // HEAD_DIM):
            sl = slice(c0 + hh * HEAD_DIM, c0 + (hh + 1) * HEAD_DIM)
            yh = _rms(y[:, hh * HEAD_DIM:(hh + 1) * HEAD_DIM], gain_ref[:, sl])
            if rope:
                yh = _rope128(yh, c_ref[...], s_ref[...])
            o_ref[:, sl] = yh.astype(o_ref.dtype)


def _mod_specs(layer, which, tm, seq, batch, d):
    def seg(i):
        return jnp.minimum((i * tm) // seq, batch)
    return [pl.BlockSpec((1, 1, d), lambda i, j, w=w: ((layer * MOD_ROWS + seg(i)) * 6 + w, 0, 0)) for w in which]


def qkv_project(xs, mods, layer, norm_g, w_bf, gain, cos_t, sin_t, *, tm, seq, batch, n_norm_cols, tn, rope):
    t, d = xs.shape
    n = w_bf.shape[1]
    sh_spec, sc_spec = _mod_specs(layer, (0, 1), tm, seq, batch, d)
    assert n_norm_cols % tn == 0 and n % tn == 0
    kern = functools.partial(_qkv_kernel, tn=tn, n_norm_cols=n_norm_cols, rope=rope)
    return pl.pallas_call(
        kern,
        grid=(t // tm, 1),
        in_specs=[pl.BlockSpec((tm, d), lambda i, j: (i, 0)),
                  pl.BlockSpec((1, d), lambda i, j: (0, 0)),
                  sh_spec, sc_spec,
                  pl.BlockSpec((d, n), lambda i, j: (0, 0), pipeline_mode=pl.Buffered(1)),
                  pl.BlockSpec((1, n), lambda i, j: (0, 0)),
                  pl.BlockSpec((tm, HEAD_DIM), lambda i, j: (i, 0)),
                  pl.BlockSpec((tm, HEAD_DIM), lambda i, j: (i, 0))],
        out_specs=pl.BlockSpec((tm, n), lambda i, j: (i, 0)),
        out_shape=jax.ShapeDtypeStruct((t, n), BF16),
        compiler_params=_cparams("arbitrary", "arbitrary"),
        name="qkv_project",
    )(xs, norm_g.reshape(1, d), mods, mods, w_bf, gain, cos_t, sin_t)


def _down_kernel(x_ref, g_ref, sh_ref, sc_ref, w_ref, o_ref):
    h = _rms(x_ref[...], g_ref[...]) * (1.0 + sc_ref[0]) + sh_ref[0]
    o_ref[...] = jnp.dot(h.astype(BF16), w_ref[...], preferred_element_type=F32)


def mla_down(xs, mods, layer, norm_g, w_bf, *, tm, seq, batch):
    t, d = xs.shape
    n = w_bf.shape[1]
    sh_spec, sc_spec = _mod_specs(layer, (0, 1), tm, seq, batch, d)
    return pl.pallas_call(
        _down_kernel,
        grid=(t // tm, 1),
        in_specs=[pl.BlockSpec((tm, d), lambda i, j: (i, 0)),
                  pl.BlockSpec((1, d), lambda i, j: (0, 0)),
                  sh_spec, sc_spec,
                  pl.BlockSpec((d, n), lambda i, j: (0, 0))],
        out_specs=pl.BlockSpec((tm, n), lambda i, j: (i, 0)),
        out_shape=jax.ShapeDtypeStruct((t, n), F32),
        compiler_params=_cparams("arbitrary", "arbitrary"),
        name="mla_down",
    )(xs, norm_g.reshape(1, d), mods, mods, w_bf)


def _mla_q_kernel(cq_ref, g_ref, w_ref, gain_ref, c_ref, s1_ref, s2_ref, o_ref):
    h = _rms(cq_ref[...], g_ref[...]).astype(BF16)
    for hh in range(w_ref.shape[1] // MLA_HEAD_PAD):
        lo = hh * MLA_HEAD_PAD
        yh = jnp.dot(h, w_ref[:, lo:lo + MLA_HEAD_PAD], preferred_element_type=F32)
        ms = jnp.sum(yh * yh, axis=-1, keepdims=True) * (1.0 / MLA_QK)
        yh = yh * lax.rsqrt(ms + NORM_EPS) * gain_ref[:, lo:lo + MLA_HEAD_PAD]
        o_ref[:, lo:lo + MLA_NOPE] = yh[:, :MLA_NOPE].astype(o_ref.dtype)
        tail = _rope64(yh[:, MLA_NOPE:], c_ref[...], s1_ref[...], s2_ref[...])
        o_ref[:, lo + MLA_NOPE:lo + MLA_HEAD_PAD] = tail.astype(o_ref.dtype)


def mla_q_project(cfull, q_lora_g, w_bf, gain, c_t, s1_t, s2_t, *, tm):
    t = cfull.shape[0]
    kq, n = w_bf.shape
    return pl.pallas_call(
        _mla_q_kernel,
        grid=(t // tm, 1),
        in_specs=[pl.BlockSpec((tm, kq), lambda i, j: (i, 0)),
                  pl.BlockSpec((1, kq), lambda i, j: (0, 0)),
                  pl.BlockSpec((kq, n), lambda i, j: (0, 0), pipeline_mode=pl.Buffered(1)),
                  pl.BlockSpec((1, n), lambda i, j: (0, 0)),
                  pl.BlockSpec((tm, LANES), lambda i, j: (i, 0)),
                  pl.BlockSpec((tm, LANES), lambda i, j: (i, 0)),
                  pl.BlockSpec((tm, LANES), lambda i, j: (i, 0))],
        out_specs=pl.BlockSpec((tm, n), lambda i, j: (i, 0)),
        out_shape=jax.ShapeDtypeStruct((t, n), BF16),
        compiler_params=_cparams("arbitrary", "arbitrary"),
        name="mla_q_project",
    )(cfull, q_lora_g.reshape(1, kq), w_bf, gain, c_t, s1_t, s2_t)


def _mla_kv_kernel(ckv_ref, g_ref, kr_ref, w_ref, gn_ref, gt_ref, c_ref, s1_ref, s2_ref, k_ref, v_ref):
    h = _rms(ckv_ref[...], g_ref[...]).astype(BF16)
    kr = kr_ref[...]
    ss_rope = jnp.sum(kr * kr, axis=-1, keepdims=True)
    rot = _rope64(kr * gt_ref[...], c_ref[...], s1_ref[...], s2_ref[...])
    per_head = MLA_NOPE + MLA_V
    for hh in range(w_ref.shape[1] // per_head):
        y = jnp.dot(h, w_ref[:, hh * per_head:(hh + 1) * per_head], preferred_element_type=F32)
        kn = y[:, :MLA_NOPE]
        ms = (jnp.sum(kn * kn, axis=-1, keepdims=True) + ss_rope) * (1.0 / MLA_QK)
        rs = lax.rsqrt(ms + NORM_EPS)
        lo = hh * MLA_HEAD_PAD
        k_ref[:, lo:lo + MLA_NOPE] = (kn * rs * gn_ref[...]).astype(k_ref.dtype)
        k_ref[:, lo + MLA_NOPE:lo + MLA_HEAD_PAD] = (rot * rs).astype(k_ref.dtype)
        v_ref[:, hh * MLA_V:(hh + 1) * MLA_V] = y[:, MLA_NOPE:].astype(v_ref.dtype)


def mla_kv_project(cfull, kv_lora_g, w_bf, gain_n, gain_t, c_t, s1_t, s2_t, *, tm, q_lora):
    t = cfull.shape[0]
    kkv, n = w_bf.shape
    heads = n // (MLA_NOPE + MLA_V)
    return pl.pallas_call(
        _mla_kv_kernel,
        grid=(t // tm, 1),
        in_specs=[pl.BlockSpec((tm, kkv), lambda i, j: (i, q_lora // kkv)),
                  pl.BlockSpec((1, kkv), lambda i, j: (0, 0)),
                  pl.BlockSpec((tm, LANES), lambda i, j: (i, (q_lora + kkv) // LANES)),
                  pl.BlockSpec((kkv, n), lambda i, j: (0, 0), pipeline_mode=pl.Buffered(1)),
                  pl.BlockSpec((1, LANES), lambda i, j: (0, 0)),
                  pl.BlockSpec((1, LANES), lambda i, j: (0, 0)),
                  pl.BlockSpec((tm, LANES), lambda i, j: (i, 0)),
                  pl.BlockSpec((tm, LANES), lambda i, j: (i, 0)),
                  pl.BlockSpec((tm, LANES), lambda i, j: (i, 0))],
        out_specs=[pl.BlockSpec((tm, heads * MLA_HEAD_PAD), lambda i, j: (i, 0)),
                   pl.BlockSpec((tm, heads * MLA_V), lambda i, j: (i, 0))],
        out_shape=[jax.ShapeDtypeStruct((t, heads * MLA_HEAD_PAD), BF16),
                   jax.ShapeDtypeStruct((t, heads * MLA_V), BF16)],
        compiler_params=_cparams("arbitrary", "arbitrary"),
        name="mla_kv_project",
    )(cfull, kv_lora_g.reshape(1, kkv), cfull, w_bf, gain_n, gain_t, c_t, s1_t, s2_t)


_NT = (((1,), (1,)), ((), ()))


def _flash_scores(q, k, s_buf):
    s_buf[:, :k.shape[0]] = lax.dot_general(q, k, _NT, preferred_element_type=F32)


def _flash_update(s_scr, v, p_scr, m_scr, l_scr, acc_scr, rg):
    tq = s_scr.shape[0]
    w = v.shape[0]
    n_tiles = w // LANES

    for g in range(tq // rg):
        rows_ = slice(g * rg, (g + 1) * rg)
        tiles = [s_scr[rows_, t * LANES:(t + 1) * LANES] for t in range(n_tiles)]
        mx = tiles[0]
        for st in tiles[1:]:
            mx = jnp.maximum(mx, st)
        m_prev = m_scr[rows_, :]
        m_next = jnp.maximum(m_prev, jnp.max(mx, axis=-1, keepdims=True))
        alpha = jnp.exp2(m_prev - m_next)
        psum = None
        for t, st in enumerate(tiles):
            p = jnp.exp2(st - m_next)
            p_scr[rows_, t * LANES:(t + 1) * LANES] = p.astype(p_scr.dtype)
            psum = p if psum is None else psum + p
        l_scr[rows_, :] = alpha * l_scr[rows_, :] + psum
        acc_scr[rows_, :] = alpha * acc_scr[rows_, :]
        m_scr[rows_, :] = m_next
    acc_scr[...] += jnp.dot(p_scr[:, :w], v, preferred_element_type=F32)


def _flash_kernel(*refs, n_lat_chunks, tk, rg):
    if n_lat_chunks:
        q_ref, kc_ref, vc_ref, k_ref, v_ref, _, o_ref, s0, s1, p0, p1, m_scr, l_scr, acc_scr = refs
    else:
        q_ref, kc_ref, vc_ref, _, o_ref, s0, s1, p0, p1, m_scr, l_scr, acc_scr = refs
    state = (m_scr, l_scr, acc_scr, rg)
    m_scr[...] = jnp.full(m_scr.shape, -jnp.inf, F32)
    l_scr[...] = jnp.zeros(l_scr.shape, F32)
    acc_scr[...] = jnp.zeros(acc_scr.shape, F32)
    q = q_ref[...]
    _flash_scores(q, kc_ref[...], s0)
    if n_lat_chunks:
        assert n_lat_chunks % 2 == 0

        def k_chunk(c):
            return k_ref[pl.ds(pl.multiple_of(c * tk, tk), tk), :]

        def v_chunk(c):
            return v_ref[pl.ds(pl.multiple_of(c * tk, tk), tk), :]

        _flash_scores(q, k_chunk(0), s1)
        _flash_update(s0, vc_ref[...], p0, *state)

        def body(j, carry):
            _flash_scores(q, k_chunk(2 * j + 1), s0)
            _flash_update(s1, v_chunk(2 * j), p1, *state)
            _flash_scores(q, k_chunk(2 * j + 2), s1)
            _flash_update(s0, v_chunk(2 * j + 1), p0, *state)
            return carry
        lax.fori_loop(0, n_lat_chunks // 2 - 1, body, 0)
        _flash_scores(q, k_chunk(n_lat_chunks - 1), s0)
        _flash_update(s1, v_chunk(n_lat_chunks - 2), p1, *state)
        _flash_update(s0, v_chunk(n_lat_chunks - 1), p0, *state)
    else:
        _flash_update(s0, vc_ref[...], p0, *state)
    l = jnp.sum(l_scr[...], axis=-1, keepdims=True)
    o_ref[...] = (acc_scr[...] / l).astype(o_ref.dtype)


def _flash_scratch(tq, w):
    return [pltpu.VMEM((tq, w), F32), pltpu.VMEM((tq, w), F32),
            pltpu.VMEM((tq, w), BF16), pltpu.VMEM((tq, w), BF16),
            pltpu.VMEM((tq, LANES), F32), pltpu.VMEM((tq, LANES), F32), pltpu.VMEM((tq, LANES), F32)]


def ctx_attention(qa, ka, va, *, batch, seq, ctx, heads, dk, q_col0, k_col0, v_col0, kv_group):
    t = qa.shape[0]
    ctx_blk0 = (batch * seq) // ctx
    o_shape = jax.ShapeDtypeStruct((t, heads * LANES), BF16)
    return pl.pallas_call(
        functools.partial(_flash_kernel, n_lat_chunks=0, tk=0, rg=FLASH_ROW_GROUP),
        grid=(batch, heads),
        in_specs=[pl.BlockSpec((ctx, dk), lambda b, h: (ctx_blk0 + b, q_col0 + h)),
                  pl.BlockSpec((ctx, dk), lambda b, h: (ctx_blk0 + b, k_col0 + h // kv_group)),
                  pl.BlockSpec((ctx, LANES), lambda b, h: (ctx_blk0 + b, v_col0 + h // kv_group)),
                  pl.BlockSpec(memory_space=pl.ANY)],
        out_specs=pl.BlockSpec((ctx, LANES), lambda b, h: (ctx_blk0 + b, h)),
        out_shape=o_shape,
        input_output_aliases={3: 0},
        scratch_shapes=_flash_scratch(ctx, ctx),
        compiler_params=_cparams("arbitrary", "arbitrary"),
        name="ctx_attention",
    )(qa, ka, va, jnp.zeros(o_shape.shape, o_shape.dtype))


def dense_attention(qa, ka, va, o_ctx, *, batch, seq, ctx, heads, dk, q_col0, k_col0, v_col0, kv_group, tq, tk):
    nq = seq // tq
    ctx_blk0 = (batch * seq) // ctx
    return pl.pallas_call(
        functools.partial(_flash_kernel, n_lat_chunks=seq // tk, tk=tk, rg=FLASH_ROW_GROUP),
        grid=(batch, heads, nq),
        in_specs=[pl.BlockSpec((tq, dk), lambda b, h, i: (b * nq + i, q_col0 + h)),
                  pl.BlockSpec((ctx, dk), lambda b, h, i: (ctx_blk0 + b, k_col0 + h // kv_group)),
                  pl.BlockSpec((ctx, LANES), lambda b, h, i: (ctx_blk0 + b, v_col0 + h // kv_group)),
                  pl.BlockSpec((seq, dk), lambda b, h, i: (b, k_col0 + h // kv_group)),
                  pl.BlockSpec((seq, LANES), lambda b, h, i: (b, v_col0 + h // kv_group)),
                  pl.BlockSpec(memory_space=pl.ANY)],
        out_specs=pl.BlockSpec((tq, LANES), lambda b, h, i: (b * nq + i, h)),
        out_shape=jax.ShapeDtypeStruct(o_ctx.shape, o_ctx.dtype),
        input_output_aliases={5: 0},
        scratch_shapes=_flash_scratch(tq, max(tk, ctx)),
        compiler_params=_cparams("arbitrary", "arbitrary", "arbitrary"),
        name="dense_attention",
    )(qa, ka, va, ka, va, o_ctx)


def _na_kernel(q_ref, k_ref, v_ref, kc_ref, vc_ref, tb_ref, _, o_ref, *, rows, kr):
    kc = kc_ref[...]
    vc = vc_ref[...]
    nt = (((1,), (1,)), ((), ()))

    def body(r, carry):
        rs = jnp.clip(r - kr // 2, 0, rows - kr)
        q0 = pl.multiple_of(r * GRID_W, GRID_W)
        k0 = pl.multiple_of(rs * GRID_W, GRID_W)
        q = q_ref[pl.ds(q0, GRID_W), :]
        kw = k_ref[pl.ds(k0, kr * GRID_W), :]
        vw = v_ref[pl.ds(k0, kr * GRID_W), :]
        s_win = lax.dot_general(q, kw, nt, preferred_element_type=F32) + tb_ref[0, r - rs]
        s_ctx = lax.dot_general(q, kc, nt, preferred_element_type=F32)
        m = jnp.maximum(jnp.max(s_win, axis=-1, keepdims=True), jnp.max(s_ctx, axis=-1, keepdims=True))
        p_win = jnp.exp2(s_win - m)
        p_ctx = jnp.exp2(s_ctx - m)
        l = jnp.sum(p_win, axis=-1, keepdims=True) + jnp.sum(p_ctx, axis=-1, keepdims=True)
        o = (jnp.dot(p_win.astype(vw.dtype), vw, preferred_element_type=F32)
             + jnp.dot(p_ctx.astype(vc.dtype), vc, preferred_element_type=F32))
        o_ref[pl.ds(q0, GRID_W), :] = (o / l).astype(o_ref.dtype)
        return carry

    lax.fori_loop(0, rows, body, 0, unroll=4)


def na_attention(qkv, tb, o_ctx, *, batch, seq, ctx, heads):
    t = qkv.shape[0]
    rows = seq // GRID_W
    kr = min(NA_ROWS, rows)
    ctx_blk0 = (batch * seq) // ctx
    d = HEAD_DIM
    return pl.pallas_call(
        functools.partial(_na_kernel, rows=rows, kr=kr),
        grid=(batch, heads),
        in_specs=[pl.BlockSpec((seq, d), lambda b, h: (b, h)),
                  pl.BlockSpec((seq, d), lambda b, h: (b, heads + h)),
                  pl.BlockSpec((seq, d), lambda b, h: (b, 2 * heads + h)),
                  pl.BlockSpec((ctx, d), lambda b, h: (ctx_blk0 + b, heads + h)),
                  pl.BlockSpec((ctx, d), lambda b, h: (ctx_blk0 + b, 2 * heads + h)),
                  pl.BlockSpec((1, kr, GRID_W, kr * GRID_W), lambda b, h: (h, 0, 0, 0)),
                  pl.BlockSpec(memory_space=pl.ANY)],
        out_specs=pl.BlockSpec((seq, d), lambda b, h: (b, h)),
        out_shape=jax.ShapeDtypeStruct((t, heads * d), BF16),
        input_output_aliases={6: 0},
        compiler_params=_cparams("arbitrary", "arbitrary"),
        name="na_attention",
    )(qkv, qkv, qkv, qkv, qkv, tb, o_ctx)


def _pack_bf16_pairs(x):
    n = x.shape[1] // 2
    lo = lax.bitcast_convert_type(x[:, :n].astype(BF16).astype(F32), jnp.uint32)
    hi = lax.bitcast_convert_type(x[:, n:].astype(BF16).astype(F32), jnp.uint32)
    return (lo >> 16) | (hi & jnp.uint32(0xFFFF0000))


def _unpack_lo(words):
    return lax.bitcast_convert_type(words << 16, F32)


def _unpack_hi(words):
    return lax.bitcast_convert_type(words & jnp.uint32(0xFFFF0000), F32)


def _proj_router_kernel(o_ref, wo_ref, x_ref, gate_ref, g_ref, sh_ref, sc_ref, w_ref, b_ref,
                        xo_ref, h_ref, route_ref, counts_ref, carry_scr, *, n_groups, epg):
    @pl.when(pl.program_id(0) == 0)
    def _():
        carry_scr[...] = jnp.zeros(carry_scr.shape, F32)

    x_new = x_ref[...] + gate_ref[0] * jnp.dot(o_ref[...], wo_ref[...], preferred_element_type=F32)
    xo_ref[...] = x_new
    h = _rms(x_new, g_ref[...]) * (1.0 + sc_ref[0]) + sh_ref[0]
    h_ref[...] = _pack_bf16_pairs(h)
    w = w_ref[...]
    h_hi = h.astype(BF16)
    h_lo = (h - h_hi.astype(F32)).astype(BF16)
    w_hi = w.astype(BF16)
    w_lo = (w - w_hi.astype(F32)).astype(BF16)
    logits = (jnp.dot(h_hi, w_hi, preferred_element_type=F32)
              + jnp.dot(h_lo, w_hi, preferred_element_type=F32)
              + jnp.dot(h_hi, w_lo, preferred_element_type=F32)) + b_ref[...]
    lane = lax.broadcasted_iota(jnp.int32, logits.shape, 1).astype(F32)
    is_g = lane < n_groups
    g_max = jnp.max(jnp.where(is_g, logits, -jnp.inf), axis=-1, keepdims=True)
    g_sum = jnp.sum(jnp.where(is_g, jnp.exp(logits - g_max), 0.0), axis=-1, keepdims=True)
    g_sel = jnp.min(jnp.where(is_g & (logits == g_max), lane, float(LANES)), axis=-1, keepdims=True)
    lo = n_groups + epg * g_sel
    in_grp = (lane >= lo) & (lane < lo + epg)
    t1 = jnp.max(jnp.where(in_grp, logits, -jnp.inf), axis=-1, keepdims=True)
    i1 = jnp.min(jnp.where(in_grp & (logits == t1), lane, float(LANES)), axis=-1, keepdims=True)
    rest = in_grp & (lane != i1)
    t2 = jnp.max(jnp.where(rest, logits, -jnp.inf), axis=-1, keepdims=True)
    i2 = jnp.min(jnp.where(rest & (logits == t2), lane, float(LANES)), axis=-1, keepdims=True)
    d = jnp.exp(t2 - t1)
    gate = 1.0 / g_sum
    w1 = gate * (1.0 / (1.0 + d))
    w2 = gate * (d / (1.0 + d))
    tm = logits.shape[0]
    chosen = jnp.where((lane == i1) | (lane == i2), 1.0, 0.0)
    earlier = jnp.where(lax.broadcasted_iota(jnp.int32, (tm, tm), 1) < lax.broadcasted_iota(jnp.int32, (tm, tm), 0),
                        1.0, 0.0).astype(BF16)
    before = jnp.dot(earlier, chosen.astype(BF16), preferred_element_type=F32) + carry_scr[...]
    rank1 = jnp.sum(jnp.where(lane == i1, before, 0.0), axis=-1, keepdims=True)
    rank2 = jnp.sum(jnp.where(lane == i2, before, 0.0), axis=-1, keepdims=True)
    total = carry_scr[...] + jnp.sum(chosen, axis=0, keepdims=True)
    carry_scr[...] = total
    counts_ref[...] = jnp.broadcast_to(total, counts_ref.shape)
    vals = (i1 - n_groups, i2 - n_groups, w1, w2, rank1, rank2)
    route = jnp.zeros(logits.shape, F32)
    for pos, v in enumerate(vals):
        route = jnp.where(lane == pos, v, route)
    route_ref[...] = route


def proj_residual_router(o, wo_bf, xs, mods, layer, norm_g, w_r, b_r, *, tm, seq, batch, n_groups, epg):
    t, d = xs.shape
    k = o.shape[1]
    gate_spec, sh_spec, sc_spec = _mod_specs(layer, (2, 3, 4), tm, seq, batch, d)
    return pl.pallas_call(
        functools.partial(_proj_router_kernel, n_groups=n_groups, epg=epg),
        grid=(t // tm, 1),
        in_specs=[pl.BlockSpec((tm, k), lambda i, j: (i, 0)),
                  pl.BlockSpec((k, d), lambda i, j: (0, 0), pipeline_mode=pl.Buffered(1)),
                  pl.BlockSpec((tm, d), lambda i, j: (i, 0)),
                  gate_spec,
                  pl.BlockSpec((1, d), lambda i, j: (0, 0)),
                  sh_spec, sc_spec,
                  pl.BlockSpec((d, LANES), lambda i, j: (0, 0)),
                  pl.BlockSpec((1, LANES), lambda i, j: (0, 0))],
        out_specs=[pl.BlockSpec((tm, d), lambda i, j: (i, 0)),
                   pl.BlockSpec((tm, d // 2), lambda i, j: (i, 0)),
                   pl.BlockSpec((tm, LANES), lambda i, j: (i, 0)),
                   pl.BlockSpec((8, LANES), lambda i, j: (0, 0))],
        out_shape=[jax.ShapeDtypeStruct((t, d), F32), jax.ShapeDtypeStruct((t, d // 2), jnp.uint32),
                   jax.ShapeDtypeStruct((t, LANES), F32), jax.ShapeDtypeStruct((8, LANES), F32)],
        scratch_shapes=[pltpu.VMEM((1, LANES), F32)],
        compiler_params=_cparams("arbitrary", "arbitrary"),
        name="proj_residual_router",
    )(o, wo_bf, xs, mods, norm_g.reshape(1, d), mods, mods, w_r, b_r)


def _issue_row_gather(idx_ref, src_hbm, dst, sem, n_rows):
    def body(r, carry):
        tok = idx_ref[0, 0, r]
        pltpu.make_async_copy(src_hbm.at[pl.ds(tok, 1)], dst.at[pl.ds(r, 1)], sem).start()
        return carry
    lax.fori_loop(0, n_rows, body, 0, unroll=8)


def _wait_row_gather(src_hbm, dst, sem, n_rows):
    pltpu.make_async_copy(src_hbm.at[pl.ds(0, n_rows)], dst, sem).wait()


def _expert_kernel(be_ref, idx_ref, idx_next_ref, h_hbm, win_ref, wout_ref, y_ref,
                   xbuf_a, xbuf_b, sem, win_bf, wout_bf, *, nb, cast_rows, k_chunk):
    i = pl.program_id(0)
    blk, half = xbuf_a.shape
    n_k = 2 * half // k_chunk
    rows_per_chunk = blk // n_k

    @pl.when(i == 0)
    def _():
        _issue_row_gather(idx_ref, h_hbm, xbuf_a, sem.at[0], blk)

    expert_changed = (i == 0) | (be_ref[i] != be_ref[jnp.maximum(i - 1, 0)])

    @pl.when(expert_changed)
    def _():
        def cast_in(c, carry):
            r0 = pl.multiple_of(c * cast_rows, cast_rows)
            win_bf[pl.ds(r0, cast_rows), :] = win_ref[0, 0, pl.ds(r0, cast_rows), :].astype(BF16)
            return carry
        lax.fori_loop(0, win_bf.shape[0] // cast_rows, cast_in, 0)

        def cast_out(c, carry):
            r0 = pl.multiple_of(c * cast_rows, cast_rows)
            wout_bf[pl.ds(r0, cast_rows), :] = wout_ref[0, 0, pl.ds(r0, cast_rows), :].astype(BF16)
            return carry
        lax.fori_loop(0, wout_bf.shape[0] // cast_rows, cast_out, 0)

    def run(x_cur, sem_cur, x_next, sem_next):
        _wait_row_gather(h_hbm, x_cur, sem_cur, blk)
        gu = None
        for kk in range(n_k):
            for r in range(kk * rows_per_chunk, (kk + 1) * rows_per_chunk):
                pltpu.make_async_copy(h_hbm.at[pl.ds(idx_next_ref[0, 0, r], 1)], x_next.at[pl.ds(r, 1)],
                                      sem_next).start()
            f0 = kk * k_chunk
            words = x_cur[:, f0 % half:f0 % half + k_chunk]
            xk = _unpack_lo(words) if f0 < half else _unpack_hi(words)
            part = jnp.dot(xk.astype(BF16), win_bf[f0:f0 + k_chunk, :], preferred_element_type=F32)
            gu = part if gu is None else gu + part
        f = gu.shape[1] // 2
        act = _silu(gu[:, :f]) * gu[:, f:]
        y_ref[...] = _pack_bf16_pairs(jnp.dot(act.astype(BF16), wout_bf[...], preferred_element_type=F32))

        @pl.when(i == nb - 1)
        def _():
            _wait_row_gather(h_hbm, x_next, sem_next, blk)

    @pl.when(i % 2 == 0)
    def _():
        run(xbuf_a, sem.at[0], xbuf_b, sem.at[1])

    @pl.when(i % 2 == 1)
    def _():
        run(xbuf_b, sem.at[1], xbuf_a, sem.at[0])


def moe_experts(h, slot_tok, block_expert, w_in, w_out, layer):
    half = h.shape[1]
    d = 2 * half
    f2 = w_in.shape[-1]
    f = f2 // 2
    nb = block_expert.shape[0]
    blk = slot_tok.shape[0] // nb
    idx = slot_tok.reshape(nb, 1, blk)
    grid_spec = pltpu.PrefetchScalarGridSpec(
        num_scalar_prefetch=1,
        grid=(nb,),
        in_specs=[pl.BlockSpec((1, 1, blk), lambda i, be: (i, 0, 0), memory_space=pltpu.SMEM),
                  pl.BlockSpec((1, 1, blk), lambda i, be: (jnp.minimum(i + 1, nb - 1), 0, 0),
                               memory_space=pltpu.SMEM),
                  pl.BlockSpec(memory_space=pl.ANY),
                  pl.BlockSpec((1, 1, d, f2), lambda i, be: (layer, be[i], 0, 0)),
                  pl.BlockSpec((1, 1, f, d), lambda i, be: (layer, be[i], 0, 0))],
        out_specs=pl.BlockSpec((blk, half), lambda i, be: (i, 0)),
        scratch_shapes=[pltpu.VMEM((blk, half), jnp.uint32),
                        pltpu.VMEM((blk, half), jnp.uint32),
                        pltpu.SemaphoreType.DMA((2,)),
                        pltpu.VMEM((d, f2), BF16),
                        pltpu.VMEM((f, d), BF16)],
    )
    return pl.pallas_call(
        functools.partial(_expert_kernel, nb=nb, cast_rows=256, k_chunk=256),
        grid_spec=grid_spec,
        out_shape=jax.ShapeDtypeStruct((nb * blk, half), jnp.uint32),
        compiler_params=_cparams("arbitrary"),
        name="moe_experts",
    )(block_expert, idx, idx, h, w_in, w_out)


def _combine_kernel(pos_ref, pos_next_ref, yb_hbm, x_ref, route_ref, gate_ref, o_ref, ybuf, sem, *, nt):
    i = pl.program_id(0)
    slot = i % 2
    n_rows = ybuf.shape[1]
    tm = n_rows // MOE_TOP_K

    @pl.when(i == 0)
    def _():
        _issue_row_gather(pos_ref, yb_hbm, ybuf.at[0], sem.at[0], n_rows)

    @pl.when(i + 1 < nt)
    def _():
        _issue_row_gather(pos_next_ref, yb_hbm, ybuf.at[1 - slot], sem.at[1 - slot], n_rows)

    _wait_row_gather(yb_hbm, ybuf.at[slot], sem.at[slot], n_rows)
    route = route_ref[...]
    w1, w2 = route[:, 2:3], route[:, 3:4]
    y1, y2 = ybuf[slot, :tm, :], ybuf[slot, tm:, :]
    half = y1.shape[1]
    gate = gate_ref[0]
    o_ref[:, :half] = x_ref[:, :half] + gate[:, :half] * (w1 * _unpack_lo(y1) + w2 * _unpack_lo(y2))
    o_ref[:, half:] = x_ref[:, half:] + gate[:, half:] * (w1 * _unpack_hi(y1) + w2 * _unpack_hi(y2))


def moe_combine(yb, pos, xs, route, mods, layer, *, tm, seq, batch, out_rows):
    t, d = xs.shape
    pos_t = pos.reshape(t // tm, tm, MOE_TOP_K).transpose(0, 2, 1).reshape(t // tm, 1, MOE_TOP_K * tm)
    nt = out_rows // tm

    def seg(i):
        return jnp.minimum((i * tm) // seq, batch)

    return pl.pallas_call(
        functools.partial(_combine_kernel, nt=nt),
        grid=(nt,),
        in_specs=[pl.BlockSpec((1, 1, MOE_TOP_K * tm), lambda i: (i, 0, 0), memory_space=pltpu.SMEM),
                  pl.BlockSpec((1, 1, MOE_TOP_K * tm), lambda i: (jnp.minimum(i + 1, nt - 1), 0, 0),
                               memory_space=pltpu.SMEM),
                  pl.BlockSpec(memory_space=pl.ANY),
                  pl.BlockSpec((tm, d), lambda i: (i, 0)),
                  pl.BlockSpec((tm, LANES), lambda i: (i, 0)),
                  pl.BlockSpec((1, 1, d), lambda i: ((layer * MOD_ROWS + seg(i)) * 6 + 5, 0, 0))],
        out_specs=pl.BlockSpec((tm, d), lambda i: (i, 0)),
        out_shape=jax.ShapeDtypeStruct((out_rows, d), F32),
        scratch_shapes=[pltpu.VMEM((2, MOE_TOP_K * tm, d // 2), jnp.uint32), pltpu.SemaphoreType.DMA((2,))],
        compiler_params=_cparams("arbitrary"),
        name="moe_combine",
    )(pos_t, pos_t, yb, xs, route, mods)


def _dispatch_plan(route, counts, n_experts, blk):
    t = route.shape[0]
    nk = t * MOE_TOP_K
    padded = (counts + blk - 1) // blk * blk
    pad_end = jnp.cumsum(padded)
    pad_start = pad_end - padded
    nb = -(-(nk + n_experts * (blk - 1)) // blk)
    e = route[:, :MOE_TOP_K].astype(jnp.int32)
    rank = route[:, 2 * MOE_TOP_K:3 * MOE_TOP_K].astype(jnp.int32)
    onehot = e[..., None] == jnp.arange(n_experts, dtype=jnp.int32)
    dest = jnp.sum(jnp.where(onehot, pad_start, 0), axis=-1) + rank
    tok = jnp.arange(nk, dtype=jnp.int32) // MOE_TOP_K
    slot_tok = jnp.zeros((nb * blk,), jnp.int32).at[dest.reshape(nk)].set(tok, unique_indices=True)
    first_row = jnp.arange(nb, dtype=jnp.int32) * blk
    block_expert = jnp.minimum(jnp.sum(pad_end[None, :] <= first_row[:, None], axis=1), n_experts - 1)
    return slot_tok, block_expert.astype(jnp.int32), dest


def _axial_angles(seq, dim):
    n_freq = dim // 4
    freqs = ROPE_THETA ** (-jnp.arange(n_freq, dtype=F32) / n_freq)
    tok = jnp.arange(seq, dtype=jnp.int32)
    row = (tok // GRID_W).astype(F32)
    col = (tok % GRID_W).astype(F32)
    return jnp.concatenate([row[:, None] * freqs, col[:, None] * freqs], axis=-1)


def _stream_table(lat, ctx_fill, batch, n_ctx_rows):
    ctx_rows = jnp.broadcast_to(ctx_fill[None, :], (n_ctx_rows, lat.shape[1]))
    return jnp.concatenate([jnp.tile(lat, (batch, 1)), ctx_rows], axis=0)


def _rope_tables_128(seq, batch, n_ctx_rows):
    ang = _axial_angles(seq, HEAD_DIM)
    cos, sin = jnp.cos(ang), jnp.sin(ang)
    c = jnp.concatenate([cos, cos], axis=-1)
    s = jnp.concatenate([-sin, sin], axis=-1)
    return (_stream_table(c, jnp.ones((HEAD_DIM,), F32), batch, n_ctx_rows),
            _stream_table(s, jnp.zeros((HEAD_DIM,), F32), batch, n_ctx_rows))


def _rope_tables_64(seq, batch, n_ctx_rows):
    ang = _axial_angles(seq, MLA_ROPE)
    cos, sin = jnp.cos(ang), jnp.sin(ang)
    half = MLA_ROPE // 2
    z = jnp.zeros((seq, LANES - MLA_ROPE), F32)
    zh = jnp.zeros((seq, half), F32)
    c = jnp.concatenate([cos, cos, z], axis=-1)
    s1 = jnp.concatenate([-sin, zh, z], axis=-1)
    s2 = jnp.concatenate([zh, sin, z], axis=-1)
    ones_pad = jnp.concatenate([jnp.ones((MLA_ROPE,), F32), jnp.zeros((LANES - MLA_ROPE,), F32)])
    zeros = jnp.zeros((LANES,), F32)
    return (_stream_table(c, ones_pad, batch, n_ctx_rows),
            _stream_table(s1, zeros, batch, n_ctx_rows),
            _stream_table(s2, zeros, batch, n_ctx_rows))


def _na_bias_table(rpb, kr):
    col = jnp.arange(GRID_W, dtype=jnp.int32)
    col_start = jnp.clip(col - NA_COLS // 2, 0, GRID_W - NA_COLS)
    col_in = (col[None, :] >= col_start[:, None]) & (col[None, :] < col_start[:, None] + NA_COLS)
    dc = jnp.clip(col[None, :] - col[:, None] + NA_COLS - 1, 0, 2 * NA_COLS - 2)
    h = rpb.shape[0]
    by_col = jnp.zeros((h, 2 * NA_ROWS - 1, GRID_W, GRID_W), F32)
    for cc in range(2 * NA_COLS - 1):
        by_col = jnp.where(dc[None, None] == cc, rpb[:, :, cc, None, None].astype(F32), by_col)
    by_col = jnp.where(col_in[None, None], by_col, NEG_INF)
    per_shift = [by_col[:, NA_ROWS - 1 - s:NA_ROWS - 1 - s + kr] for s in range(kr)]
    bias = jnp.stack(per_shift, axis=1)
    return bias.transpose(0, 1, 3, 2, 4).reshape(h, kr, GRID_W, kr * GRID_W)


def _pick_tile(*extents):
    for tm in (512, 256, 128):
        if all(e % tm == 0 for e in extents):
            return tm
    raise ValueError("row extents must be multiples of 128")


def kernel(x, c, ctx, c_ctx, mod_w, mod_b, norm_g, gqa_w_qkv, gqa_q_norm, gqa_k_norm, gqa_w_o, na_w_qkv, na_q_norm, na_k_norm, na_rpb, na_w_o, mla_w_down, mla_q_lora_norm, mla_w_uq, mla_kv_lora_norm, mla_w_ukv, mla_q_norm, mla_k_norm, mla_w_o, moe_w_group, moe_b_group, moe_w_expert, moe_b_expert, moe_w_in, moe_w_out):
    batch, seq, d = x.shape
    n_ctx = ctx.shape[1]
    depth = mod_w.shape[0]
    heads = d // HEAD_DIM
    kv_heads = heads // 4
    n_lat = batch * seq
    n_ctx_rows = batch * n_ctx
    assert batch + 1 <= MOD_ROWS and n_lat % n_ctx == 0 and seq % GRID_W == 0
    tm = _pick_tile(seq, n_ctx_rows)
    tq = min(FLASH_TQ, seq)
    tk = min(FLASH_TK, seq)
    assert seq % tq == 0 and seq % (2 * tk) == 0
    tn_d = min(512, d)
    dims = dict(tm=tm, seq=seq, batch=batch)

    xs = jnp.concatenate([x.reshape(n_lat, d), ctx.reshape(n_ctx_rows, d)], axis=0)

    cond = jnp.zeros((MOD_ROWS, d), F32).at[:batch].set(c).at[batch].set(c_ctx)
    mods = adaln_all(cond, mod_w, mod_b).reshape(depth * MOD_ROWS * 6, 1, d)

    cos_a, sin_a = _rope_tables_128(seq, batch, n_ctx_rows)
    c64, s64a, s64b = _rope_tables_64(seq, batch, n_ctx_rows)

    n_groups = moe_w_group.shape[-1]
    epg = moe_w_expert.shape[-1]
    n_experts = n_groups * epg
    assert n_groups + n_experts <= LANES

    for i in range(depth):
        kind, j = i % 3, i // 3
        if kind == 0:
            scale = HEAD_DIM ** -0.5 * LOG2E
            nq, nkv = heads * HEAD_DIM, kv_heads * HEAD_DIM
            gain = jnp.concatenate([jnp.tile(gqa_q_norm[j] * scale, heads), jnp.tile(gqa_k_norm[j], kv_heads),
                                    jnp.ones((nkv,), F32)]).reshape(1, -1)
            qkv = qkv_project(xs, mods, i, norm_g[i, 0], gqa_w_qkv[j].astype(BF16), gain, cos_a, sin_a,
                              n_norm_cols=nq + nkv, tn=min(512, nkv), rope=True, **dims)
            cols = dict(dk=HEAD_DIM, q_col0=0, k_col0=heads, v_col0=heads + kv_heads, kv_group=heads // kv_heads)
            o_ctx = ctx_attention(qkv, qkv, qkv, batch=batch, seq=seq, ctx=n_ctx, heads=heads, **cols)
            o = dense_attention(qkv, qkv, qkv, o_ctx, batch=batch, seq=seq, ctx=n_ctx, heads=heads, tq=tq, tk=tk,
                                **cols)
            w_o = gqa_w_o[j]
        elif kind == 1:
            scale = HEAD_DIM ** -0.5 * LOG2E
            nq = heads * HEAD_DIM
            gain = jnp.concatenate([jnp.tile(na_q_norm[j] * scale, heads), jnp.tile(na_k_norm[j], heads),
                                    jnp.ones((nq,), F32)]).reshape(1, -1)
            qkv = qkv_project(xs, mods, i, norm_g[i, 0], na_w_qkv[j].astype(BF16), gain, cos_a, sin_a,
                              n_norm_cols=2 * nq, tn=tn_d, rope=False, **dims)
            rows = seq // GRID_W
            tb = _na_bias_table(na_rpb[j], min(NA_ROWS, rows)) * LOG2E
            o_ctx = ctx_attention(qkv, qkv, qkv, batch=batch, seq=seq, ctx=n_ctx, heads=heads, dk=HEAD_DIM,
                                  q_col0=0, k_col0=heads, v_col0=2 * heads, kv_group=1)
            o = na_attention(qkv, tb, o_ctx, batch=batch, seq=seq, ctx=n_ctx, heads=heads)
            w_o = na_w_o[j]
        else:
            scale = MLA_QK ** -0.5 * LOG2E
            q_lora = mla_q_lora_norm.shape[-1]
            kv_lora = mla_kv_lora_norm.shape[-1]
            assert q_lora % kv_lora == 0 and (q_lora + kv_lora) % LANES == 0
            n_down = q_lora + kv_lora + LANES
            w_down = jnp.pad(mla_w_down[j], ((0, 0), (0, n_down - mla_w_down.shape[-1]))).astype(BF16)
            cfull = mla_down(xs, mods, i, norm_g[i, 0], w_down, **dims)
            pad = MLA_HEAD_PAD - MLA_QK
            w_uq = jnp.pad(mla_w_uq[j].reshape(q_lora, heads, MLA_QK), ((0, 0), (0, 0), (0, pad)))
            w_uq = w_uq.reshape(q_lora, heads * MLA_HEAD_PAD).astype(BF16)
            q_gain = jnp.tile(jnp.pad(mla_q_norm[j] * scale, (0, pad)), heads).reshape(1, -1)
            qa = mla_q_project(cfull, mla_q_lora_norm[j], w_uq, q_gain, c64, s64a, s64b, tm=tm)
            gain_n = mla_k_norm[j, :MLA_NOPE].reshape(1, LANES)
            gain_t = jnp.pad(mla_k_norm[j, MLA_NOPE:], (0, LANES - MLA_ROPE)).reshape(1, LANES)
            ka, va = mla_kv_project(cfull, mla_kv_lora_norm[j], mla_w_ukv[j].astype(BF16), gain_n, gain_t,
                                    c64, s64a, s64b, tm=tm, q_lora=q_lora)
            cols = dict(dk=MLA_HEAD_PAD, q_col0=0, k_col0=0, v_col0=0, kv_group=1)
            o_ctx = ctx_attention(qa, ka, va, batch=batch, seq=seq, ctx=n_ctx, heads=heads, **cols)
            o = dense_attention(qa, ka, va, o_ctx, batch=batch, seq=seq, ctx=n_ctx, heads=heads, tq=tq, tk=tk, **cols)
            w_o = mla_w_o[j]

        w_r = jnp.concatenate([moe_w_group[i], moe_w_expert[i].transpose(1, 0, 2).reshape(d, n_experts)], axis=1)
        w_r = jnp.pad(w_r, ((0, 0), (0, LANES - w_r.shape[1])))
        b_r = jnp.pad(jnp.concatenate([moe_b_group[i], moe_b_expert[i].reshape(-1)]),
                      (0, LANES - n_groups - n_experts)).reshape(1, LANES)
        xs, h2, route, lane_counts = proj_residual_router(o, w_o.astype(BF16), xs, mods, i, norm_g[i, 1], w_r, b_r,
                                                          n_groups=n_groups, epg=epg, **dims)
        counts = lane_counts[0, n_groups:n_groups + n_experts].astype(jnp.int32)
        slot_tok, block_expert, pos = _dispatch_plan(route, counts, n_experts, MOE_BLOCK if i < depth - 1 else 256)
        yb = moe_experts(h2, slot_tok, block_expert, moe_w_in, moe_w_out, i)
        out_rows = n_lat if i == depth - 1 else n_lat + n_ctx_rows
        xs = moe_combine(yb, pos, xs, route, mods, i, tm=min(COMBINE_TM, tm), seq=seq, batch=batch,
                         out_rows=out_rows)

    return xs.reshape(batch, seq, d)
```

```python
import functools

import numpy as np
import jax
import jax.numpy as jnp
from jax import lax
from jax.experimental import pallas as pl
from jax.experimental.pallas import tpu as pltpu

F32 = jnp.float32
BF16 = jnp.bfloat16

GRID_W = 64
HEAD_DIM = 128
ROPE_THETA = 10000.0
NORM_EPS = 1e-6
NEG_INF = -1e30
NA_ROWS = 8
NA_COLS = 16
MLA_NOPE = 128
MLA_ROPE = 64
MLA_V = 128
MLA_QK = MLA_NOPE + MLA_ROPE
MLA_HEAD_PAD = 256
MOE_TOP_K = 2
LANES = 128
MOD_ROWS = 8
MOE_BLOCK = 256
COMBINE_TM = 256
FLASH_TQ = 1024
FLASH_TK = 512
FLASH_ROW_GROUP = 32
LOG2E = 1.4426950408889634
VMEM_LIMIT = 52 * 1024 * 1024


def _cparams(*sem):
    return pltpu.CompilerParams(dimension_semantics=sem, vmem_limit_bytes=VMEM_LIMIT)


def _silu(v):
    return v / (1.0 + jnp.exp(-v))


def _adaln_kernel(c_ref, w_ref, b_ref, o_ref):
    cond = _silu(c_ref[...]).astype(BF16)
    o_ref[0] = jnp.dot(cond, w_ref[0].astype(BF16), preferred_element_type=F32) + b_ref[0]


def adaln_all(cond, mod_w, mod_b):
    depth, d, n = mod_w.shape
    tn = 1024
    return pl.pallas_call(
        _adaln_kernel,
        grid=(depth, n // tn),
        in_specs=[pl.BlockSpec((MOD_ROWS, d), lambda l, j: (0, 0)),
                  pl.BlockSpec((1, d, tn), lambda l, j: (l, 0, j)),
                  pl.BlockSpec((1, 1, tn), lambda l, j: (l, 0, j))],
        out_specs=pl.BlockSpec((1, MOD_ROWS, tn), lambda l, j: (l, 0, j)),
        out_shape=jax.ShapeDtypeStruct((depth, MOD_ROWS, n), F32),
        compiler_params=_cparams("arbitrary", "arbitrary"),
        name="adaln",
    )(cond, mod_w, mod_b.reshape(depth, 1, n))


def _rms(x, g):
    ms = jnp.mean(x * x, axis=-1, keepdims=True)
    return x * lax.rsqrt(ms + NORM_EPS) * g


def _rope128(y, c, s):
    return y * c + pltpu.roll(y, 64, 1) * s


def _rope64(y, c, s1, s2):
    return y * c + pltpu.roll(y, 96, 1) * s1 + pltpu.roll(y, 32, 1) * s2


def _qkv_kernel(x_ref, g_ref, sh_ref, sc_ref, w_ref, gain_ref, c_ref, s_ref, o_ref, *, tn, n_norm_cols, rope):
    h = (_rms(x_ref[...], g_ref[...]) * (1.0 + sc_ref[0]) + sh_ref[0]).astype(BF16)
    for c0 in range(0, w_ref.shape[1], tn):
        y = jnp.dot(h, w_ref[:, c0:c0 + tn], preferred_element_type=F32)
        if c0 >= n_norm_cols:
            o_ref[:, c0:c0 + tn] = y.astype(o_ref.dtype)
            continue
        for hh in range(tn // HEAD_DIM):
            sl = slice(c0 + hh * HEAD_DIM, c0 + (hh + 1) * HEAD_DIM)
            yh = _rms(y[:, hh * HEAD_DIM:(hh + 1) * HEAD_DIM], gain_ref[:, sl])
            if rope:
                yh = _rope128(yh, c_ref[...], s_ref[...])
            o_ref[:, sl] = yh.astype(o_ref.dtype)


def _mod_specs(layer, which, tm, seq, batch, d):
    def seg(i):
        return jnp.minimum((i * tm) // seq, batch)
    return [pl.BlockSpec((1, 1, d), lambda i, j, w=w: ((layer * MOD_ROWS + seg(i)) * 6 + w, 0, 0)) for w in which]


def qkv_project(xs, mods, layer, norm_g, w_bf, gain, cos_t, sin_t, *, tm, seq, batch, n_norm_cols, tn, rope):
    t, d = xs.shape
    n = w_bf.shape[1]
    sh_spec, sc_spec = _mod_specs(layer, (0, 1), tm, seq, batch, d)
    assert n_norm_cols % tn == 0 and n % tn == 0
    kern = functools.partial(_qkv_kernel, tn=tn, n_norm_cols=n_norm_cols, rope=rope)
    return pl.pallas_call(
        kern,
        grid=(t // tm, 1),
        in_specs=[pl.BlockSpec((tm, d), lambda i, j: (i, 0)),
                  pl.BlockSpec((1, d), lambda i, j: (0, 0)),
                  sh_spec, sc_spec,
                  pl.BlockSpec((d, n), lambda i, j: (0, 0), pipeline_mode=pl.Buffered(1)),
                  pl.BlockSpec((1, n), lambda i, j: (0, 0)),
                  pl.BlockSpec((tm, HEAD_DIM), lambda i, j: (i, 0)),
                  pl.BlockSpec((tm, HEAD_DIM), lambda i, j: (i, 0))],
        out_specs=pl.BlockSpec((tm, n), lambda i, j: (i, 0)),
        out_shape=jax.ShapeDtypeStruct((t, n), BF16),
        compiler_params=_cparams("arbitrary", "arbitrary"),
        name="qkv_project",
    )(xs, norm_g.reshape(1, d), mods, mods, w_bf, gain, cos_t, sin_t)


def _down_kernel(x_ref, g_ref, sh_ref, sc_ref, w_ref, o_ref):
    h = _rms(x_ref[...], g_ref[...]) * (1.0 + sc_ref[0]) + sh_ref[0]
    o_ref[...] = jnp.dot(h.astype(BF16), w_ref[...], preferred_element_type=F32)


def mla_down(xs, mods, layer, norm_g, w_bf, *, tm, seq, batch):
    t, d = xs.shape
    n = w_bf.shape[1]
    sh_spec, sc_spec = _mod_specs(layer, (0, 1), tm, seq, batch, d)
    return pl.pallas_call(
        _down_kernel,
        grid=(t // tm, 1),
        in_specs=[pl.BlockSpec((tm, d), lambda i, j: (i, 0)),
                  pl.BlockSpec((1, d), lambda i, j: (0, 0)),
                  sh_spec, sc_spec,
                  pl.BlockSpec((d, n), lambda i, j: (0, 0))],
        out_specs=pl.BlockSpec((tm, n), lambda i, j: (i, 0)),
        out_shape=jax.ShapeDtypeStruct((t, n), F32),
        compiler_params=_cparams("arbitrary", "arbitrary"),
        name="mla_down",
    )(xs, norm_g.reshape(1, d), mods, mods, w_bf)


def _mla_q_kernel(cq_ref, g_ref, w_ref, gain_ref, c_ref, s1_ref, s2_ref, o_ref):
    h = _rms(cq_ref[...], g_ref[...]).astype(BF16)
    for hh in range(w_ref.shape[1] // MLA_HEAD_PAD):
        lo = hh * MLA_HEAD_PAD
        yh = jnp.dot(h, w_ref[:, lo:lo + MLA_HEAD_PAD], preferred_element_type=F32)
        ms = jnp.sum(yh * yh, axis=-1, keepdims=True) * (1.0 / MLA_QK)
        yh = yh * lax.rsqrt(ms + NORM_EPS) * gain_ref[:, lo:lo + MLA_HEAD_PAD]
        o_ref[:, lo:lo + MLA_NOPE] = yh[:, :MLA_NOPE].astype(o_ref.dtype)
        tail = _rope64(yh[:, MLA_NOPE:], c_ref[...], s1_ref[...], s2_ref[...])
        o_ref[:, lo + MLA_NOPE:lo + MLA_HEAD_PAD] = tail.astype(o_ref.dtype)


def mla_q_project(cfull, q_lora_g, w_bf, gain, c_t, s1_t, s2_t, *, tm):
    t = cfull.shape[0]
    kq, n = w_bf.shape
    return pl.pallas_call(
        _mla_q_kernel,
        grid=(t // tm, 1),
        in_specs=[pl.BlockSpec((tm, kq), lambda i, j: (i, 0)),
                  pl.BlockSpec((1, kq), lambda i, j: (0, 0)),
                  pl.BlockSpec((kq, n), lambda i, j: (0, 0), pipeline_mode=pl.Buffered(1)),
                  pl.BlockSpec((1, n), lambda i, j: (0, 0)),
                  pl.BlockSpec((tm, LANES), lambda i, j: (i, 0)),
                  pl.BlockSpec((tm, LANES), lambda i, j: (i, 0)),
                  pl.BlockSpec((tm, LANES), lambda i, j: (i, 0))],
        out_specs=pl.BlockSpec((tm, n), lambda i, j: (i, 0)),
        out_shape=jax.ShapeDtypeStruct((t, n), BF16),
        compiler_params=_cparams("arbitrary", "arbitrary"),
        name="mla_q_project",
    )(cfull, q_lora_g.reshape(1, kq), w_bf, gain, c_t, s1_t, s2_t)


def _mla_kv_kernel(ckv_ref, g_ref, kr_ref, w_ref, gn_ref, gt_ref, c_ref, s1_ref, s2_ref, k_ref, v_ref):
    h = _rms(ckv_ref[...], g_ref[...]).astype(BF16)
    kr = kr_ref[...]
    ss_rope = jnp.sum(kr * kr, axis=-1, keepdims=True)
    rot = _rope64(kr * gt_ref[...], c_ref[...], s1_ref[...], s2_ref[...])
    per_head = MLA_NOPE + MLA_V
    for hh in range(w_ref.shape[1] // per_head):
        y = jnp.dot(h, w_ref[:, hh * per_head:(hh + 1) * per_head], preferred_element_type=F32)
        kn = y[:, :MLA_NOPE]
        ms = (jnp.sum(kn * kn, axis=-1, keepdims=True) + ss_rope) * (1.0 / MLA_QK)
        rs = lax.rsqrt(ms + NORM_EPS)
        lo = hh * MLA_HEAD_PAD
        k_ref[:, lo:lo + MLA_NOPE] = (kn * rs * gn_ref[...]).astype(k_ref.dtype)
        k_ref[:, lo + MLA_NOPE:lo + MLA_HEAD_PAD] = (rot * rs).astype(k_ref.dtype)
        v_ref[:, hh * MLA_V:(hh + 1) * MLA_V] = y[:, MLA_NOPE:].astype(v_ref.dtype)


def mla_kv_project(cfull, kv_lora_g, w_bf, gain_n, gain_t, c_t, s1_t, s2_t, *, tm, q_lora):
    t = cfull.shape[0]
    kkv, n = w_bf.shape
    heads = n // (MLA_NOPE + MLA_V)
    return pl.pallas_call(
        _mla_kv_kernel,
        grid=(t // tm, 1),
        in_specs=[pl.BlockSpec((tm, kkv), lambda i, j: (i, q_lora // kkv)),
                  pl.BlockSpec((1, kkv), lambda i, j: (0, 0)),
                  pl.BlockSpec((tm, LANES), lambda i, j: (i, (q_lora + kkv) // LANES)),
                  pl.BlockSpec((kkv, n), lambda i, j: (0, 0), pipeline_mode=pl.Buffered(1)),
                  pl.BlockSpec((1, LANES), lambda i, j: (0, 0)),
                  pl.BlockSpec((1, LANES), lambda i, j: (0, 0)),
                  pl.BlockSpec((tm, LANES), lambda i, j: (i, 0)),
                  pl.BlockSpec((tm, LANES), lambda i, j: (i, 0)),
                  pl.BlockSpec((tm, LANES), lambda i, j: (i, 0))],
        out_specs=[pl.BlockSpec((tm, heads * MLA_HEAD_PAD), lambda i, j: (i, 0)),
                   pl.BlockSpec((tm, heads * MLA_V), lambda i, j: (i, 0))],
        out_shape=[jax.ShapeDtypeStruct((t, heads * MLA_HEAD_PAD), BF16),
                   jax.ShapeDtypeStruct((t, heads * MLA_V), BF16)],
        compiler_params=_cparams("arbitrary", "arbitrary"),
        name="mla_kv_project",
    )(cfull, kv_lora_g.reshape(1, kkv), cfull, w_bf, gain_n, gain_t, c_t, s1_t, s2_t)


_NT = (((1,), (1,)), ((), ()))


def _flash_scores(q, k, s_buf):
    s_buf[:, :k.shape[0]] = lax.dot_general(q, k, _NT, preferred_element_type=F32)


def _flash_update(s_scr, v, p_scr, m_scr, l_scr, acc_scr, rg):
    tq = s_scr.shape[0]
    w = v.shape[0]
    n_tiles = w // LANES

    for g in range(tq // rg):
        rows_ = slice(g * rg, (g + 1) * rg)
        tiles = [s_scr[rows_, t * LANES:(t + 1) * LANES] for t in range(n_tiles)]
        mx = tiles[0]
        for st in tiles[1:]:
            mx = jnp.maximum(mx, st)
        m_prev = m_scr[rows_, :]
        m_next = jnp.maximum(m_prev, jnp.max(mx, axis=-1, keepdims=True))
        alpha = jnp.exp2(m_prev - m_next)
        psum = None
        for t, st in enumerate(tiles):
            p = jnp.exp2(st - m_next)
            p_scr[rows_, t * LANES:(t + 1) * LANES] = p.astype(p_scr.dtype)
            psum = p if psum is None else psum + p
        l_scr[rows_, :] = alpha * l_scr[rows_, :] + psum
        acc_scr[rows_, :] = alpha * acc_scr[rows_, :]
        m_scr[rows_, :] = m_next
    acc_scr[...] += jnp.dot(p_scr[:, :w], v, preferred_element_type=F32)


def _flash_kernel(*refs, n_lat_chunks, tk, rg):
    if n_lat_chunks:
        q_ref, kc_ref, vc_ref, k_ref, v_ref, _, o_ref, s0, s1, p0, p1, m_scr, l_scr, acc_scr = refs
    else:
        q_ref, kc_ref, vc_ref, _, o_ref, s0, s1, p0, p1, m_scr, l_scr, acc_scr = refs
    state = (m_scr, l_scr, acc_scr, rg)
    m_scr[...] = jnp.full(m_scr.shape, -jnp.inf, F32)
    l_scr[...] = jnp.zeros(l_scr.shape, F32)
    acc_scr[...] = jnp.zeros(acc_scr.shape, F32)
    q = q_ref[...]
    _flash_scores(q, kc_ref[...], s0)
    if n_lat_chunks:
        assert n_lat_chunks % 2 == 0

        def k_chunk(c):
            return k_ref[pl.ds(pl.multiple_of(c * tk, tk), tk), :]

        def v_chunk(c):
            return v_ref[pl.ds(pl.multiple_of(c * tk, tk), tk), :]

        _flash_scores(q, k_chunk(0), s1)
        _flash_update(s0, vc_ref[...], p0, *state)

        def body(j, carry):
            _flash_scores(q, k_chunk(2 * j + 1), s0)
            _flash_update(s1, v_chunk(2 * j), p1, *state)
            _flash_scores(q, k_chunk(2 * j + 2), s1)
            _flash_update(s0, v_chunk(2 * j + 1), p0, *state)
            return carry
        lax.fori_loop(0, n_lat_chunks // 2 - 1, body, 0)
        _flash_scores(q, k_chunk(n_lat_chunks - 1), s0)
        _flash_update(s1, v_chunk(n_lat_chunks - 2), p1, *state)
        _flash_update(s0, v_chunk(n_lat_chunks - 1), p0, *state)
    else:
        _flash_update(s0, vc_ref[...], p0, *state)
    l = jnp.sum(l_scr[...], axis=-1, keepdims=True)
    o_ref[...] = (acc_scr[...] / l).astype(o_ref.dtype)


def _flash_scratch(tq, w):
    return [pltpu.VMEM((tq, w), F32), pltpu.VMEM((tq, w), F32),
            pltpu.VMEM((tq, w), BF16), pltpu.VMEM((tq, w), BF16),
            pltpu.VMEM((tq, LANES), F32), pltpu.VMEM((tq, LANES), F32), pltpu.VMEM((tq, LANES), F32)]


def ctx_attention(qa, ka, va, *, batch, seq, ctx, heads, dk, q_col0, k_col0, v_col0, kv_group):
    t = qa.shape[0]
    ctx_blk0 = (batch * seq) // ctx
    o_shape = jax.ShapeDtypeStruct((t, heads * LANES), BF16)
    return pl.pallas_call(
        functools.partial(_flash_kernel, n_lat_chunks=0, tk=0, rg=FLASH_ROW_GROUP),
        grid=(batch, heads),
        in_specs=[pl.BlockSpec((ctx, dk), lambda b, h: (ctx_blk0 + b, q_col0 + h)),
                  pl.BlockSpec((ctx, dk), lambda b, h: (ctx_blk0 + b, k_col0 + h // kv_group)),
                  pl.BlockSpec((ctx, LANES), lambda b, h: (ctx_blk0 + b, v_col0 + h // kv_group)),
                  pl.BlockSpec(memory_space=pl.ANY)],
        out_specs=pl.BlockSpec((ctx, LANES), lambda b, h: (ctx_blk0 + b, h)),
        out_shape=o_shape,
        input_output_aliases={3: 0},
        scratch_shapes=_flash_scratch(ctx, ctx),
        compiler_params=_cparams("arbitrary", "arbitrary"),
        name="ctx_attention",
    )(qa, ka, va, jnp.zeros(o_shape.shape, o_shape.dtype))


def dense_attention(qa, ka, va, o_ctx, *, batch, seq, ctx, heads, dk, q_col0, k_col0, v_col0, kv_group, tq, tk):
    nq = seq // tq
    ctx_blk0 = (batch * seq) // ctx
    return pl.pallas_call(
        functools.partial(_flash_kernel, n_lat_chunks=seq // tk, tk=tk, rg=FLASH_ROW_GROUP),
        grid=(batch, heads, nq),
        in_specs=[pl.BlockSpec((tq, dk), lambda b, h, i: (b * nq + i, q_col0 + h)),
                  pl.BlockSpec((ctx, dk), lambda b, h, i: (ctx_blk0 + b, k_col0 + h // kv_group)),
                  pl.BlockSpec((ctx, LANES), lambda b, h, i: (ctx_blk0 + b, v_col0 + h // kv_group)),
                  pl.BlockSpec((seq, dk), lambda b, h, i: (b, k_col0 + h // kv_group)),
                  pl.BlockSpec((seq, LANES), lambda b, h, i: (b, v_col0 + h // kv_group)),
                  pl.BlockSpec(memory_space=pl.ANY)],
        out_specs=pl.BlockSpec((tq, LANES), lambda b, h, i: (b * nq + i, h)),
        out_shape=jax.ShapeDtypeStruct(o_ctx.shape, o_ctx.dtype),
        input_output_aliases={5: 0},
        scratch_shapes=_flash_scratch(tq, max(tk, ctx)),
        compiler_params=_cparams("arbitrary", "arbitrary", "arbitrary"),
        name="dense_attention",
    )(qa, ka, va, ka, va, o_ctx)


def _na_kernel(q_ref, k_ref, v_ref, kc_ref, vc_ref, tb_ref, _, o_ref, *, rows, kr):
    kc = kc_ref[...]
    vc = vc_ref[...]
    nt = (((1,), (1,)), ((), ()))

    def body(r, carry):
        rs = jnp.clip(r - kr // 2, 0, rows - kr)
        q0 = pl.multiple_of(r * GRID_W, GRID_W)
        k0 = pl.multiple_of(rs * GRID_W, GRID_W)
        q = q_ref[pl.ds(q0, GRID_W), :]
        kw = k_ref[pl.ds(k0, kr * GRID_W), :]
        vw = v_ref[pl.ds(k0, kr * GRID_W), :]
        s_win = lax.dot_general(q, kw, nt, preferred_element_type=F32) + tb_ref[0, r - rs]
        s_ctx = lax.dot_general(q, kc, nt, preferred_element_type=F32)
        m = jnp.maximum(jnp.max(s_win, axis=-1, keepdims=True), jnp.max(s_ctx, axis=-1, keepdims=True))
        p_win = jnp.exp2(s_win - m)
        p_ctx = jnp.exp2(s_ctx - m)
        l = jnp.sum(p_win, axis=-1, keepdims=True) + jnp.sum(p_ctx, axis=-1, keepdims=True)
        o = (jnp.dot(p_win.astype(vw.dtype), vw, preferred_element_type=F32)
             + jnp.dot(p_ctx.astype(vc.dtype), vc, preferred_element_type=F32))
        o_ref[pl.ds(q0, GRID_W), :] = (o / l).astype(o_ref.dtype)
        return carry

    lax.fori_loop(0, rows, body, 0, unroll=4)


def na_attention(qkv, tb, o_ctx, *, batch, seq, ctx, heads):
    t = qkv.shape[0]
    rows = seq // GRID_W
    kr = min(NA_ROWS, rows)
    ctx_blk0 = (batch * seq) // ctx
    d = HEAD_DIM
    return pl.pallas_call(
        functools.partial(_na_kernel, rows=rows, kr=kr),
        grid=(batch, heads),
        in_specs=[pl.BlockSpec((seq, d), lambda b, h: (b, h)),
                  pl.BlockSpec((seq, d), lambda b, h: (b, heads + h)),
                  pl.BlockSpec((seq, d), lambda b, h: (b, 2 * heads + h)),
                  pl.BlockSpec((ctx, d), lambda b, h: (ctx_blk0 + b, heads + h)),
                  pl.BlockSpec((ctx, d), lambda b, h: (ctx_blk0 + b, 2 * heads + h)),
                  pl.BlockSpec((1, kr, GRID_W, kr * GRID_W), lambda b, h: (h, 0, 0, 0)),
                  pl.BlockSpec(memory_space=pl.ANY)],
        out_specs=pl.BlockSpec((seq, d), lambda b, h: (b, h)),
        out_shape=jax.ShapeDtypeStruct((t, heads * d), BF16),
        input_output_aliases={6: 0},
        compiler_params=_cparams("arbitrary", "arbitrary"),
        name="na_attention",
    )(qkv, qkv, qkv, qkv, qkv, tb, o_ctx)


NA_QROWS = 8
NA_SPAN = 16


def _na_block_kernel(q_ref, k_ref, v_ref, kc_ref, vc_ref, bias_ref, _, o_ref, s_scr, p_scr, l_scr,
                     *, rows, rg):
    t = pl.program_id(2)
    ks = jnp.clip(t * NA_QROWS - NA_ROWS // 2, 0, rows - NA_SPAN)
    k0 = pl.multiple_of(ks * GRID_W, GRID_W)
    n_win = NA_SPAN * GRID_W
    q = q_ref[...]
    kw = k_ref[pl.ds(k0, n_win), :]
    vw = v_ref[pl.ds(k0, n_win), :]
    n_ctx = kc_ref.shape[0]
    s_scr[:, :n_win] = lax.dot_general(q, kw, _NT, preferred_element_type=F32) + bias_ref[0, 0]
    s_scr[:, n_win:] = lax.dot_general(q, kc_ref[...], _NT, preferred_element_type=F32)
    n_tiles = (n_win + n_ctx) // LANES
    for g in range(q.shape[0] // rg):
        rows_ = slice(g * rg, (g + 1) * rg)
        tiles = [s_scr[rows_, c * LANES:(c + 1) * LANES] for c in range(n_tiles)]
        mx = tiles[0]
        for st in tiles[1:]:
            mx = jnp.maximum(mx, st)
        m = jnp.max(mx, axis=-1, keepdims=True)
        psum = None
        for c, st in enumerate(tiles):
            p = jnp.exp2(st - m)
            p_scr[rows_, c * LANES:(c + 1) * LANES] = p.astype(p_scr.dtype)
            psum = p if psum is None else psum + p
        l_scr[rows_, :] = psum
    acc = (jnp.dot(p_scr[:, :n_win], vw, preferred_element_type=F32)
           + jnp.dot(p_scr[:, n_win:], vc_ref[...], preferred_element_type=F32))
    l = jnp.sum(l_scr[...], axis=-1, keepdims=True)
    o_ref[...] = (acc / l).astype(o_ref.dtype)


def na_attention_blocked(qkv, bias, o_ctx, *, batch, seq, ctx, heads):
    rows = seq // GRID_W
    nt = rows // NA_QROWS
    assert rows % NA_QROWS == 0 and rows >= NA_SPAN
    ctx_blk0 = (batch * seq) // ctx
    d = HEAD_DIM
    tq = NA_QROWS * GRID_W
    n_keys = NA_SPAN * GRID_W + ctx

    def kind(t):
        return jnp.where(t == 0, 0, jnp.where(t == nt - 1, 2, 1))

    return pl.pallas_call(
        functools.partial(_na_block_kernel, rows=rows, rg=FLASH_ROW_GROUP),
        grid=(batch, heads, nt),
        in_specs=[pl.BlockSpec((tq, d), lambda b, h, t: (b * nt + t, h)),
                  pl.BlockSpec((seq, d), lambda b, h, t: (b, heads + h)),
                  pl.BlockSpec((seq, d), lambda b, h, t: (b, 2 * heads + h)),
                  pl.BlockSpec((ctx, d), lambda b, h, t: (ctx_blk0 + b, heads + h)),
                  pl.BlockSpec((ctx, d), lambda b, h, t: (ctx_blk0 + b, 2 * heads + h)),
                  pl.BlockSpec((1, 1, tq, NA_SPAN * GRID_W), lambda b, h, t: (h, kind(t), 0, 0)),
                  pl.BlockSpec(memory_space=pl.ANY)],
        out_specs=pl.BlockSpec((tq, d), lambda b, h, t: (b * nt + t, h)),
        out_shape=jax.ShapeDtypeStruct(o_ctx.shape, o_ctx.dtype),
        input_output_aliases={6: 0},
        scratch_shapes=[pltpu.VMEM((tq, n_keys), F32), pltpu.VMEM((tq, n_keys), BF16),
                        pltpu.VMEM((tq, LANES), F32)],
        compiler_params=_cparams("arbitrary", "arbitrary", "arbitrary"),
        name="na_attention",
    )(qkv, qkv, qkv, qkv, qkv, bias, o_ctx)


def _na_bias_blocks(rpb, rows):
    h = rpb.shape[0]
    kr = NA_ROWS
    col = np.arange(GRID_W)
    col_start = np.clip(col - NA_COLS // 2, 0, GRID_W - NA_COLS)
    col_in = (col[None, :] >= col_start[:, None]) & (col[None, :] < col_start[:, None] + NA_COLS)
    dc = np.clip(col[None, :] - col[:, None] + NA_COLS - 1, 0, 2 * NA_COLS - 2)
    by_col = jnp.zeros((h, 2 * NA_ROWS - 1, GRID_W, GRID_W), F32)
    for cc in range(2 * NA_COLS - 1):
        by_col = jnp.where(jnp.asarray(dc == cc)[None, None], rpb[:, :, cc, None, None].astype(F32), by_col)
    starts = [(0, 0), (NA_QROWS, NA_QROWS - kr // 2), (rows - NA_QROWS, rows - NA_SPAN)]
    sel = np.zeros((3, NA_QROWS, NA_SPAN, 2 * NA_ROWS - 1), np.float32)
    row_ok = np.zeros((3, NA_QROWS, NA_SPAN), bool)
    for kind, (r0, k0) in enumerate(starts):
        for ri in range(NA_QROWS):
            r = r0 + ri
            rs = int(np.clip(r - kr // 2, 0, rows - kr))
            for kj in range(NA_SPAN):
                k_abs = k0 + kj
                if rs <= k_abs < rs + kr:
                    row_ok[kind, ri, kj] = True
                    sel[kind, ri, kj, k_abs - r + NA_ROWS - 1] = 1.0
    raw = jnp.einsum('trkd,hdqc->htrqkc', jnp.asarray(sel), by_col, precision=lax.Precision.HIGHEST) * LOG2E
    ok = jnp.asarray(row_ok[:, :, None, :, None] & col_in[None, None, :, None, :])
    bias = jnp.where(ok[None], raw, NEG_INF)
    return bias.reshape(h, 3, NA_QROWS * GRID_W, NA_SPAN * GRID_W)


def _pack_bf16_pairs(x):
    n = x.shape[1] // 2
    lo = lax.bitcast_convert_type(x[:, :n].astype(BF16).astype(F32), jnp.uint32)
    hi = lax.bitcast_convert_type(x[:, n:].astype(BF16).astype(F32), jnp.uint32)
    return (lo >> 16) | (hi & jnp.uint32(0xFFFF0000))


def _unpack_lo(words):
    return lax.bitcast_convert_type(words << 16, F32)


def _unpack_hi(words):
    return lax.bitcast_convert_type(words & jnp.uint32(0xFFFF0000), F32)


def _proj_router_kernel(o_ref, wo_ref, x_ref, gate_ref, g_ref, sh_ref, sc_ref, w_ref, b_ref,
                        xo_ref, h_ref, route_ref, counts_ref, carry_scr, *, n_groups, epg):
    @pl.when(pl.program_id(0) == 0)
    def _():
        carry_scr[...] = jnp.zeros(carry_scr.shape, F32)

    x_new = x_ref[...] + gate_ref[0] * jnp.dot(o_ref[...], wo_ref[...], preferred_element_type=F32)
    xo_ref[...] = x_new
    h = _rms(x_new, g_ref[...]) * (1.0 + sc_ref[0]) + sh_ref[0]
    h_ref[...] = _pack_bf16_pairs(h)
    w = w_ref[...]
    h_hi = h.astype(BF16)
    h_lo = (h - h_hi.astype(F32)).astype(BF16)
    w_hi = w.astype(BF16)
    w_lo = (w - w_hi.astype(F32)).astype(BF16)
    logits = (jnp.dot(h_hi, w_hi, preferred_element_type=F32)
              + jnp.dot(h_lo, w_hi, preferred_element_type=F32)
              + jnp.dot(h_hi, w_lo, preferred_element_type=F32)) + b_ref[...]
    lane = lax.broadcasted_iota(jnp.int32, logits.shape, 1).astype(F32)
    is_g = lane < n_groups
    g_max = jnp.max(jnp.where(is_g, logits, -jnp.inf), axis=-1, keepdims=True)
    g_sum = jnp.sum(jnp.where(is_g, jnp.exp(logits - g_max), 0.0), axis=-1, keepdims=True)
    g_sel = jnp.min(jnp.where(is_g & (logits == g_max), lane, float(LANES)), axis=-1, keepdims=True)
    lo = n_groups + epg * g_sel
    in_grp = (lane >= lo) & (lane < lo + epg)
    t1 = jnp.max(jnp.where(in_grp, logits, -jnp.inf), axis=-1, keepdims=True)
    i1 = jnp.min(jnp.where(in_grp & (logits == t1), lane, float(LANES)), axis=-1, keepdims=True)
    rest = in_grp & (lane != i1)
    t2 = jnp.max(jnp.where(rest, logits, -jnp.inf), axis=-1, keepdims=True)
    i2 = jnp.min(jnp.where(rest & (logits == t2), lane, float(LANES)), axis=-1, keepdims=True)
    d = jnp.exp(t2 - t1)
    gate = 1.0 / g_sum
    w1 = gate * (1.0 / (1.0 + d))
    w2 = gate * (d / (1.0 + d))
    tm = logits.shape[0]
    chosen = jnp.where((lane == i1) | (lane == i2), 1.0, 0.0)
    earlier = jnp.where(lax.broadcasted_iota(jnp.int32, (tm, tm), 1) < lax.broadcasted_iota(jnp.int32, (tm, tm), 0),
                        1.0, 0.0).astype(BF16)
    before = jnp.dot(earlier, chosen.astype(BF16), preferred_element_type=F32) + carry_scr[...]
    rank1 = jnp.sum(jnp.where(lane == i1, before, 0.0), axis=-1, keepdims=True)
    rank2 = jnp.sum(jnp.where(lane == i2, before, 0.0), axis=-1, keepdims=True)
    total = carry_scr[...] + jnp.sum(chosen, axis=0, keepdims=True)
    carry_scr[...] = total
    counts_ref[...] = jnp.broadcast_to(total, counts_ref.shape)
    vals = (i1 - n_groups, i2 - n_groups, w1, w2, rank1, rank2)
    route = jnp.zeros(logits.shape, F32)
    for pos, v in enumerate(vals):
        route = jnp.where(lane == pos, v, route)
    route_ref[...] = route


def proj_residual_router(o, wo_bf, xs, mods, layer, norm_g, w_r, b_r, *, tm, seq, batch, n_groups, epg):
    t, d = xs.shape
    k = o.shape[1]
    gate_spec, sh_spec, sc_spec = _mod_specs(layer, (2, 3, 4), tm, seq, batch, d)
    return pl.pallas_call(
        functools.partial(_proj_router_kernel, n_groups=n_groups, epg=epg),
        grid=(t // tm, 1),
        in_specs=[pl.BlockSpec((tm, k), lambda i, j: (i, 0)),
                  pl.BlockSpec((k, d), lambda i, j: (0, 0), pipeline_mode=pl.Buffered(1)),
                  pl.BlockSpec((tm, d), lambda i, j: (i, 0)),
                  gate_spec,
                  pl.BlockSpec((1, d), lambda i, j: (0, 0)),
                  sh_spec, sc_spec,
                  pl.BlockSpec((d, LANES), lambda i, j: (0, 0)),
                  pl.BlockSpec((1, LANES), lambda i, j: (0, 0))],
        out_specs=[pl.BlockSpec((tm, d), lambda i, j: (i, 0)),
                   pl.BlockSpec((tm, d // 2), lambda i, j: (i, 0)),
                   pl.BlockSpec((tm, LANES), lambda i, j: (i, 0)),
                   pl.BlockSpec((8, LANES), lambda i, j: (0, 0))],
        out_shape=[jax.ShapeDtypeStruct((t, d), F32), jax.ShapeDtypeStruct((t, d // 2), jnp.uint32),
                   jax.ShapeDtypeStruct((t, LANES), F32), jax.ShapeDtypeStruct((8, LANES), F32)],
        scratch_shapes=[pltpu.VMEM((1, LANES), F32)],
        compiler_params=_cparams("arbitrary", "arbitrary"),
        name="proj_residual_router",
    )(o, wo_bf, xs, mods, norm_g.reshape(1, d), mods, mods, w_r, b_r)


def _issue_row_gather(idx_ref, src_hbm, dst, sem, n_rows):
    def body(r, carry):
        tok = idx_ref[0, 0, r]
        pltpu.make_async_copy(src_hbm.at[pl.ds(tok, 1)], dst.at[pl.ds(r, 1)], sem).start()
        return carry
    lax.fori_loop(0, n_rows, body, 0, unroll=8)


def _wait_row_gather(src_hbm, dst, sem, n_rows):
    pltpu.make_async_copy(src_hbm.at[pl.ds(0, n_rows)], dst, sem).wait()


def _expert_kernel(be_ref, idx_ref, idx_next_ref, h_hbm, win_ref, wout_ref, y_ref,
                   xbuf_a, xbuf_b, sem, win_bf, wout_bf, *, nb, cast_rows, k_chunk):
    i = pl.program_id(0)
    blk, half = xbuf_a.shape
    n_k = 2 * half // k_chunk
    rows_per_chunk = blk // n_k

    @pl.when(i == 0)
    def _():
        _issue_row_gather(idx_ref, h_hbm, xbuf_a, sem.at[0], blk)

    expert_changed = (i == 0) | (be_ref[i] != be_ref[jnp.maximum(i - 1, 0)])

    @pl.when(expert_changed)
    def _():
        def cast_in(c, carry):
            r0 = pl.multiple_of(c * cast_rows, cast_rows)
            win_bf[pl.ds(r0, cast_rows), :] = win_ref[0, 0, pl.ds(r0, cast_rows), :].astype(BF16)
            return carry
        lax.fori_loop(0, win_bf.shape[0] // cast_rows, cast_in, 0)

        def cast_out(c, carry):
            r0 = pl.multiple_of(c * cast_rows, cast_rows)
            wout_bf[pl.ds(r0, cast_rows), :] = wout_ref[0, 0, pl.ds(r0, cast_rows), :].astype(BF16)
            return carry
        lax.fori_loop(0, wout_bf.shape[0] // cast_rows, cast_out, 0)

    def run(x_cur, sem_cur, x_next, sem_next):
        _wait_row_gather(h_hbm, x_cur, sem_cur, blk)
        gu = None
        for kk in range(n_k):
            for r in range(kk * rows_per_chunk, (kk + 1) * rows_per_chunk):
                pltpu.make_async_copy(h_hbm.at[pl.ds(idx_next_ref[0, 0, r], 1)], x_next.at[pl.ds(r, 1)],
                                      sem_next).start()
            f0 = kk * k_chunk
            words = x_cur[:, f0 % half:f0 % half + k_chunk]
            xk = _unpack_lo(words) if f0 < half else _unpack_hi(words)
            part = jnp.dot(xk.astype(BF16), win_bf[f0:f0 + k_chunk, :], preferred_element_type=F32)
            gu = part if gu is None else gu + part
        f = gu.shape[1] // 2
        act = _silu(gu[:, :f]) * gu[:, f:]
        y_ref[...] = _pack_bf16_pairs(jnp.dot(act.astype(BF16), wout_bf[...], preferred_element_type=F32))

        @pl.when(i == nb - 1)
        def _():
            _wait_row_gather(h_hbm, x_next, sem_next, blk)

    @pl.when(i % 2 == 0)
    def _():
        run(xbuf_a, sem.at[0], xbuf_b, sem.at[1])

    @pl.when(i % 2 == 1)
    def _():
        run(xbuf_b, sem.at[1], xbuf_a, sem.at[0])


def moe_experts(h, slot_tok, block_expert, w_in, w_out, layer):
    half = h.shape[1]
    d = 2 * half
    f2 = w_in.shape[-1]
    f = f2 // 2
    nb = block_expert.shape[0]
    blk = slot_tok.shape[0] // nb
    idx = slot_tok.reshape(nb, 1, blk)
    grid_spec = pltpu.PrefetchScalarGridSpec(
        num_scalar_prefetch=1,
        grid=(nb,),
        in_specs=[pl.BlockSpec((1, 1, blk), lambda i, be: (i, 0, 0), memory_space=pltpu.SMEM),
                  pl.BlockSpec((1, 1, blk), lambda i, be: (jnp.minimum(i + 1, nb - 1), 0, 0),
                               memory_space=pltpu.SMEM),
                  pl.BlockSpec(memory_space=pl.ANY),
                  pl.BlockSpec((1, 1, d, f2), lambda i, be: (layer, be[i], 0, 0)),
                  pl.BlockSpec((1, 1, f, d), lambda i, be: (layer, be[i], 0, 0))],
        out_specs=pl.BlockSpec((blk, half), lambda i, be: (i, 0)),
        scratch_shapes=[pltpu.VMEM((blk, half), jnp.uint32),
                        pltpu.VMEM((blk, half), jnp.uint32),
                        pltpu.SemaphoreType.DMA((2,)),
                        pltpu.VMEM((d, f2), BF16),
                        pltpu.VMEM((f, d), BF16)],
    )
    return pl.pallas_call(
        functools.partial(_expert_kernel, nb=nb, cast_rows=256, k_chunk=256),
        grid_spec=grid_spec,
        out_shape=jax.ShapeDtypeStruct((nb * blk, half), jnp.uint32),
        compiler_params=_cparams("arbitrary"),
        name="moe_experts",
    )(block_expert, idx, idx, h, w_in, w_out)


def _gather_rows_kernel(idx_ref, src_hbm, dst_hbm, sem, *, nb, blk):
    i = pl.program_id(0)

    def wait_block(slot):
        pltpu.make_async_copy(src_hbm.at[pl.ds(0, blk)], dst_hbm.at[pl.ds(0, blk)], sem.at[slot]).wait()

    def issue(slot):
        base = i * blk

        def body(r, carry):
            pltpu.make_async_copy(src_hbm.at[pl.ds(idx_ref[0, 0, r], 1)], dst_hbm.at[pl.ds(base + r, 1)],
                                  sem.at[slot]).start()
            return carry
        lax.fori_loop(0, blk, body, 0, unroll=8)

    @pl.when(i % 2 == 0)
    def _():
        issue(0)

    @pl.when(i % 2 == 1)
    def _():
        issue(1)

    @pl.when((i > 0) & (i % 2 == 1))
    def _():
        wait_block(0)

    @pl.when((i > 0) & (i % 2 == 0))
    def _():
        wait_block(1)

    @pl.when(i == nb - 1)
    def _():
        wait_block((nb - 1) % 2)


def gather_rows(src, slot_src_row, blk):
    n_slots = slot_src_row.shape[0]
    nb = n_slots // blk
    return pl.pallas_call(
        functools.partial(_gather_rows_kernel, nb=nb, blk=blk),
        grid=(nb,),
        in_specs=[pl.BlockSpec((1, 1, blk), lambda i: (i, 0, 0), memory_space=pltpu.SMEM),
                  pl.BlockSpec(memory_space=pl.ANY)],
        out_specs=pl.BlockSpec(memory_space=pl.ANY),
        out_shape=jax.ShapeDtypeStruct((n_slots, src.shape[1]), src.dtype),
        scratch_shapes=[pltpu.SemaphoreType.DMA((2,))],
        compiler_params=_cparams("arbitrary"),
        name="gather_rows",
    )(slot_src_row.reshape(nb, 1, blk), src)


def _expert_dense_kernel(be_ref, x_ref, win_ref, wout_ref, y_ref, win_bf, wout_bf, *, cast_rows, k_chunk):
    i = pl.program_id(0)
    half = x_ref.shape[1]
    expert_changed = (i == 0) | (be_ref[i] != be_ref[jnp.maximum(i - 1, 0)])

    @pl.when(expert_changed)
    def _():
        def cast_in(c, carry):
            r0 = pl.multiple_of(c * cast_rows, cast_rows)
            win_bf[pl.ds(r0, cast_rows), :] = win_ref[0, 0, pl.ds(r0, cast_rows), :].astype(BF16)
            return carry
        lax.fori_loop(0, win_bf.shape[0] // cast_rows, cast_in, 0)

        def cast_out(c, carry):
            r0 = pl.multiple_of(c * cast_rows, cast_rows)
            wout_bf[pl.ds(r0, cast_rows), :] = wout_ref[0, 0, pl.ds(r0, cast_rows), :].astype(BF16)
            return carry
        lax.fori_loop(0, wout_bf.shape[0] // cast_rows, cast_out, 0)

    gu = None
    for f0 in range(0, 2 * half, k_chunk):
        words = x_ref[:, f0 % half:f0 % half + k_chunk]
        xk = _unpack_lo(words) if f0 < half else _unpack_hi(words)
        part = jnp.dot(xk.astype(BF16), win_bf[f0:f0 + k_chunk, :], preferred_element_type=F32)
        gu = part if gu is None else gu + part
    f = gu.shape[1] // 2
    act = _silu(gu[:, :f]) * gu[:, f:]
    y_ref[...] = _pack_bf16_pairs(jnp.dot(act.astype(BF16), wout_bf[...], preferred_element_type=F32))


def moe_experts_dense(hs, block_expert, w_in, w_out, layer):
    half = hs.shape[1]
    d = 2 * half
    f2 = w_in.shape[-1]
    f = f2 // 2
    nb = block_expert.shape[0]
    blk = hs.shape[0] // nb
    grid_spec = pltpu.PrefetchScalarGridSpec(
        num_scalar_prefetch=1,
        grid=(nb,),
        in_specs=[pl.BlockSpec((blk, half), lambda i, be: (i, 0)),
                  pl.BlockSpec((1, 1, d, f2), lambda i, be: (layer, be[i], 0, 0)),
                  pl.BlockSpec((1, 1, f, d), lambda i, be: (layer, be[i], 0, 0))],
        out_specs=pl.BlockSpec((blk, half), lambda i, be: (i, 0)),
        scratch_shapes=[pltpu.VMEM((d, f2), BF16), pltpu.VMEM((f, d), BF16)],
    )
    return pl.pallas_call(
        functools.partial(_expert_dense_kernel, cast_rows=256, k_chunk=256),
        grid_spec=grid_spec,
        out_shape=jax.ShapeDtypeStruct((nb * blk, half), jnp.uint32),
        compiler_params=_cparams("arbitrary"),
        name="moe_experts_dense",
    )(block_expert, hs, w_in, w_out)


def _combine_kernel(pos_ref, pos_next_ref, yb_hbm, x_ref, route_ref, gate_ref, o_ref, ybuf, sem, *, nt):
    i = pl.program_id(0)
    slot = i % 2
    n_rows = ybuf.shape[1]
    tm = n_rows // MOE_TOP_K

    @pl.when(i == 0)
    def _():
        _issue_row_gather(pos_ref, yb_hbm, ybuf.at[0], sem.at[0], n_rows)

    @pl.when(i + 1 < nt)
    def _():
        _issue_row_gather(pos_next_ref, yb_hbm, ybuf.at[1 - slot], sem.at[1 - slot], n_rows)

    _wait_row_gather(yb_hbm, ybuf.at[slot], sem.at[slot], n_rows)
    route = route_ref[...]
    w1, w2 = route[:, 2:3], route[:, 3:4]
    y1, y2 = ybuf[slot, :tm, :], ybuf[slot, tm:, :]
    half = y1.shape[1]
    gate = gate_ref[0]
    o_ref[:, :half] = x_ref[:, :half] + gate[:, :half] * (w1 * _unpack_lo(y1) + w2 * _unpack_lo(y2))
    o_ref[:, half:] = x_ref[:, half:] + gate[:, half:] * (w1 * _unpack_hi(y1) + w2 * _unpack_hi(y2))


def moe_combine(yb, pos, xs, route, mods, layer, *, tm, seq, batch, out_rows):
    t, d = xs.shape
    pos_t = pos.reshape(t // tm, tm, MOE_TOP_K).transpose(0, 2, 1).reshape(t // tm, 1, MOE_TOP_K * tm)
    nt = out_rows // tm

    def seg(i):
        return jnp.minimum((i * tm) // seq, batch)

    return pl.pallas_call(
        functools.partial(_combine_kernel, nt=nt),
        grid=(nt,),
        in_specs=[pl.BlockSpec((1, 1, MOE_TOP_K * tm), lambda i: (i, 0, 0), memory_space=pltpu.SMEM),
                  pl.BlockSpec((1, 1, MOE_TOP_K * tm), lambda i: (jnp.minimum(i + 1, nt - 1), 0, 0),
                               memory_space=pltpu.SMEM),
                  pl.BlockSpec(memory_space=pl.ANY),
                  pl.BlockSpec((tm, d), lambda i: (i, 0)),
                  pl.BlockSpec((tm, LANES), lambda i: (i, 0)),
                  pl.BlockSpec((1, 1, d), lambda i: ((layer * MOD_ROWS + seg(i)) * 6 + 5, 0, 0))],
        out_specs=pl.BlockSpec((tm, d), lambda i: (i, 0)),
        out_shape=jax.ShapeDtypeStruct((out_rows, d), F32),
        scratch_shapes=[pltpu.VMEM((2, MOE_TOP_K * tm, d // 2), jnp.uint32), pltpu.SemaphoreType.DMA((2,))],
        compiler_params=_cparams("arbitrary"),
        name="moe_combine",
    )(pos_t, pos_t, yb, xs, route, mods)


def _dispatch_plan(route, counts, n_experts, blk):
    t = route.shape[0]
    nk = t * MOE_TOP_K
    padded = (counts + blk - 1) // blk * blk
    pad_end = jnp.cumsum(padded)
    pad_start = pad_end - padded
    nb = -(-(nk + n_experts * (blk - 1)) // blk)
    e = route[:, :MOE_TOP_K].astype(jnp.int32)
    rank = route[:, 2 * MOE_TOP_K:3 * MOE_TOP_K].astype(jnp.int32)
    onehot = e[..., None] == jnp.arange(n_experts, dtype=jnp.int32)
    dest = jnp.sum(jnp.where(onehot, pad_start, 0), axis=-1) + rank
    tok = jnp.arange(nk, dtype=jnp.int32) // MOE_TOP_K
    slot_tok = jnp.zeros((nb * blk,), jnp.int32).at[dest.reshape(nk)].set(tok, unique_indices=True)
    first_row = jnp.arange(nb, dtype=jnp.int32) * blk
    block_expert = jnp.minimum(jnp.sum(pad_end[None, :] <= first_row[:, None], axis=1), n_experts - 1)
    return slot_tok, block_expert.astype(jnp.int32), dest


def _axial_angles(seq, dim):
    n_freq = dim // 4
    freqs = ROPE_THETA ** (-jnp.arange(n_freq, dtype=F32) / n_freq)
    tok = jnp.arange(seq, dtype=jnp.int32)
    row = (tok // GRID_W).astype(F32)
    col = (tok % GRID_W).astype(F32)
    return jnp.concatenate([row[:, None] * freqs, col[:, None] * freqs], axis=-1)


def _stream_table(lat, ctx_fill, batch, n_ctx_rows):
    ctx_rows = jnp.broadcast_to(ctx_fill[None, :], (n_ctx_rows, lat.shape[1]))
    return jnp.concatenate([jnp.tile(lat, (batch, 1)), ctx_rows], axis=0)


def _rope_tables_128(seq, batch, n_ctx_rows):
    ang = _axial_angles(seq, HEAD_DIM)
    cos, sin = jnp.cos(ang), jnp.sin(ang)
    c = jnp.concatenate([cos, cos], axis=-1)
    s = jnp.concatenate([-sin, sin], axis=-1)
    return (_stream_table(c, jnp.ones((HEAD_DIM,), F32), batch, n_ctx_rows),
            _stream_table(s, jnp.zeros((HEAD_DIM,), F32), batch, n_ctx_rows))


def _rope_tables_64(seq, batch, n_ctx_rows):
    ang = _axial_angles(seq, MLA_ROPE)
    cos, sin = jnp.cos(ang), jnp.sin(ang)
    half = MLA_ROPE // 2
    z = jnp.zeros((seq, LANES - MLA_ROPE), F32)
    zh = jnp.zeros((seq, half), F32)
    c = jnp.concatenate([cos, cos, z], axis=-1)
    s1 = jnp.concatenate([-sin, zh, z], axis=-1)
    s2 = jnp.concatenate([zh, sin, z], axis=-1)
    ones_pad = jnp.concatenate([jnp.ones((MLA_ROPE,), F32), jnp.zeros((LANES - MLA_ROPE,), F32)])
    zeros = jnp.zeros((LANES,), F32)
    return (_stream_table(c, ones_pad, batch, n_ctx_rows),
            _stream_table(s1, zeros, batch, n_ctx_rows),
            _stream_table(s2, zeros, batch, n_ctx_rows))


def _na_bias_table(rpb, kr):
    col = jnp.arange(GRID_W, dtype=jnp.int32)
    col_start = jnp.clip(col - NA_COLS // 2, 0, GRID_W - NA_COLS)
    col_in = (col[None, :] >= col_start[:, None]) & (col[None, :] < col_start[:, None] + NA_COLS)
    dc = jnp.clip(col[None, :] - col[:, None] + NA_COLS - 1, 0, 2 * NA_COLS - 2)
    h = rpb.shape[0]
    by_col = jnp.zeros((h, 2 * NA_ROWS - 1, GRID_W, GRID_W), F32)
    for cc in range(2 * NA_COLS - 1):
        by_col = jnp.where(dc[None, None] == cc, rpb[:, :, cc, None, None].astype(F32), by_col)
    by_col = jnp.where(col_in[None, None], by_col, NEG_INF)
    per_shift = [by_col[:, NA_ROWS - 1 - s:NA_ROWS - 1 - s + kr] for s in range(kr)]
    bias = jnp.stack(per_shift, axis=1)
    return bias.transpose(0, 1, 3, 2, 4).reshape(h, kr, GRID_W, kr * GRID_W)


def _pick_tile(*extents):
    for tm in (512, 256, 128):
        if all(e % tm == 0 for e in extents):
            return tm
    raise ValueError("row extents must be multiples of 128")


def kernel(x, c, ctx, c_ctx, mod_w, mod_b, norm_g, gqa_w_qkv, gqa_q_norm, gqa_k_norm, gqa_w_o, na_w_qkv, na_q_norm, na_k_norm, na_rpb, na_w_o, mla_w_down, mla_q_lora_norm, mla_w_uq, mla_kv_lora_norm, mla_w_ukv, mla_q_norm, mla_k_norm, mla_w_o, moe_w_group, moe_b_group, moe_w_expert, moe_b_expert, moe_w_in, moe_w_out):
    batch, seq, d = x.shape
    n_ctx = ctx.shape[1]
    depth = mod_w.shape[0]
    heads = d // HEAD_DIM
    kv_heads = heads // 4
    n_lat = batch * seq
    n_ctx_rows = batch * n_ctx
    assert batch + 1 <= MOD_ROWS and n_lat % n_ctx == 0 and seq % GRID_W == 0
    tm = _pick_tile(seq, n_ctx_rows)
    tq = min(FLASH_TQ, seq)
    tk = min(FLASH_TK, seq)
    assert seq % tq == 0 and seq % (2 * tk) == 0
    tn_d = min(512, d)
    dims = dict(tm=tm, seq=seq, batch=batch)

    xs = jnp.concatenate([x.reshape(n_lat, d), ctx.reshape(n_ctx_rows, d)], axis=0)

    cond = jnp.zeros((MOD_ROWS, d), F32).at[:batch].set(c).at[batch].set(c_ctx)
    mods = adaln_all(cond, mod_w, mod_b).reshape(depth * MOD_ROWS * 6, 1, d)

    cos_a, sin_a = _rope_tables_128(seq, batch, n_ctx_rows)
    c64, s64a, s64b = _rope_tables_64(seq, batch, n_ctx_rows)

    n_groups = moe_w_group.shape[-1]
    epg = moe_w_expert.shape[-1]
    n_experts = n_groups * epg
    assert n_groups + n_experts <= LANES

    for i in range(depth):
        kind, j = i % 3, i // 3
        if kind == 0:
            scale = HEAD_DIM ** -0.5 * LOG2E
            nq, nkv = heads * HEAD_DIM, kv_heads * HEAD_DIM
            gain = jnp.concatenate([jnp.tile(gqa_q_norm[j] * scale, heads), jnp.tile(gqa_k_norm[j], kv_heads),
                                    jnp.ones((nkv,), F32)]).reshape(1, -1)
            qkv = qkv_project(xs, mods, i, norm_g[i, 0], gqa_w_qkv[j].astype(BF16), gain, cos_a, sin_a,
                              n_norm_cols=nq + nkv, tn=min(512, nkv), rope=True, **dims)
            cols = dict(dk=HEAD_DIM, q_col0=0, k_col0=heads, v_col0=heads + kv_heads, kv_group=heads // kv_heads)
            o_ctx = ctx_attention(qkv, qkv, qkv, batch=batch, seq=seq, ctx=n_ctx, heads=heads, **cols)
            o = dense_attention(qkv, qkv, qkv, o_ctx, batch=batch, seq=seq, ctx=n_ctx, heads=heads, tq=tq,
                                tk=tk if j == 0 else min(2 * tk, seq // 2), **cols)
            w_o = gqa_w_o[j]
        elif kind == 1:
            scale = HEAD_DIM ** -0.5 * LOG2E
            nq = heads * HEAD_DIM
            gain = jnp.concatenate([jnp.tile(na_q_norm[j] * scale, heads), jnp.tile(na_k_norm[j], heads),
                                    jnp.ones((nq,), F32)]).reshape(1, -1)
            qkv = qkv_project(xs, mods, i, norm_g[i, 0], na_w_qkv[j].astype(BF16), gain, cos_a, sin_a,
                              n_norm_cols=2 * nq, tn=tn_d, rope=False, **dims)
            bias = _na_bias_blocks(na_rpb[j], seq // GRID_W)
            o_ctx = ctx_attention(qkv, qkv, qkv, batch=batch, seq=seq, ctx=n_ctx, heads=heads, dk=HEAD_DIM,
                                  q_col0=0, k_col0=heads, v_col0=2 * heads, kv_group=1)
            o = na_attention_blocked(qkv, bias, o_ctx, batch=batch, seq=seq, ctx=n_ctx, heads=heads)
            w_o = na_w_o[j]
        else:
            scale = MLA_QK ** -0.5 * LOG2E
            q_lora = mla_q_lora_norm.shape[-1]
            kv_lora = mla_kv_lora_norm.shape[-1]
            assert q_lora % kv_lora == 0 and (q_lora + kv_lora) % LANES == 0
            n_down = q_lora + kv_lora + LANES
            w_down = jnp.pad(mla_w_down[j], ((0, 0), (0, n_down - mla_w_down.shape[-1]))).astype(BF16)
            cfull = mla_down(xs, mods, i, norm_g[i, 0], w_down, **dims)
            pad = MLA_HEAD_PAD - MLA_QK
            w_uq = jnp.pad(mla_w_uq[j].reshape(q_lora, heads, MLA_QK), ((0, 0), (0, 0), (0, pad)))
            w_uq = w_uq.reshape(q_lora, heads * MLA_HEAD_PAD).astype(BF16)
            q_gain = jnp.tile(jnp.pad(mla_q_norm[j] * scale, (0, pad)), heads).reshape(1, -1)
            qa = mla_q_project(cfull, mla_q_lora_norm[j], w_uq, q_gain, c64, s64a, s64b, tm=tm)
            gain_n = mla_k_norm[j, :MLA_NOPE].reshape(1, LANES)
            gain_t = jnp.pad(mla_k_norm[j, MLA_NOPE:], (0, LANES - MLA_ROPE)).reshape(1, LANES)
            ka, va = mla_kv_project(cfull, mla_kv_lora_norm[j], mla_w_ukv[j].astype(BF16), gain_n, gain_t,
                                    c64, s64a, s64b, tm=tm, q_lora=q_lora)
            cols = dict(dk=MLA_HEAD_PAD, q_col0=0, k_col0=0, v_col0=0, kv_group=1)
            o_ctx = ctx_attention(qa, ka, va, batch=batch, seq=seq, ctx=n_ctx, heads=heads, **cols)
            o = dense_attention(qa, ka, va, o_ctx, batch=batch, seq=seq, ctx=n_ctx, heads=heads, tq=tq, tk=tk, **cols)
            w_o = mla_w_o[j]

        w_r = jnp.concatenate([moe_w_group[i], moe_w_expert[i].transpose(1, 0, 2).reshape(d, n_experts)], axis=1)
        w_r = jnp.pad(w_r, ((0, 0), (0, LANES - w_r.shape[1])))
        b_r = jnp.pad(jnp.concatenate([moe_b_group[i], moe_b_expert[i].reshape(-1)]),
                      (0, LANES - n_groups - n_experts)).reshape(1, LANES)
        xs, h2, route, lane_counts = proj_residual_router(o, w_o.astype(BF16), xs, mods, i, norm_g[i, 1], w_r, b_r,
                                                          n_groups=n_groups, epg=epg, **dims)
        counts = lane_counts[0, n_groups:n_groups + n_experts].astype(jnp.int32)
        slot_tok, block_expert, pos = _dispatch_plan(route, counts, n_experts, MOE_BLOCK)
        if i < 2:
            yb = moe_experts(h2, slot_tok, block_expert, moe_w_in, moe_w_out, i)
        else:
            yb = moe_experts_dense(gather_rows(h2, slot_tok, MOE_BLOCK), block_expert, moe_w_in, moe_w_out, i)
        out_rows = n_lat if i == depth - 1 else n_lat + n_ctx_rows
        xs = moe_combine(yb, pos, xs, route, mods, i, tm=min(COMBINE_TM, tm), seq=seq, batch=batch,
                         out_rows=out_rows)

    return xs.reshape(batch, seq, d)
```

```python
import functools

import jax
import jax.numpy as jnp
from jax import lax
from jax.experimental import pallas as pl
from jax.experimental.pallas import tpu as pltpu

F32 = jnp.float32
BF16 = jnp.bfloat16

GRID_W = 64
HEAD_DIM = 128
ROPE_THETA = 10000.0
NORM_EPS = 1e-6
NEG_INF = -1e30
NA_ROWS = 8
NA_COLS = 16
MLA_NOPE = 128
MLA_ROPE = 64
MLA_V = 128
MLA_QK = MLA_NOPE + MLA_ROPE
MLA_HEAD_PAD = 256
MOE_TOP_K = 2
LANES = 128
MOD_ROWS = 8
MOE_BLOCK = 256
COMBINE_TM = 256
FLASH_TQ = 1024
FLASH_TK = 1024
FLASH_ROW_GROUP = 32
LOG2E = 1.4426950408889634
VMEM_LIMIT = 52 * 1024 * 1024


def _cparams(*sem):
    return pltpu.CompilerParams(dimension_semantics=sem, vmem_limit_bytes=VMEM_LIMIT)


def _silu(v):
    return v / (1.0 + jnp.exp(-v))


def _adaln_kernel(c_ref, w_ref, b_ref, o_ref):
    cond = _silu(c_ref[...]).astype(BF16)
    o_ref[0] = jnp.dot(cond, w_ref[0].astype(BF16), preferred_element_type=F32) + b_ref[0]


def adaln_all(cond, mod_w, mod_b):
    depth, d, n = mod_w.shape
    tn = 1024
    return pl.pallas_call(
        _adaln_kernel,
        grid=(depth, n // tn),
        in_specs=[pl.BlockSpec((MOD_ROWS, d), lambda l, j: (0, 0)),
                  pl.BlockSpec((1, d, tn), lambda l, j: (l, 0, j)),
                  pl.BlockSpec((1, 1, tn), lambda l, j: (l, 0, j))],
        out_specs=pl.BlockSpec((1, MOD_ROWS, tn), lambda l, j: (l, 0, j)),
        out_shape=jax.ShapeDtypeStruct((depth, MOD_ROWS, n), F32),
        compiler_params=_cparams("arbitrary", "arbitrary"),
        name="adaln",
    )(cond, mod_w, mod_b.reshape(depth, 1, n))


def _rms(x, g):
    ms = jnp.mean(x * x, axis=-1, keepdims=True)
    return x * lax.rsqrt(ms + NORM_EPS) * g


def _rope128(y, c, s):
    return y * c + pltpu.roll(y, 64, 1) * s


def _rope64(y, c, s1, s2):
    return y * c + pltpu.roll(y, 96, 1) * s1 + pltpu.roll(y, 32, 1) * s2


def _qkv_kernel(x_ref, g_ref, sh_ref, sc_ref, w_ref, gain_ref, c_ref, s_ref, o_ref, *, tn, n_norm_cols, rope):
    h = (_rms(x_ref[...], g_ref[...]) * (1.0 + sc_ref[0]) + sh_ref[0]).astype(BF16)
    for c0 in range(0, w_ref.shape[1], tn):
        y = jnp.dot(h, w_ref[:, c0:c0 + tn], preferred_element_type=F32)
        if c0 >= n_norm_cols:
            o_ref[:, c0:c0 + tn] = y.astype(o_ref.dtype)
            continue
        for hh in range(tn // HEAD_DIM):
            sl = slice(c0 + hh * HEAD_DIM, c0 + (hh + 1) * HEAD_DIM)
            yh = _rms(y[:, hh * HEAD_DIM:(hh + 1) * HEAD_DIM], gain_ref[:, sl])
            if rope:
                yh = _rope128(yh, c_ref[...], s_ref[...])
            o_ref[:, sl] = yh.astype(o_ref.dtype)


def _mod_specs(layer, which, tm, seq, batch, d):
    def seg(i):
        return jnp.minimum((i * tm) // seq, batch)
    return [pl.BlockSpec((1, 1, d), lambda i, j, w=w: ((layer * MOD_ROWS + seg(i)) * 6 + w, 0, 0)) for w in which]


def qkv_project(xs, mods, layer, norm_g, w_bf, gain, cos_t, sin_t, *, tm, seq, batch, n_norm_cols, tn, rope):
    t, d = xs.shape
    n = w_bf.shape[1]
    sh_spec, sc_spec = _mod_specs(layer, (0, 1), tm, seq, batch, d)
    assert n_norm_cols % tn == 0 and n % tn == 0
    kern = functools.partial(_qkv_kernel, tn=tn, n_norm_cols=n_norm_cols, rope=rope)
    return pl.pallas_call(
        kern,
        grid=(t // tm, 1),
        in_specs=[pl.BlockSpec((tm, d), lambda i, j: (i, 0)),
                  pl.BlockSpec((1, d), lambda i, j: (0, 0)),
                  sh_spec, sc_spec,
                  pl.BlockSpec((d, n), lambda i, j: (0, 0), pipeline_mode=pl.Buffered(1)),
                  pl.BlockSpec((1, n), lambda i, j: (0, 0)),
                  pl.BlockSpec((tm, HEAD_DIM), lambda i, j: (i, 0)),
                  pl.BlockSpec((tm, HEAD_DIM), lambda i, j: (i, 0))],
        out_specs=pl.BlockSpec((tm, n), lambda i, j: (i, 0)),
        out_shape=jax.ShapeDtypeStruct((t, n), BF16),
        compiler_params=_cparams("arbitrary", "arbitrary"),
        name="qkv_project",
    )(xs, norm_g.reshape(1, d), mods, mods, w_bf, gain, cos_t, sin_t)


def _down_kernel(x_ref, g_ref, sh_ref, sc_ref, w_ref, o_ref):
    h = _rms(x_ref[...], g_ref[...]) * (1.0 + sc_ref[0]) + sh_ref[0]
    o_ref[...] = jnp.dot(h.astype(BF16), w_ref[...], preferred_element_type=F32)


def mla_down(xs, mods, layer, norm_g, w_bf, *, tm, seq, batch):
    t, d = xs.shape
    n = w_bf.shape[1]
    sh_spec, sc_spec = _mod_specs(layer, (0, 1), tm, seq, batch, d)
    return pl.pallas_call(
        _down_kernel,
        grid=(t // tm, 1),
        in_specs=[pl.BlockSpec((tm, d), lambda i, j: (i, 0)),
                  pl.BlockSpec((1, d), lambda i, j: (0, 0)),
                  sh_spec, sc_spec,
                  pl.BlockSpec((d, n), lambda i, j: (0, 0))],
        out_specs=pl.BlockSpec((tm, n), lambda i, j: (i, 0)),
        out_shape=jax.ShapeDtypeStruct((t, n), F32),
        compiler_params=_cparams("arbitrary", "arbitrary"),
        name="mla_down",
    )(xs, norm_g.reshape(1, d), mods, mods, w_bf)


def _mla_q_kernel(cq_ref, g_ref, w_ref, gain_ref, c_ref, s1_ref, s2_ref, o_ref):
    h = _rms(cq_ref[...], g_ref[...]).astype(BF16)
    for hh in range(w_ref.shape[1] // MLA_HEAD_PAD):
        lo = hh * MLA_HEAD_PAD
        yh = jnp.dot(h, w_ref[:, lo:lo + MLA_HEAD_PAD], preferred_element_type=F32)
        ms = jnp.sum(yh * yh, axis=-1, keepdims=True) * (1.0 / MLA_QK)
        yh = yh * lax.rsqrt(ms + NORM_EPS) * gain_ref[:, lo:lo + MLA_HEAD_PAD]
        o_ref[:, lo:lo + MLA_NOPE] = yh[:, :MLA_NOPE].astype(o_ref.dtype)
        tail = _rope64(yh[:, MLA_NOPE:], c_ref[...], s1_ref[...], s2_ref[...])
        o_ref[:, lo + MLA_NOPE:lo + MLA_HEAD_PAD] = tail.astype(o_ref.dtype)


def mla_q_project(cfull, q_lora_g, w_bf, gain, c_t, s1_t, s2_t, *, tm):
    t = cfull.shape[0]
    kq, n = w_bf.shape
    return pl.pallas_call(
        _mla_q_kernel,
        grid=(t // tm, 1),
        in_specs=[pl.BlockSpec((tm, kq), lambda i, j: (i, 0)),
                  pl.BlockSpec((1, kq), lambda i, j: (0, 0)),
                  pl.BlockSpec((kq, n), lambda i, j: (0, 0), pipeline_mode=pl.Buffered(1)),
                  pl.BlockSpec((1, n), lambda i, j: (0, 0)),
                  pl.BlockSpec((tm, LANES), lambda i, j: (i, 0)),
                  pl.BlockSpec((tm, LANES), lambda i, j: (i, 0)),
                  pl.BlockSpec((tm, LANES), lambda i, j: (i, 0))],
        out_specs=pl.BlockSpec((tm, n), lambda i, j: (i, 0)),
        out_shape=jax.ShapeDtypeStruct((t, n), BF16),
        compiler_params=_cparams("arbitrary", "arbitrary"),
        name="mla_q_project",
    )(cfull, q_lora_g.reshape(1, kq), w_bf, gain, c_t, s1_t, s2_t)


def _mla_kv_kernel(ckv_ref, g_ref, kr_ref, w_ref, gn_ref, gt_ref, c_ref, s1_ref, s2_ref, k_ref, v_ref):
    h = _rms(ckv_ref[...], g_ref[...]).astype(BF16)
    kr = kr_ref[...]
    ss_rope = jnp.sum(kr * kr, axis=-1, keepdims=True)
    rot = _rope64(kr * gt_ref[...], c_ref[...], s1_ref[...], s2_ref[...])
    per_head = MLA_NOPE + MLA_V
    for hh in range(w_ref.shape[1] // per_head):
        y = jnp.dot(h, w_ref[:, hh * per_head:(hh + 1) * per_head], preferred_element_type=F32)
        kn = y[:, :MLA_NOPE]
        ms = (jnp.sum(kn * kn, axis=-1, keepdims=True) + ss_rope) * (1.0 / MLA_QK)
        rs = lax.rsqrt(ms + NORM_EPS)
        lo = hh * MLA_HEAD_PAD
        k_ref[:, lo:lo + MLA_NOPE] = (kn * rs * gn_ref[...]).astype(k_ref.dtype)
        k_ref[:, lo + MLA_NOPE:lo + MLA_HEAD_PAD] = (rot * rs).astype(k_ref.dtype)
        v_ref[:, hh * MLA_V:(hh + 1) * MLA_V] = y[:, MLA_NOPE:].astype(v_ref.dtype)


def mla_kv_project(cfull, kv_lora_g, w_bf, gain_n, gain_t, c_t, s1_t, s2_t, *, tm, q_lora):
    t = cfull.shape[0]
    kkv, n = w_bf.shape
    heads = n // (MLA_NOPE + MLA_V)
    return pl.pallas_call(
        _mla_kv_kernel,
        grid=(t // tm, 1),
        in_specs=[pl.BlockSpec((tm, kkv), lambda i, j: (i, q_lora // kkv)),
                  pl.BlockSpec((1, kkv), lambda i, j: (0, 0)),
                  pl.BlockSpec((tm, LANES), lambda i, j: (i, (q_lora + kkv) // LANES)),
                  pl.BlockSpec((kkv, n), lambda i, j: (0, 0), pipeline_mode=pl.Buffered(1)),
                  pl.BlockSpec((1, LANES), lambda i, j: (0, 0)),
                  pl.BlockSpec((1, LANES), lambda i, j: (0, 0)),
                  pl.BlockSpec((tm, LANES), lambda i, j: (i, 0)),
                  pl.BlockSpec((tm, LANES), lambda i, j: (i, 0)),
                  pl.BlockSpec((tm, LANES), lambda i, j: (i, 0))],
        out_specs=[pl.BlockSpec((tm, heads * MLA_HEAD_PAD), lambda i, j: (i, 0)),
                   pl.BlockSpec((tm, heads * MLA_V), lambda i, j: (i, 0))],
        out_shape=[jax.ShapeDtypeStruct((t, heads * MLA_HEAD_PAD), BF16),
                   jax.ShapeDtypeStruct((t, heads * MLA_V), BF16)],
        compiler_params=_cparams("arbitrary", "arbitrary"),
        name="mla_kv_project",
    )(cfull, kv_lora_g.reshape(1, kkv), cfull, w_bf, gain_n, gain_t, c_t, s1_t, s2_t)


_NT = (((1,), (1,)), ((), ()))


def _flash_scores(q, k, s_buf):
    s_buf[:, :k.shape[0]] = lax.dot_general(q, k, _NT, preferred_element_type=F32)


def _flash_update(s_scr, v, p_scr, m_scr, l_scr, acc_scr, rg):
    tq = s_scr.shape[0]
    w = v.shape[0]
    n_tiles = w // LANES

    for g in range(tq // rg):
        rows_ = slice(g * rg, (g + 1) * rg)
        tiles = [s_scr[rows_, t * LANES:(t + 1) * LANES] for t in range(n_tiles)]
        mx = tiles[0]
        for st in tiles[1:]:
            mx = jnp.maximum(mx, st)
        m_prev = m_scr[rows_, :]
        m_next = jnp.maximum(m_prev, jnp.max(mx, axis=-1, keepdims=True))
        alpha = jnp.exp2(m_prev - m_next)
        psum = None
        for t, st in enumerate(tiles):
            p = jnp.exp2(st - m_next)
            p_scr[rows_, t * LANES:(t + 1) * LANES] = p.astype(p_scr.dtype)
            psum = p if psum is None else psum + p
        l_scr[rows_, :] = alpha * l_scr[rows_, :] + psum
        acc_scr[rows_, :] = alpha * acc_scr[rows_, :]
        m_scr[rows_, :] = m_next
    acc_scr[...] += jnp.dot(p_scr[:, :w], v, preferred_element_type=F32)


def _flash_kernel(*refs, n_lat_chunks, tk, rg):
    if n_lat_chunks:
        q_ref, kc_ref, vc_ref, k_ref, v_ref, _, o_ref, s0, s1, p0, p1, m_scr, l_scr, acc_scr = refs
    else:
        q_ref, kc_ref, vc_ref, _, o_ref, s0, s1, p0, p1, m_scr, l_scr, acc_scr = refs
    state = (m_scr, l_scr, acc_scr, rg)
    m_scr[...] = jnp.full(m_scr.shape, -jnp.inf, F32)
    l_scr[...] = jnp.zeros(l_scr.shape, F32)
    acc_scr[...] = jnp.zeros(acc_scr.shape, F32)
    q = q_ref[...]
    _flash_scores(q, kc_ref[...], s0)
    if n_lat_chunks:
        assert n_lat_chunks % 2 == 0

        def k_chunk(c):
            return k_ref[pl.ds(pl.multiple_of(c * tk, tk), tk), :]

        def v_chunk(c):
            return v_ref[pl.ds(pl.multiple_of(c * tk, tk), tk), :]

        _flash_scores(q, k_chunk(0), s1)
        _flash_update(s0, vc_ref[...], p0, *state)

        def body(j, carry):
            _flash_scores(q, k_chunk(2 * j + 1), s0)
            _flash_update(s1, v_chunk(2 * j), p1, *state)
            _flash_scores(q, k_chunk(2 * j + 2), s1)
            _flash_update(s0, v_chunk(2 * j + 1), p0, *state)
            return carry
        lax.fori_loop(0, n_lat_chunks // 2 - 1, body, 0)
        _flash_scores(q, k_chunk(n_lat_chunks - 1), s0)
        _flash_update(s1, v_chunk(n_lat_chunks - 2), p1, *state)
        _flash_update(s0, v_chunk(n_lat_chunks - 1), p0, *state)
    else:
        _flash_update(s0, vc_ref[...], p0, *state)
    l = jnp.sum(l_scr[...], axis=-1, keepdims=True)
    o_ref[...] = (acc_scr[...] / l).astype(o_ref.dtype)


def _flash_scratch(tq, w):
    return [pltpu.VMEM((tq, w), F32), pltpu.VMEM((tq, w), F32),
            pltpu.VMEM((tq, w), BF16), pltpu.VMEM((tq, w), BF16),
            pltpu.VMEM((tq, LANES), F32), pltpu.VMEM((tq, LANES), F32), pltpu.VMEM((tq, LANES), F32)]


def ctx_attention(qa, ka, va, *, batch, seq, ctx, heads, dk, q_col0, k_col0, v_col0, kv_group):
    t = qa.shape[0]
    ctx_blk0 = (batch * seq) // ctx
    o_shape = jax.ShapeDtypeStruct((t, heads * LANES), BF16)
    return pl.pallas_call(
        functools.partial(_flash_kernel, n_lat_chunks=0, tk=0, rg=FLASH_ROW_GROUP),
        grid=(batch, heads),
        in_specs=[pl.BlockSpec((ctx, dk), lambda b, h: (ctx_blk0 + b, q_col0 + h)),
                  pl.BlockSpec((ctx, dk), lambda b, h: (ctx_blk0 + b, k_col0 + h // kv_group)),
                  pl.BlockSpec((ctx, LANES), lambda b, h: (ctx_blk0 + b, v_col0 + h // kv_group)),
                  pl.BlockSpec(memory_space=pl.ANY)],
        out_specs=pl.BlockSpec((ctx, LANES), lambda b, h: (ctx_blk0 + b, h)),
        out_shape=o_shape,
        input_output_aliases={3: 0},
        scratch_shapes=_flash_scratch(ctx, ctx),
        compiler_params=_cparams("arbitrary", "arbitrary"),
        name="ctx_attention",
    )(qa, ka, va, jnp.zeros(o_shape.shape, o_shape.dtype))


def dense_attention(qa, ka, va, o_ctx, *, batch, seq, ctx, heads, dk, q_col0, k_col0, v_col0, kv_group, tq, tk):
    nq = seq // tq
    ctx_blk0 = (batch * seq) // ctx
    return pl.pallas_call(
        functools.partial(_flash_kernel, n_lat_chunks=seq // tk, tk=tk, rg=FLASH_ROW_GROUP),
        grid=(batch, heads, nq),
        in_specs=[pl.BlockSpec((tq, dk), lambda b, h, i: (b * nq + i, q_col0 + h)),
                  pl.BlockSpec((ctx, dk), lambda b, h, i: (ctx_blk0 + b, k_col0 + h // kv_group)),
                  pl.BlockSpec((ctx, LANES), lambda b, h, i: (ctx_blk0 + b, v_col0 + h // kv_group)),
                  pl.BlockSpec((seq, dk), lambda b, h, i: (b, k_col0 + h // kv_group)),
                  pl.BlockSpec((seq, LANES), lambda b, h, i: (b, v_col0 + h // kv_group)),
                  pl.BlockSpec(memory_space=pl.ANY)],
        out_specs=pl.BlockSpec((tq, LANES), lambda b, h, i: (b * nq + i, h)),
        out_shape=jax.ShapeDtypeStruct(o_ctx.shape, o_ctx.dtype),
        input_output_aliases={5: 0},
        scratch_shapes=_flash_scratch(tq, max(tk, ctx)),
        compiler_params=_cparams("arbitrary", "arbitrary", "arbitrary"),
        name="dense_attention",
    )(qa, ka, va, ka, va, o_ctx)


def _na_kernel(q_ref, k_ref, v_ref, kc_ref, vc_ref, tb_ref, _, o_ref, *, rows, kr):
    kc = kc_ref[...]
    vc = vc_ref[...]
    nt = (((1,), (1,)), ((), ()))

    def body(r, carry):
        rs = jnp.clip(r - kr // 2, 0, rows - kr)
        q0 = pl.multiple_of(r * GRID_W, GRID_W)
        k0 = pl.multiple_of(rs * GRID_W, GRID_W)
        q = q_ref[pl.ds(q0, GRID_W), :]
        kw = k_ref[pl.ds(k0, kr * GRID_W), :]
        vw = v_ref[pl.ds(k0, kr * GRID_W), :]
        s_win = lax.dot_general(q, kw, nt, preferred_element_type=F32) + tb_ref[0, r - rs]
        s_ctx = lax.dot_general(q, kc, nt, preferred_element_type=F32)
        m = jnp.maximum(jnp.max(s_win, axis=-1, keepdims=True), jnp.max(s_ctx, axis=-1, keepdims=True))
        p_win = jnp.exp2(s_win - m)
        p_ctx = jnp.exp2(s_ctx - m)
        l = jnp.sum(p_win, axis=-1, keepdims=True) + jnp.sum(p_ctx, axis=-1, keepdims=True)
        o = (jnp.dot(p_win.astype(vw.dtype), vw, preferred_element_type=F32)
             + jnp.dot(p_ctx.astype(vc.dtype), vc, preferred_element_type=F32))
        o_ref[pl.ds(q0, GRID_W), :] = (o / l).astype(o_ref.dtype)
        return carry

    lax.fori_loop(0, rows, body, 0, unroll=4)


def na_attention(qkv, tb, o_ctx, *, batch, seq, ctx, heads):
    t = qkv.shape[0]
    rows = seq // GRID_W
    kr = min(NA_ROWS, rows)
    ctx_blk0 = (batch * seq) // ctx
    d = HEAD_DIM
    return pl.pallas_call(
        functools.partial(_na_kernel, rows=rows, kr=kr),
        grid=(batch, heads),
        in_specs=[pl.BlockSpec((seq, d), lambda b, h: (b, h)),
                  pl.BlockSpec((seq, d), lambda b, h: (b, heads + h)),
                  pl.BlockSpec((seq, d), lambda b, h: (b, 2 * heads + h)),
                  pl.BlockSpec((ctx, d), lambda b, h: (ctx_blk0 + b, heads + h)),
                  pl.BlockSpec((ctx, d), lambda b, h: (ctx_blk0 + b, 2 * heads + h)),
                  pl.BlockSpec((1, kr, GRID_W, kr * GRID_W), lambda b, h: (h, 0, 0, 0)),
                  pl.BlockSpec(memory_space=pl.ANY)],
        out_specs=pl.BlockSpec((seq, d), lambda b, h: (b, h)),
        out_shape=jax.ShapeDtypeStruct((t, heads * d), BF16),
        input_output_aliases={6: 0},
        compiler_params=_cparams("arbitrary", "arbitrary"),
        name="na_attention",
    )(qkv, qkv, qkv, qkv, qkv, tb, o_ctx)


def _pack_bf16_pairs(x):
    n = x.shape[1] // 2
    lo = lax.bitcast_convert_type(x[:, :n].astype(BF16).astype(F32), jnp.uint32)
    hi = lax.bitcast_convert_type(x[:, n:].astype(BF16).astype(F32), jnp.uint32)
    return (lo >> 16) | (hi & jnp.uint32(0xFFFF0000))


def _unpack_lo(words):
    return lax.bitcast_convert_type(words << 16, F32)


def _unpack_hi(words):
    return lax.bitcast_convert_type(words & jnp.uint32(0xFFFF0000), F32)


def _proj_router_kernel(o_ref, wo_ref, x_ref, gate_ref, g_ref, sh_ref, sc_ref, w_ref, b_ref,
                        xo_ref, h_ref, route_ref, counts_ref, carry_scr, *, n_groups, epg):
    @pl.when(pl.program_id(0) == 0)
    def _():
        carry_scr[...] = jnp.zeros(carry_scr.shape, F32)

    x_new = x_ref[...] + gate_ref[0] * jnp.dot(o_ref[...], wo_ref[...], preferred_element_type=F32)
    xo_ref[...] = x_new
    h = _rms(x_new, g_ref[...]) * (1.0 + sc_ref[0]) + sh_ref[0]
    h_ref[...] = _pack_bf16_pairs(h)
    w = w_ref[...]
    h_hi = h.astype(BF16)
    h_lo = (h - h_hi.astype(F32)).astype(BF16)
    w_hi = w.astype(BF16)
    w_lo = (w - w_hi.astype(F32)).astype(BF16)
    logits = (jnp.dot(h_hi, w_hi, preferred_element_type=F32)
              + jnp.dot(h_lo, w_hi, preferred_element_type=F32)
              + jnp.dot(h_hi, w_lo, preferred_element_type=F32)) + b_ref[...]
    lane = lax.broadcasted_iota(jnp.int32, logits.shape, 1).astype(F32)
    is_g = lane < n_groups
    g_max = jnp.max(jnp.where(is_g, logits, -jnp.inf), axis=-1, keepdims=True)
    g_sum = jnp.sum(jnp.where(is_g, jnp.exp(logits - g_max), 0.0), axis=-1, keepdims=True)
    g_sel = jnp.min(jnp.where(is_g & (logits == g_max), lane, float(LANES)), axis=-1, keepdims=True)
    lo = n_groups + epg * g_sel
    in_grp = (lane >= lo) & (lane < lo + epg)
    t1 = jnp.max(jnp.where(in_grp, logits, -jnp.inf), axis=-1, keepdims=True)
    i1 = jnp.min(jnp.where(in_grp & (logits == t1), lane, float(LANES)), axis=-1, keepdims=True)
    rest = in_grp & (lane != i1)
    t2 = jnp.max(jnp.where(rest, logits, -jnp.inf), axis=-1, keepdims=True)
    i2 = jnp.min(jnp.where(rest & (logits == t2), lane, float(LANES)), axis=-1, keepdims=True)
    d = jnp.exp(t2 - t1)
    gate = 1.0 / g_sum
    w1 = gate * (1.0 / (1.0 + d))
    w2 = gate * (d / (1.0 + d))
    tm = logits.shape[0]
    chosen = jnp.where((lane == i1) | (lane == i2), 1.0, 0.0)
    earlier = jnp.where(lax.broadcasted_iota(jnp.int32, (tm, tm), 1) < lax.broadcasted_iota(jnp.int32, (tm, tm), 0),
                        1.0, 0.0).astype(BF16)
    before = jnp.dot(earlier, chosen.astype(BF16), preferred_element_type=F32) + carry_scr[...]
    rank1 = jnp.sum(jnp.where(lane == i1, before, 0.0), axis=-1, keepdims=True)
    rank2 = jnp.sum(jnp.where(lane == i2, before, 0.0), axis=-1, keepdims=True)
    total = carry_scr[...] + jnp.sum(chosen, axis=0, keepdims=True)
    carry_scr[...] = total
    counts_ref[...] = jnp.broadcast_to(total, counts_ref.shape)
    vals = (i1 - n_groups, i2 - n_groups, w1, w2, rank1, rank2)
    route = jnp.zeros(logits.shape, F32)
    for pos, v in enumerate(vals):
        route = jnp.where(lane == pos, v, route)
    route_ref[...] = route


def proj_residual_router(o, wo_bf, xs, mods, layer, norm_g, w_r, b_r, *, tm, seq, batch, n_groups, epg):
    t, d = xs.shape
    k = o.shape[1]
    gate_spec, sh_spec, sc_spec = _mod_specs(layer, (2, 3, 4), tm, seq, batch, d)
    return pl.pallas_call(
        functools.partial(_proj_router_kernel, n_groups=n_groups, epg=epg),
        grid=(t // tm, 1),
        in_specs=[pl.BlockSpec((tm, k), lambda i, j: (i, 0)),
                  pl.BlockSpec((k, d), lambda i, j: (0, 0), pipeline_mode=pl.Buffered(1)),
                  pl.BlockSpec((tm, d), lambda i, j: (i, 0)),
                  gate_spec,
                  pl.BlockSpec((1, d), lambda i, j: (0, 0)),
                  sh_spec, sc_spec,
                  pl.BlockSpec((d, LANES), lambda i, j: (0, 0)),
                  pl.BlockSpec((1, LANES), lambda i, j: (0, 0))],
        out_specs=[pl.BlockSpec((tm, d), lambda i, j: (i, 0)),
                   pl.BlockSpec((tm, d // 2), lambda i, j: (i, 0)),
                   pl.BlockSpec((tm, LANES), lambda i, j: (i, 0)),
                   pl.BlockSpec((8, LANES), lambda i, j: (0, 0))],
        out_shape=[jax.ShapeDtypeStruct((t, d), F32), jax.ShapeDtypeStruct((t, d // 2), jnp.uint32),
                   jax.ShapeDtypeStruct((t, LANES), F32), jax.ShapeDtypeStruct((8, LANES), F32)],
        scratch_shapes=[pltpu.VMEM((1, LANES), F32)],
        compiler_params=_cparams("arbitrary", "arbitrary"),
        name="proj_residual_router",
    )(o, wo_bf, xs, mods, norm_g.reshape(1, d), mods, mods, w_r, b_r)


def _issue_row_gather(idx_ref, src_hbm, dst, sem, n_rows):
    def body(r, carry):
        tok = idx_ref[0, 0, r]
        pltpu.make_async_copy(src_hbm.at[pl.ds(tok, 1)], dst.at[pl.ds(r, 1)], sem).start()
        return carry
    lax.fori_loop(0, n_rows, body, 0, unroll=8)


def _wait_row_gather(src_hbm, dst, sem, n_rows):
    pltpu.make_async_copy(src_hbm.at[pl.ds(0, n_rows)], dst, sem).wait()


def _expert_kernel(be_ref, idx_ref, idx1_ref, idx2_ref, h_hbm, win_ref, wout_ref, y_ref,
                   xbuf0, xbuf1, xbuf2, sem, win_bf, wout_bf, *, nb, cast_rows, k_chunk):
    i = pl.program_id(0)
    blk, half = xbuf0.shape
    bufs = (xbuf0, xbuf1, xbuf2)
    n_points = 2 * half // k_chunk + half // k_chunk
    base, extra = divmod(blk, n_points)
    point_rows = [base + (1 if p < extra else 0) for p in range(n_points)]

    @pl.when(i == 0)
    def _():
        _issue_row_gather(idx_ref, h_hbm, xbuf0, sem.at[0], blk)
        _issue_row_gather(idx1_ref, h_hbm, xbuf1, sem.at[1], blk)

    expert_changed = (i == 0) | (be_ref[i] != be_ref[jnp.maximum(i - 1, 0)])

    @pl.when(expert_changed)
    def _():
        def cast_in(c, carry):
            r0 = pl.multiple_of(c * cast_rows, cast_rows)
            win_bf[pl.ds(r0, cast_rows), :] = win_ref[0, 0, pl.ds(r0, cast_rows), :].astype(BF16)
            return carry
        lax.fori_loop(0, win_bf.shape[0] // cast_rows, cast_in, 0)

        def cast_out(c, carry):
            r0 = pl.multiple_of(c * cast_rows, cast_rows)
            wout_bf[pl.ds(r0, cast_rows), :] = wout_ref[0, 0, pl.ds(r0, cast_rows), :].astype(BF16)
            return carry
        lax.fori_loop(0, wout_bf.shape[0] // cast_rows, cast_out, 0)

    def run(cur):
        ahead = (cur + 2) % 3
        x_cur, x_new = bufs[cur], bufs[ahead]
        issued = [0]

        def request_rows(point):
            for r in range(issued[0], issued[0] + point_rows[point]):
                pltpu.make_async_copy(h_hbm.at[pl.ds(idx2_ref[0, 0, r], 1)], x_new.at[pl.ds(r, 1)],
                                      sem.at[ahead]).start()
            issued[0] += point_rows[point]

        _wait_row_gather(h_hbm, x_cur, sem.at[cur], blk)
        point = 0
        gu = None
        for f0 in range(0, 2 * half, k_chunk):
            request_rows(point)
            point += 1
            words = x_cur[:, f0 % half:f0 % half + k_chunk]
            xk = _unpack_lo(words) if f0 < half else _unpack_hi(words)
            part = jnp.dot(xk.astype(BF16), win_bf[f0:f0 + k_chunk, :], preferred_element_type=F32)
            gu = part if gu is None else gu + part
        f = gu.shape[1] // 2
        act = (_silu(gu[:, :f]) * gu[:, f:]).astype(BF16)
        for c0 in range(0, half, k_chunk):
            request_rows(point)
            point += 1
            y_lo = jnp.dot(act, wout_bf[:, c0:c0 + k_chunk], preferred_element_type=F32)
            y_hi = jnp.dot(act, wout_bf[:, half + c0:half + c0 + k_chunk], preferred_element_type=F32)
            y_ref[:, c0:c0 + k_chunk] = _pack_bf16_pairs(jnp.concatenate([y_lo, y_hi], axis=1))
        assert issued[0] == blk

        @pl.when(i == nb - 1)
        def _():
            _wait_row_gather(h_hbm, bufs[(cur + 1) % 3], sem.at[(cur + 1) % 3], blk)
            _wait_row_gather(h_hbm, x_new, sem.at[ahead], blk)

    for c in range(3):
        @pl.when(i % 3 == c)
        def _(c=c):
            run(c)


def moe_experts(h, slot_tok, block_expert, w_in, w_out, layer):
    half = h.shape[1]
    d = 2 * half
    f2 = w_in.shape[-1]
    f = f2 // 2
    nb = block_expert.shape[0]
    blk = slot_tok.shape[0] // nb
    idx = slot_tok.reshape(nb, 1, blk)
    grid_spec = pltpu.PrefetchScalarGridSpec(
        num_scalar_prefetch=1,
        grid=(nb,),
        in_specs=[pl.BlockSpec((1, 1, blk), lambda i, be: (i, 0, 0), memory_space=pltpu.SMEM),
                  pl.BlockSpec((1, 1, blk), lambda i, be: (jnp.minimum(i + 1, nb - 1), 0, 0),
                               memory_space=pltpu.SMEM),
                  pl.BlockSpec((1, 1, blk), lambda i, be: (jnp.minimum(i + 2, nb - 1), 0, 0),
                               memory_space=pltpu.SMEM),
                  pl.BlockSpec(memory_space=pl.ANY),
                  pl.BlockSpec((1, 1, d, f2), lambda i, be: (layer, be[i], 0, 0)),
                  pl.BlockSpec((1, 1, f, d), lambda i, be: (layer, be[i], 0, 0))],
        out_specs=pl.BlockSpec((blk, half), lambda i, be: (i, 0)),
        scratch_shapes=[pltpu.VMEM((blk, half), jnp.uint32),
                        pltpu.VMEM((blk, half), jnp.uint32),
                        pltpu.VMEM((blk, half), jnp.uint32),
                        pltpu.SemaphoreType.DMA((3,)),
                        pltpu.VMEM((d, f2), BF16),
                        pltpu.VMEM((f, d), BF16)],
    )
    return pl.pallas_call(
        functools.partial(_expert_kernel, nb=nb, cast_rows=256, k_chunk=256),
        grid_spec=grid_spec,
        out_shape=jax.ShapeDtypeStruct((nb * blk, half), jnp.uint32),
        compiler_params=_cparams("arbitrary"),
        name="moe_experts",
    )(block_expert, idx, idx, idx, h, w_in, w_out)


def _combine_kernel(pos_ref, pos_next_ref, yb_hbm, x_ref, route_ref, gate_ref, o_ref, ybuf, sem, *, nt):
    i = pl.program_id(0)
    slot = i % 2
    n_rows = ybuf.shape[1]
    tm = n_rows // MOE_TOP_K

    @pl.when(i == 0)
    def _():
        _issue_row_gather(pos_ref, yb_hbm, ybuf.at[0], sem.at[0], n_rows)

    @pl.when(i + 1 < nt)
    def _():
        _issue_row_gather(pos_next_ref, yb_hbm, ybuf.at[1 - slot], sem.at[1 - slot], n_rows)

    _wait_row_gather(yb_hbm, ybuf.at[slot], sem.at[slot], n_rows)
    route = route_ref[...]
    w1, w2 = route[:, 2:3], route[:, 3:4]
    y1, y2 = ybuf[slot, :tm, :], ybuf[slot, tm:, :]
    half = y1.shape[1]
    gate = gate_ref[0]
    o_ref[:, :half] = x_ref[:, :half] + gate[:, :half] * (w1 * _unpack_lo(y1) + w2 * _unpack_lo(y2))
    o_ref[:, half:] = x_ref[:, half:] + gate[:, half:] * (w1 * _unpack_hi(y1) + w2 * _unpack_hi(y2))


def moe_combine(yb, pos, xs, route, mods, layer, *, tm, seq, batch, out_rows):
    t, d = xs.shape
    pos_t = pos.reshape(t // tm, tm, MOE_TOP_K).transpose(0, 2, 1).reshape(t // tm, 1, MOE_TOP_K * tm)
    nt = out_rows // tm

    def seg(i):
        return jnp.minimum((i * tm) // seq, batch)

    return pl.pallas_call(
        functools.partial(_combine_kernel, nt=nt),
        grid=(nt,),
        in_specs=[pl.BlockSpec((1, 1, MOE_TOP_K * tm), lambda i: (i, 0, 0), memory_space=pltpu.SMEM),
                  pl.BlockSpec((1, 1, MOE_TOP_K * tm), lambda i: (jnp.minimum(i + 1, nt - 1), 0, 0),
                               memory_space=pltpu.SMEM),
                  pl.BlockSpec(memory_space=pl.ANY),
                  pl.BlockSpec((tm, d), lambda i: (i, 0)),
                  pl.BlockSpec((tm, LANES), lambda i: (i, 0)),
                  pl.BlockSpec((1, 1, d), lambda i: ((layer * MOD_ROWS + seg(i)) * 6 + 5, 0, 0))],
        out_specs=pl.BlockSpec((tm, d), lambda i: (i, 0)),
        out_shape=jax.ShapeDtypeStruct((out_rows, d), F32),
        scratch_shapes=[pltpu.VMEM((2, MOE_TOP_K * tm, d // 2), jnp.uint32), pltpu.SemaphoreType.DMA((2,))],
        compiler_params=_cparams("arbitrary"),
        name="moe_combine",
    )(pos_t, pos_t, yb, xs, route, mods)


def _dispatch_plan(route, counts, n_experts, blk):
    t = route.shape[0]
    nk = t * MOE_TOP_K
    padded = (counts + blk - 1) // blk * blk
    pad_end = jnp.cumsum(padded)
    pad_start = pad_end - padded
    nb = -(-(nk + n_experts * (blk - 1)) // blk)
    e = route[:, :MOE_TOP_K].astype(jnp.int32)
    rank = route[:, 2 * MOE_TOP_K:3 * MOE_TOP_K].astype(jnp.int32)
    onehot = e[..., None] == jnp.arange(n_experts, dtype=jnp.int32)
    dest = jnp.sum(jnp.where(onehot, pad_start, 0), axis=-1) + rank
    tok = jnp.arange(nk, dtype=jnp.int32) // MOE_TOP_K
    slot_tok = jnp.zeros((nb * blk,), jnp.int32).at[dest.reshape(nk)].set(tok, unique_indices=True)
    first_row = jnp.arange(nb, dtype=jnp.int32) * blk
    block_expert = jnp.minimum(jnp.sum(pad_end[None, :] <= first_row[:, None], axis=1), n_experts - 1)
    return slot_tok, block_expert.astype(jnp.int32), dest


def _axial_angles(seq, dim):
    n_freq = dim // 4
    freqs = ROPE_THETA ** (-jnp.arange(n_freq, dtype=F32) / n_freq)
    tok = jnp.arange(seq, dtype=jnp.int32)
    row = (tok // GRID_W).astype(F32)
    col = (tok % GRID_W).astype(F32)
    return jnp.concatenate([row[:, None] * freqs, col[:, None] * freqs], axis=-1)


def _stream_table(lat, ctx_fill, batch, n_ctx_rows):
    ctx_rows = jnp.broadcast_to(ctx_fill[None, :], (n_ctx_rows, lat.shape[1]))
    return jnp.concatenate([jnp.tile(lat, (batch, 1)), ctx_rows], axis=0)


def _rope_tables_128(seq, batch, n_ctx_rows):
    ang = _axial_angles(seq, HEAD_DIM)
    cos, sin = jnp.cos(ang), jnp.sin(ang)
    c = jnp.concatenate([cos, cos], axis=-1)
    s = jnp.concatenate([-sin, sin], axis=-1)
    return (_stream_table(c, jnp.ones((HEAD_DIM,), F32), batch, n_ctx_rows),
            _stream_table(s, jnp.zeros((HEAD_DIM,), F32), batch, n_ctx_rows))


def _rope_tables_64(seq, batch, n_ctx_rows):
    ang = _axial_angles(seq, MLA_ROPE)
    cos, sin = jnp.cos(ang), jnp.sin(ang)
    half = MLA_ROPE // 2
    z = jnp.zeros((seq, LANES - MLA_ROPE), F32)
    zh = jnp.zeros((seq, half), F32)
    c = jnp.concatenate([cos, cos, z], axis=-1)
    s1 = jnp.concatenate([-sin, zh, z], axis=-1)
    s2 = jnp.concatenate([zh, sin, z], axis=-1)
    ones_pad = jnp.concatenate([jnp.ones((MLA_ROPE,), F32), jnp.zeros((LANES - MLA_ROPE,), F32)])
    zeros = jnp.zeros((LANES,), F32)
    return (_stream_table(c, ones_pad, batch, n_ctx_rows),
            _stream_table(s1, zeros, batch, n_ctx_rows),
            _stream_table(s2, zeros, batch, n_ctx_rows))


def _na_bias_table(rpb, kr):
    col = jnp.arange(GRID_W, dtype=jnp.int32)
    col_start = jnp.clip(col - NA_COLS // 2, 0, GRID_W - NA_COLS)
    col_in = (col[None, :] >= col_start[:, None]) & (col[None, :] < col_start[:, None] + NA_COLS)
    dc = jnp.clip(col[None, :] - col[:, None] + NA_COLS - 1, 0, 2 * NA_COLS - 2)
    h = rpb.shape[0]
    by_col = jnp.zeros((h, 2 * NA_ROWS - 1, GRID_W, GRID_W), F32)
    for cc in range(2 * NA_COLS - 1):
        by_col = jnp.where(dc[None, None] == cc, rpb[:, :, cc, None, None].astype(F32), by_col)
    by_col = jnp.where(col_in[None, None], by_col, NEG_INF)
    per_shift = [by_col[:, NA_ROWS - 1 - s:NA_ROWS - 1 - s + kr] for s in range(kr)]
    bias = jnp.stack(per_shift, axis=1)
    return bias.transpose(0, 1, 3, 2, 4).reshape(h, kr, GRID_W, kr * GRID_W)


def _pick_tile(*extents):
    for tm in (512, 256, 128):
        if all(e % tm == 0 for e in extents):
            return tm
    raise ValueError("row extents must be multiples of 128")


def kernel(x, c, ctx, c_ctx, mod_w, mod_b, norm_g, gqa_w_qkv, gqa_q_norm, gqa_k_norm, gqa_w_o, na_w_qkv, na_q_norm, na_k_norm, na_rpb, na_w_o, mla_w_down, mla_q_lora_norm, mla_w_uq, mla_kv_lora_norm, mla_w_ukv, mla_q_norm, mla_k_norm, mla_w_o, moe_w_group, moe_b_group, moe_w_expert, moe_b_expert, moe_w_in, moe_w_out):
    batch, seq, d = x.shape
    n_ctx = ctx.shape[1]
    depth = mod_w.shape[0]
    heads = d // HEAD_DIM
    kv_heads = heads // 4
    n_lat = batch * seq
    n_ctx_rows = batch * n_ctx
    assert batch + 1 <= MOD_ROWS and n_lat % n_ctx == 0 and seq % GRID_W == 0
    tm = _pick_tile(seq, n_ctx_rows)
    tq = min(FLASH_TQ, seq)
    tk = min(FLASH_TK, seq)
    assert seq % tq == 0 and seq % (2 * tk) == 0
    tn_d = min(512, d)
    dims = dict(tm=tm, seq=seq, batch=batch)

    xs = jnp.concatenate([x.reshape(n_lat, d), ctx.reshape(n_ctx_rows, d)], axis=0)

    cond = jnp.zeros((MOD_ROWS, d), F32).at[:batch].set(c).at[batch].set(c_ctx)
    mods = adaln_all(cond, mod_w, mod_b).reshape(depth * MOD_ROWS * 6, 1, d)

    cos_a, sin_a = _rope_tables_128(seq, batch, n_ctx_rows)
    c64, s64a, s64b = _rope_tables_64(seq, batch, n_ctx_rows)

    n_groups = moe_w_group.shape[-1]
    epg = moe_w_expert.shape[-1]
    n_experts = n_groups * epg
    assert n_groups + n_experts <= LANES

    for i in range(depth):
        kind, j = i % 3, i // 3
        if kind == 0:
            scale = HEAD_DIM ** -0.5 * LOG2E
            nq, nkv = heads * HEAD_DIM, kv_heads * HEAD_DIM
            gain = jnp.concatenate([jnp.tile(gqa_q_norm[j] * scale, heads), jnp.tile(gqa_k_norm[j], kv_heads),
                                    jnp.ones((nkv,), F32)]).reshape(1, -1)
            qkv = qkv_project(xs, mods, i, norm_g[i, 0], gqa_w_qkv[j].astype(BF16), gain, cos_a, sin_a,
                              n_norm_cols=nq + nkv, tn=min(512, nkv), rope=True, **dims)
            cols = dict(dk=HEAD_DIM, q_col0=0, k_col0=heads, v_col0=heads + kv_heads, kv_group=heads // kv_heads)
            o_ctx = ctx_attention(qkv, qkv, qkv, batch=batch, seq=seq, ctx=n_ctx, heads=heads, **cols)
            o = dense_attention(qkv, qkv, qkv, o_ctx, batch=batch, seq=seq, ctx=n_ctx, heads=heads, tq=tq, tk=tk,
                                **cols)
            w_o = gqa_w_o[j]
        elif kind == 1:
            scale = HEAD_DIM ** -0.5 * LOG2E
            nq = heads * HEAD_DIM
            gain = jnp.concatenate([jnp.tile(na_q_norm[j] * scale, heads), jnp.tile(na_k_norm[j], heads),
                                    jnp.ones((nq,), F32)]).reshape(1, -1)
            qkv = qkv_project(xs, mods, i, norm_g[i, 0], na_w_qkv[j].astype(BF16), gain, cos_a, sin_a,
                              n_norm_cols=2 * nq, tn=tn_d, rope=False, **dims)
            rows = seq // GRID_W
            tb = _na_bias_table(na_rpb[j], min(NA_ROWS, rows)) * LOG2E
            o_ctx = ctx_attention(qkv, qkv, qkv, batch=batch, seq=seq, ctx=n_ctx, heads=heads, dk=HEAD_DIM,
                                  q_col0=0, k_col0=heads, v_col0=2 * heads, kv_group=1)
            o = na_attention(qkv, tb, o_ctx, batch=batch, seq=seq, ctx=n_ctx, heads=heads)
            w_o = na_w_o[j]
        else:
            scale = MLA_QK ** -0.5 * LOG2E
            q_lora = mla_q_lora_norm.shape[-1]
            kv_lora = mla_kv_lora_norm.shape[-1]
            assert q_lora % kv_lora == 0 and (q_lora + kv_lora) % LANES == 0
            n_down = q_lora + kv_lora + LANES
            w_down = jnp.pad(mla_w_down[j], ((0, 0), (0, n_down - mla_w_down.shape[-1]))).astype(BF16)
            cfull = mla_down(xs, mods, i, norm_g[i, 0], w_down, **dims)
            pad = MLA_HEAD_PAD - MLA_QK
            w_uq = jnp.pad(mla_w_uq[j].reshape(q_lora, heads, MLA_QK), ((0, 0), (0, 0), (0, pad)))
            w_uq = w_uq.reshape(q_lora, heads * MLA_HEAD_PAD).astype(BF16)
            q_gain = jnp.tile(jnp.pad(mla_q_norm[j] * scale, (0, pad)), heads).reshape(1, -1)
            qa = mla_q_project(cfull, mla_q_lora_norm[j], w_uq, q_gain, c64, s64a, s64b, tm=tm)
            gain_n = mla_k_norm[j, :MLA_NOPE].reshape(1, LANES)
            gain_t = jnp.pad(mla_k_norm[j, MLA_NOPE:], (0, LANES - MLA_ROPE)).reshape(1, LANES)
            ka, va = mla_kv_project(cfull, mla_kv_lora_norm[j], mla_w_ukv[j].astype(BF16), gain_n, gain_t,
                                    c64, s64a, s64b, tm=tm, q_lora=q_lora)
            cols = dict(dk=MLA_HEAD_PAD, q_col0=0, k_col0=0, v_col0=0, kv_group=1)
            o_ctx = ctx_attention(qa, ka, va, batch=batch, seq=seq, ctx=n_ctx, heads=heads, **cols)
            o = dense_attention(qa, ka, va, o_ctx, batch=batch, seq=seq, ctx=n_ctx, heads=heads, tq=tq, tk=tk, **cols)
            w_o = mla_w_o[j]

        w_r = jnp.concatenate([moe_w_group[i], moe_w_expert[i].transpose(1, 0, 2).reshape(d, n_experts)], axis=1)
        w_r = jnp.pad(w_r, ((0, 0), (0, LANES - w_r.shape[1])))
        b_r = jnp.pad(jnp.concatenate([moe_b_group[i], moe_b_expert[i].reshape(-1)]),
                      (0, LANES - n_groups - n_experts)).reshape(1, LANES)
        xs, h2, route, lane_counts = proj_residual_router(o, w_o.astype(BF16), xs, mods, i, norm_g[i, 1], w_r, b_r,
                                                          n_groups=n_groups, epg=epg, **dims)
        counts = lane_counts[0, n_groups:n_groups + n_experts].astype(jnp.int32)
        slot_tok, block_expert, pos = _dispatch_plan(route, counts, n_experts, MOE_BLOCK)
        yb = moe_experts(h2, slot_tok, block_expert, moe_w_in, moe_w_out, i)
        out_rows = n_lat if i == depth - 1 else n_lat + n_ctx_rows
        xs = moe_combine(yb, pos, xs, route, mods, i, tm=min(COMBINE_TM, tm), seq=seq, batch=batch,
                         out_rows=out_rows)

    return xs.reshape(batch, seq, d)
```

```python
import functools

import jax
import jax.numpy as jnp
from jax import lax
from jax.experimental import pallas as pl
from jax.experimental.pallas import tpu as pltpu

F32 = jnp.float32
BF16 = jnp.bfloat16

GRID_W = 64
HEAD_DIM = 128
ROPE_THETA = 10000.0
NORM_EPS = 1e-6
NEG_INF = -1e30
NA_ROWS = 8
NA_COLS = 16
MLA_NOPE = 128
MLA_ROPE = 64
MLA_V = 128
MLA_QK = MLA_NOPE + MLA_ROPE
MLA_HEAD_PAD = 256
MOE_TOP_K = 2
LANES = 128
MOD_ROWS = 8
MOE_BLOCK = 256
COMBINE_TM = 256
FLASH_TQ = 1024
FLASH_TK = 1024
FLASH_ROW_GROUP = 32
LOG2E = 1.4426950408889634
VMEM_LIMIT = 52 * 1024 * 1024


def _cparams(*sem):
    return pltpu.CompilerParams(dimension_semantics=sem, vmem_limit_bytes=VMEM_LIMIT)


def _silu(v):
    return v / (1.0 + jnp.exp(-v))


def _adaln_kernel(c_ref, w_ref, b_ref, o_ref):
    cond = _silu(c_ref[...]).astype(BF16)
    o_ref[0] = jnp.dot(cond, w_ref[0].astype(BF16), preferred_element_type=F32) + b_ref[0]


def adaln_all(cond, mod_w, mod_b):
    depth, d, n = mod_w.shape
    tn = 1024
    return pl.pallas_call(
        _adaln_kernel,
        grid=(depth, n // tn),
        in_specs=[pl.BlockSpec((MOD_ROWS, d), lambda l, j: (0, 0)),
                  pl.BlockSpec((1, d, tn), lambda l, j: (l, 0, j)),
                  pl.BlockSpec((1, 1, tn), lambda l, j: (l, 0, j))],
        out_specs=pl.BlockSpec((1, MOD_ROWS, tn), lambda l, j: (l, 0, j)),
        out_shape=jax.ShapeDtypeStruct((depth, MOD_ROWS, n), F32),
        compiler_params=_cparams("arbitrary", "arbitrary"),
        name="adaln",
    )(cond, mod_w, mod_b.reshape(depth, 1, n))


def _rms(x, g):
    ms = jnp.mean(x * x, axis=-1, keepdims=True)
    return x * lax.rsqrt(ms + NORM_EPS) * g


def _rope128(y, c, s):
    return y * c + pltpu.roll(y, 64, 1) * s


def _rope64(y, c, s1, s2):
    return y * c + pltpu.roll(y, 96, 1) * s1 + pltpu.roll(y, 32, 1) * s2


def _qkv_kernel(x_ref, g_ref, sh_ref, sc_ref, w_ref, gain_ref, c_ref, s_ref, o_ref, *, tn, n_norm_cols, rope):
    h = (_rms(x_ref[...], g_ref[...]) * (1.0 + sc_ref[0]) + sh_ref[0]).astype(BF16)
    for c0 in range(0, w_ref.shape[1], tn):
        y = jnp.dot(h, w_ref[:, c0:c0 + tn], preferred_element_type=F32)
        if c0 >= n_norm_cols:
            o_ref[:, c0:c0 + tn] = y.astype(o_ref.dtype)
            continue
        for hh in range(tn // HEAD_DIM):
            sl = slice(c0 + hh * HEAD_DIM, c0 + (hh + 1) * HEAD_DIM)
            yh = _rms(y[:, hh * HEAD_DIM:(hh + 1) * HEAD_DIM], gain_ref[:, sl])
            if rope:
                yh = _rope128(yh, c_ref[...], s_ref[...])
            o_ref[:, sl] = yh.astype(o_ref.dtype)


def _mod_specs(layer, which, tm, seq, batch, d):
    def seg(i):
        return jnp.minimum((i * tm) // seq, batch)
    return [pl.BlockSpec((1, 1, d), lambda i, j, w=w: ((layer * MOD_ROWS + seg(i)) * 6 + w, 0, 0)) for w in which]


def qkv_project(xs, mods, layer, norm_g, w_bf, gain, cos_t, sin_t, *, tm, seq, batch, n_norm_cols, tn, rope):
    t, d = xs.shape
    n = w_bf.shape[1]
    sh_spec, sc_spec = _mod_specs(layer, (0, 1), tm, seq, batch, d)
    assert n_norm_cols % tn == 0 and n % tn == 0
    kern = functools.partial(_qkv_kernel, tn=tn, n_norm_cols=n_norm_cols, rope=rope)
    return pl.pallas_call(
        kern,
        grid=(t // tm, 1),
        in_specs=[pl.BlockSpec((tm, d), lambda i, j: (i, 0)),
                  pl.BlockSpec((1, d), lambda i, j: (0, 0)),
                  sh_spec, sc_spec,
                  pl.BlockSpec((d, n), lambda i, j: (0, 0), pipeline_mode=pl.Buffered(1)),
                  pl.BlockSpec((1, n), lambda i, j: (0, 0)),
                  pl.BlockSpec((tm, HEAD_DIM), lambda i, j: (i, 0)),
                  pl.BlockSpec((tm, HEAD_DIM), lambda i, j: (i, 0))],
        out_specs=pl.BlockSpec((tm, n), lambda i, j: (i, 0)),
        out_shape=jax.ShapeDtypeStruct((t, n), BF16),
        compiler_params=_cparams("arbitrary", "arbitrary"),
        name="qkv_project",
    )(xs, norm_g.reshape(1, d), mods, mods, w_bf, gain, cos_t, sin_t)


def _down_kernel(x_ref, g_ref, sh_ref, sc_ref, w_ref, o_ref):
    h = _rms(x_ref[...], g_ref[...]) * (1.0 + sc_ref[0]) + sh_ref[0]
    o_ref[...] = jnp.dot(h.astype(BF16), w_ref[...], preferred_element_type=F32)


def mla_down(xs, mods, layer, norm_g, w_bf, *, tm, seq, batch):
    t, d = xs.shape
    n = w_bf.shape[1]
    sh_spec, sc_spec = _mod_specs(layer, (0, 1), tm, seq, batch, d)
    return pl.pallas_call(
        _down_kernel,
        grid=(t // tm, 1),
        in_specs=[pl.BlockSpec((tm, d), lambda i, j: (i, 0)),
                  pl.BlockSpec((1, d), lambda i, j: (0, 0)),
                  sh_spec, sc_spec,
                  pl.BlockSpec((d, n), lambda i, j: (0, 0))],
        out_specs=pl.BlockSpec((tm, n), lambda i, j: (i, 0)),
        out_shape=jax.ShapeDtypeStruct((t, n), F32),
        compiler_params=_cparams("arbitrary", "arbitrary"),
        name="mla_down",
    )(xs, norm_g.reshape(1, d), mods, mods, w_bf)


def _mla_q_kernel(cq_ref, g_ref, w_ref, gain_ref, c_ref, s1_ref, s2_ref, o_ref):
    h = _rms(cq_ref[...], g_ref[...]).astype(BF16)
    for hh in range(w_ref.shape[1] // MLA_HEAD_PAD):
        lo = hh * MLA_HEAD_PAD
        yh = jnp.dot(h, w_ref[:, lo:lo + MLA_HEAD_PAD], preferred_element_type=F32)
        ms = jnp.sum(yh * yh, axis=-1, keepdims=True) * (1.0 / MLA_QK)
        yh = yh * lax.rsqrt(ms + NORM_EPS) * gain_ref[:, lo:lo + MLA_HEAD_PAD]
        o_ref[:, lo:lo + MLA_NOPE] = yh[:, :MLA_NOPE].astype(o_ref.dtype)
        tail = _rope64(yh[:, MLA_NOPE:], c_ref[...], s1_ref[...], s2_ref[...])
        o_ref[:, lo + MLA_NOPE:lo + MLA_HEAD_PAD] = tail.astype(o_ref.dtype)


def mla_q_project(cfull, q_lora_g, w_bf, gain, c_t, s1_t, s2_t, *, tm):
    t = cfull.shape[0]
    kq, n = w_bf.shape
    return pl.pallas_call(
        _mla_q_kernel,
        grid=(t // tm, 1),
        in_specs=[pl.BlockSpec((tm, kq), lambda i, j: (i, 0)),
                  pl.BlockSpec((1, kq), lambda i, j: (0, 0)),
                  pl.BlockSpec((kq, n), lambda i, j: (0, 0), pipeline_mode=pl.Buffered(1)),
                  pl.BlockSpec((1, n), lambda i, j: (0, 0)),
                  pl.BlockSpec((tm, LANES), lambda i, j: (i, 0)),
                  pl.BlockSpec((tm, LANES), lambda i, j: (i, 0)),
                  pl.BlockSpec((tm, LANES), lambda i, j: (i, 0))],
        out_specs=pl.BlockSpec((tm, n), lambda i, j: (i, 0)),
        out_shape=jax.ShapeDtypeStruct((t, n), BF16),
        compiler_params=_cparams("arbitrary", "arbitrary"),
        name="mla_q_project",
    )(cfull, q_lora_g.reshape(1, kq), w_bf, gain, c_t, s1_t, s2_t)


def _mla_kv_kernel(ckv_ref, g_ref, kr_ref, w_ref, gn_ref, gt_ref, c_ref, s1_ref, s2_ref, k_ref, v_ref):
    h = _rms(ckv_ref[...], g_ref[...]).astype(BF16)
    kr = kr_ref[...]
    ss_rope = jnp.sum(kr * kr, axis=-1, keepdims=True)
    rot = _rope64(kr * gt_ref[...], c_ref[...], s1_ref[...], s2_ref[...])
    per_head = MLA_NOPE + MLA_V
    for hh in range(w_ref.shape[1] // per_head):
        y = jnp.dot(h, w_ref[:, hh * per_head:(hh + 1) * per_head], preferred_element_type=F32)
        kn = y[:, :MLA_NOPE]
        ms = (jnp.sum(kn * kn, axis=-1, keepdims=True) + ss_rope) * (1.0 / MLA_QK)
        rs = lax.rsqrt(ms + NORM_EPS)
        lo = hh * MLA_HEAD_PAD
        k_ref[:, lo:lo + MLA_NOPE] = (kn * rs * gn_ref[...]).astype(k_ref.dtype)
        k_ref[:, lo + MLA_NOPE:lo + MLA_HEAD_PAD] = (rot * rs).astype(k_ref.dtype)
        v_ref[:, hh * MLA_V:(hh + 1) * MLA_V] = y[:, MLA_NOPE:].astype(v_ref.dtype)


def mla_kv_project(cfull, kv_lora_g, w_bf, gain_n, gain_t, c_t, s1_t, s2_t, *, tm, q_lora):
    t = cfull.shape[0]
    kkv, n = w_bf.shape
    heads = n // (MLA_NOPE + MLA_V)
    return pl.pallas_call(
        _mla_kv_kernel,
        grid=(t // tm, 1),
        in_specs=[pl.BlockSpec((tm, kkv), lambda i, j: (i, q_lora // kkv)),
                  pl.BlockSpec((1, kkv), lambda i, j: (0, 0)),
                  pl.BlockSpec((tm, LANES), lambda i, j: (i, (q_lora + kkv) // LANES)),
                  pl.BlockSpec((kkv, n), lambda i, j: (0, 0), pipeline_mode=pl.Buffered(1)),
                  pl.BlockSpec((1, LANES), lambda i, j: (0, 0)),
                  pl.BlockSpec((1, LANES), lambda i, j: (0, 0)),
                  pl.BlockSpec((tm, LANES), lambda i, j: (i, 0)),
                  pl.BlockSpec((tm, LANES), lambda i, j: (i, 0)),
                  pl.BlockSpec((tm, LANES), lambda i, j: (i, 0))],
        out_specs=[pl.BlockSpec((tm, heads * MLA_HEAD_PAD), lambda i, j: (i, 0)),
                   pl.BlockSpec((tm, heads * MLA_V), lambda i, j: (i, 0))],
        out_shape=[jax.ShapeDtypeStruct((t, heads * MLA_HEAD_PAD), BF16),
                   jax.ShapeDtypeStruct((t, heads * MLA_V), BF16)],
        compiler_params=_cparams("arbitrary", "arbitrary"),
        name="mla_kv_project",
    )(cfull, kv_lora_g.reshape(1, kkv), cfull, w_bf, gain_n, gain_t, c_t, s1_t, s2_t)


_NT = (((1,), (1,)), ((), ()))


def _flash_scores(q, k, s_buf):
    s_buf[:, :k.shape[0]] = lax.dot_general(q, k, _NT, preferred_element_type=F32)


def _flash_update(s_scr, v, p_scr, m_scr, l_scr, acc_scr, rg):
    tq = s_scr.shape[0]
    w = v.shape[0]
    n_tiles = w // LANES

    for g in range(tq // rg):
        rows_ = slice(g * rg, (g + 1) * rg)
        tiles = [s_scr[rows_, t * LANES:(t + 1) * LANES] for t in range(n_tiles)]
        mx = tiles[0]
        for st in tiles[1:]:
            mx = jnp.maximum(mx, st)
        m_prev = m_scr[rows_, :]
        m_next = jnp.maximum(m_prev, jnp.max(mx, axis=-1, keepdims=True))
        alpha = jnp.exp2(m_prev - m_next)
        psum = None
        for t, st in enumerate(tiles):
            p = jnp.exp2(st - m_next)
            p_scr[rows_, t * LANES:(t + 1) * LANES] = p.astype(p_scr.dtype)
            psum = p if psum is None else psum + p
        l_scr[rows_, :] = alpha * l_scr[rows_, :] + psum
        acc_scr[rows_, :] = alpha * acc_scr[rows_, :]
        m_scr[rows_, :] = m_next
    acc_scr[...] += jnp.dot(p_scr[:, :w], v, preferred_element_type=F32)


def _flash_kernel(*refs, n_lat_chunks, tk, rg):
    if n_lat_chunks:
        q_ref, kc_ref, vc_ref, k_ref, v_ref, _, o_ref, s0, s1, p0, p1, m_scr, l_scr, acc_scr = refs
    else:
        q_ref, kc_ref, vc_ref, _, o_ref, s0, s1, p0, p1, m_scr, l_scr, acc_scr = refs
    state = (m_scr, l_scr, acc_scr, rg)
    m_scr[...] = jnp.full(m_scr.shape, -jnp.inf, F32)
    l_scr[...] = jnp.zeros(l_scr.shape, F32)
    acc_scr[...] = jnp.zeros(acc_scr.shape, F32)
    q = q_ref[...]
    _flash_scores(q, kc_ref[...], s0)
    if n_lat_chunks:
        assert n_lat_chunks % 2 == 0

        def k_chunk(c):
            return k_ref[pl.ds(pl.multiple_of(c * tk, tk), tk), :]

        def v_chunk(c):
            return v_ref[pl.ds(pl.multiple_of(c * tk, tk), tk), :]

        _flash_scores(q, k_chunk(0), s1)
        _flash_update(s0, vc_ref[...], p0, *state)

        def body(j, carry):
            _flash_scores(q, k_chunk(2 * j + 1), s0)
            _flash_update(s1, v_chunk(2 * j), p1, *state)
            _flash_scores(q, k_chunk(2 * j + 2), s1)
            _flash_update(s0, v_chunk(2 * j + 1), p0, *state)
            return carry
        lax.fori_loop(0, n_lat_chunks // 2 - 1, body, 0)
        _flash_scores(q, k_chunk(n_lat_chunks - 1), s0)
        _flash_update(s1, v_chunk(n_lat_chunks - 2), p1, *state)
        _flash_update(s0, v_chunk(n_lat_chunks - 1), p0, *state)
    else:
        _flash_update(s0, vc_ref[...], p0, *state)
    l = jnp.sum(l_scr[...], axis=-1, keepdims=True)
    o_ref[...] = (acc_scr[...] / l).astype(o_ref.dtype)


def _flash_scratch(tq, w):
    return [pltpu.VMEM((tq, w), F32), pltpu.VMEM((tq, w), F32),
            pltpu.VMEM((tq, w), BF16), pltpu.VMEM((tq, w), BF16),
            pltpu.VMEM((tq, LANES), F32), pltpu.VMEM((tq, LANES), F32), pltpu.VMEM((tq, LANES), F32)]


def ctx_attention(qa, ka, va, *, batch, seq, ctx, heads, dk, q_col0, k_col0, v_col0, kv_group):
    t = qa.shape[0]
    ctx_blk0 = (batch * seq) // ctx
    o_shape = jax.ShapeDtypeStruct((t, heads * LANES), BF16)
    return pl.pallas_call(
        functools.partial(_flash_kernel, n_lat_chunks=0, tk=0, rg=FLASH_ROW_GROUP),
        grid=(batch, heads),
        in_specs=[pl.BlockSpec((ctx, dk), lambda b, h: (ctx_blk0 + b, q_col0 + h)),
                  pl.BlockSpec((ctx, dk), lambda b, h: (ctx_blk0 + b, k_col0 + h // kv_group)),
                  pl.BlockSpec((ctx, LANES), lambda b, h: (ctx_blk0 + b, v_col0 + h // kv_group)),
                  pl.BlockSpec(memory_space=pl.ANY)],
        out_specs=pl.BlockSpec((ctx, LANES), lambda b, h: (ctx_blk0 + b, h)),
        out_shape=o_shape,
        input_output_aliases={3: 0},
        scratch_shapes=_flash_scratch(ctx, ctx),
        compiler_params=_cparams("arbitrary", "arbitrary"),
        name="ctx_attention",
    )(qa, ka, va, jnp.zeros(o_shape.shape, o_shape.dtype))


def dense_attention(qa, ka, va, o_ctx, *, batch, seq, ctx, heads, dk, q_col0, k_col0, v_col0, kv_group, tq, tk):
    nq = seq // tq
    ctx_blk0 = (batch * seq) // ctx
    return pl.pallas_call(
        functools.partial(_flash_kernel, n_lat_chunks=seq // tk, tk=tk, rg=FLASH_ROW_GROUP),
        grid=(batch, heads, nq),
        in_specs=[pl.BlockSpec((tq, dk), lambda b, h, i: (b * nq + i, q_col0 + h)),
                  pl.BlockSpec((ctx, dk), lambda b, h, i: (ctx_blk0 + b, k_col0 + h // kv_group)),
                  pl.BlockSpec((ctx, LANES), lambda b, h, i: (ctx_blk0 + b, v_col0 + h // kv_group)),
                  pl.BlockSpec((seq, dk), lambda b, h, i: (b, k_col0 + h // kv_group)),
                  pl.BlockSpec((seq, LANES), lambda b, h, i: (b, v_col0 + h // kv_group)),
                  pl.BlockSpec(memory_space=pl.ANY)],
        out_specs=pl.BlockSpec((tq, LANES), lambda b, h, i: (b * nq + i, h)),
        out_shape=jax.ShapeDtypeStruct(o_ctx.shape, o_ctx.dtype),
        input_output_aliases={5: 0},
        scratch_shapes=_flash_scratch(tq, max(tk, ctx)),
        compiler_params=_cparams("arbitrary", "arbitrary", "arbitrary"),
        name="dense_attention",
    )(qa, ka, va, ka, va, o_ctx)


def _na_kernel(q_ref, k_ref, v_ref, kc_ref, vc_ref, tb_ref, _, o_ref, *, rows, kr):
    kc = kc_ref[...]
    vc = vc_ref[...]
    nt = (((1,), (1,)), ((), ()))

    def body(r, carry):
        rs = jnp.clip(r - kr // 2, 0, rows - kr)
        q0 = pl.multiple_of(r * GRID_W, GRID_W)
        k0 = pl.multiple_of(rs * GRID_W, GRID_W)
        q = q_ref[pl.ds(q0, GRID_W), :]
        kw = k_ref[pl.ds(k0, kr * GRID_W), :]
        vw = v_ref[pl.ds(k0, kr * GRID_W), :]
        s_win = lax.dot_general(q, kw, nt, preferred_element_type=F32) + tb_ref[0, r - rs]
        s_ctx = lax.dot_general(q, kc, nt, preferred_element_type=F32)
        m = jnp.maximum(jnp.max(s_win, axis=-1, keepdims=True), jnp.max(s_ctx, axis=-1, keepdims=True))
        p_win = jnp.exp2(s_win - m)
        p_ctx = jnp.exp2(s_ctx - m)
        l = jnp.sum(p_win, axis=-1, keepdims=True) + jnp.sum(p_ctx, axis=-1, keepdims=True)
        o = (jnp.dot(p_win.astype(vw.dtype), vw, preferred_element_type=F32)
             + jnp.dot(p_ctx.astype(vc.dtype), vc, preferred_element_type=F32))
        o_ref[pl.ds(q0, GRID_W), :] = (o / l).astype(o_ref.dtype)
        return carry

    lax.fori_loop(0, rows, body, 0, unroll=8)


def na_attention(qkv, tb, o_ctx, *, batch, seq, ctx, heads):
    t = qkv.shape[0]
    rows = seq // GRID_W
    kr = min(NA_ROWS, rows)
    ctx_blk0 = (batch * seq) // ctx
    d = HEAD_DIM
    return pl.pallas_call(
        functools.partial(_na_kernel, rows=rows, kr=kr),
        grid=(batch, heads),
        in_specs=[pl.BlockSpec((seq, d), lambda b, h: (b, h)),
                  pl.BlockSpec((seq, d), lambda b, h: (b, heads + h)),
                  pl.BlockSpec((seq, d), lambda b, h: (b, 2 * heads + h)),
                  pl.BlockSpec((ctx, d), lambda b, h: (ctx_blk0 + b, heads + h)),
                  pl.BlockSpec((ctx, d), lambda b, h: (ctx_blk0 + b, 2 * heads + h)),
                  pl.BlockSpec((1, kr, GRID_W, kr * GRID_W), lambda b, h: (h, 0, 0, 0)),
                  pl.BlockSpec(memory_space=pl.ANY)],
        out_specs=pl.BlockSpec((seq, d), lambda b, h: (b, h)),
        out_shape=jax.ShapeDtypeStruct((t, heads * d), BF16),
        input_output_aliases={6: 0},
        compiler_params=_cparams("arbitrary", "arbitrary"),
        name="na_attention",
    )(qkv, qkv, qkv, qkv, qkv, tb, o_ctx)


def _pack_bf16_pairs(x):
    n = x.shape[1] // 2
    lo = lax.bitcast_convert_type(x[:, :n].astype(BF16).astype(F32), jnp.uint32)
    hi = lax.bitcast_convert_type(x[:, n:].astype(BF16).astype(F32), jnp.uint32)
    return (lo >> 16) | (hi & jnp.uint32(0xFFFF0000))


def _unpack_lo(words):
    return lax.bitcast_convert_type(words << 16, F32)


def _unpack_hi(words):
    return lax.bitcast_convert_type(words & jnp.uint32(0xFFFF0000), F32)


def _proj_router_kernel(o_ref, wo_ref, x_ref, gate_ref, g_ref, sh_ref, sc_ref, w_ref, b_ref,
                        xo_ref, h_ref, route_ref, counts_ref, carry_scr, *, n_groups, epg):
    @pl.when(pl.program_id(0) == 0)
    def _():
        carry_scr[...] = jnp.zeros(carry_scr.shape, F32)

    x_new = x_ref[...] + gate_ref[0] * jnp.dot(o_ref[...], wo_ref[...], preferred_element_type=F32)
    xo_ref[...] = x_new
    h = _rms(x_new, g_ref[...]) * (1.0 + sc_ref[0]) + sh_ref[0]
    h_ref[...] = _pack_bf16_pairs(h)
    w = w_ref[...]
    h_hi = h.astype(BF16)
    h_lo = (h - h_hi.astype(F32)).astype(BF16)
    w_hi = w.astype(BF16)
    w_lo = (w - w_hi.astype(F32)).astype(BF16)
    both = jnp.dot(h_hi, jnp.concatenate([w_hi, w_lo], axis=1), preferred_element_type=F32)
    logits = (both[:, :LANES] + jnp.dot(h_lo, w_hi, preferred_element_type=F32) + both[:, LANES:]) + b_ref[...]
    lane = lax.broadcasted_iota(jnp.int32, logits.shape, 1).astype(F32)
    is_g = lane < n_groups
    g_max = jnp.max(jnp.where(is_g, logits, -jnp.inf), axis=-1, keepdims=True)
    g_sum = jnp.sum(jnp.where(is_g, jnp.exp(logits - g_max), 0.0), axis=-1, keepdims=True)
    g_sel = jnp.min(jnp.where(is_g & (logits == g_max), lane, float(LANES)), axis=-1, keepdims=True)
    lo = n_groups + epg * g_sel
    in_grp = (lane >= lo) & (lane < lo + epg)
    t1 = jnp.max(jnp.where(in_grp, logits, -jnp.inf), axis=-1, keepdims=True)
    i1 = jnp.min(jnp.where(in_grp & (logits == t1), lane, float(LANES)), axis=-1, keepdims=True)
    rest = in_grp & (lane != i1)
    t2 = jnp.max(jnp.where(rest, logits, -jnp.inf), axis=-1, keepdims=True)
    i2 = jnp.min(jnp.where(rest & (logits == t2), lane, float(LANES)), axis=-1, keepdims=True)
    d = jnp.exp(t2 - t1)
    gate = 1.0 / g_sum
    w1 = gate * (1.0 / (1.0 + d))
    w2 = gate * (d / (1.0 + d))
    tm = logits.shape[0]
    chosen = jnp.where((lane == i1) | (lane == i2), 1.0, 0.0)
    earlier = jnp.where(lax.broadcasted_iota(jnp.int32, (tm, tm), 1) < lax.broadcasted_iota(jnp.int32, (tm, tm), 0),
                        1.0, 0.0).astype(BF16)
    before = jnp.dot(earlier, chosen.astype(BF16), preferred_element_type=F32) + carry_scr[...]
    rank1 = jnp.sum(jnp.where(lane == i1, before, 0.0), axis=-1, keepdims=True)
    rank2 = jnp.sum(jnp.where(lane == i2, before, 0.0), axis=-1, keepdims=True)
    total = carry_scr[...] + jnp.sum(chosen, axis=0, keepdims=True)
    carry_scr[...] = total
    counts_ref[...] = jnp.broadcast_to(total, counts_ref.shape)
    vals = (i1 - n_groups, i2 - n_groups, w1, w2, rank1, rank2)
    route = jnp.zeros(logits.shape, F32)
    for pos, v in enumerate(vals):
        route = jnp.where(lane == pos, v, route)
    route_ref[...] = route


def proj_residual_router(o, wo_bf, xs, mods, layer, norm_g, w_r, b_r, *, tm, seq, batch, n_groups, epg):
    t, d = xs.shape
    k = o.shape[1]
    gate_spec, sh_spec, sc_spec = _mod_specs(layer, (2, 3, 4), tm, seq, batch, d)
    return pl.pallas_call(
        functools.partial(_proj_router_kernel, n_groups=n_groups, epg=epg),
        grid=(t // tm, 1),
        in_specs=[pl.BlockSpec((tm, k), lambda i, j: (i, 0)),
                  pl.BlockSpec((k, d), lambda i, j: (0, 0), pipeline_mode=pl.Buffered(1)),
                  pl.BlockSpec((tm, d), lambda i, j: (i, 0)),
                  gate_spec,
                  pl.BlockSpec((1, d), lambda i, j: (0, 0)),
                  sh_spec, sc_spec,
                  pl.BlockSpec((d, LANES), lambda i, j: (0, 0)),
                  pl.BlockSpec((1, LANES), lambda i, j: (0, 0))],
        out_specs=[pl.BlockSpec((tm, d), lambda i, j: (i, 0)),
                   pl.BlockSpec((tm, d // 2), lambda i, j: (i, 0)),
                   pl.BlockSpec((tm, LANES), lambda i, j: (i, 0)),
                   pl.BlockSpec((8, LANES), lambda i, j: (0, 0))],
        out_shape=[jax.ShapeDtypeStruct((t, d), F32), jax.ShapeDtypeStruct((t, d // 2), jnp.uint32),
                   jax.ShapeDtypeStruct((t, LANES), F32), jax.ShapeDtypeStruct((8, LANES), F32)],
        scratch_shapes=[pltpu.VMEM((1, LANES), F32)],
        compiler_params=_cparams("arbitrary", "arbitrary"),
        name="proj_residual_router",
    )(o, wo_bf, xs, mods, norm_g.reshape(1, d), mods, mods, w_r, b_r)


def _issue_row_gather(idx_ref, src_hbm, dst, sem, n_rows):
    def body(r, carry):
        tok = idx_ref[0, 0, r]
        pltpu.make_async_copy(src_hbm.at[pl.ds(tok, 1)], dst.at[pl.ds(r, 1)], sem).start()
        return carry
    lax.fori_loop(0, n_rows, body, 0, unroll=8)


def _wait_row_gather(src_hbm, dst, sem, n_rows):
    pltpu.make_async_copy(src_hbm.at[pl.ds(0, n_rows)], dst, sem).wait()


def _expert_kernel(be_ref, idx_ref, idx1_ref, idx2_ref, h_hbm, win_ref, wout_ref, y_ref,
                   xbuf0, xbuf1, xbuf2, sem, win_bf, wout_bf, *, nb, cast_rows, k_chunk):
    i = pl.program_id(0)
    blk, half = xbuf0.shape
    bufs = (xbuf0, xbuf1, xbuf2)
    n_points = 2 * half // k_chunk + half // k_chunk
    base, extra = divmod(blk, n_points)
    point_rows = [base + (1 if p < extra else 0) for p in range(n_points)]

    @pl.when(i == 0)
    def _():
        _issue_row_gather(idx_ref, h_hbm, xbuf0, sem.at[0], blk)
        _issue_row_gather(idx1_ref, h_hbm, xbuf1, sem.at[1], blk)

    expert_changed = (i == 0) | (be_ref[i] != be_ref[jnp.maximum(i - 1, 0)])

    @pl.when(expert_changed)
    def _():
        def cast_in(c, carry):
            r0 = pl.multiple_of(c * cast_rows, cast_rows)
            win_bf[pl.ds(r0, cast_rows), :] = win_ref[0, 0, pl.ds(r0, cast_rows), :].astype(BF16)
            return carry
        lax.fori_loop(0, win_bf.shape[0] // cast_rows, cast_in, 0)

        def cast_out(c, carry):
            r0 = pl.multiple_of(c * cast_rows, cast_rows)
            wout_bf[pl.ds(r0, cast_rows), :] = wout_ref[0, 0, pl.ds(r0, cast_rows), :].astype(BF16)
            return carry
        lax.fori_loop(0, wout_bf.shape[0] // cast_rows, cast_out, 0)

    def run(cur):
        ahead = (cur + 2) % 3
        x_cur, x_new = bufs[cur], bufs[ahead]
        issued = [0]

        def request_rows(point):
            for r in range(issued[0], issued[0] + point_rows[point]):
                pltpu.make_async_copy(h_hbm.at[pl.ds(idx2_ref[0, 0, r], 1)], x_new.at[pl.ds(r, 1)],
                                      sem.at[ahead]).start()
            issued[0] += point_rows[point]

        _wait_row_gather(h_hbm, x_cur, sem.at[cur], blk)
        point = 0
        gu = None
        for f0 in range(0, 2 * half, k_chunk):
            request_rows(point)
            point += 1
            words = x_cur[:, f0 % half:f0 % half + k_chunk]
            xk = _unpack_lo(words) if f0 < half else _unpack_hi(words)
            part = jnp.dot(xk.astype(BF16), win_bf[f0:f0 + k_chunk, :], preferred_element_type=F32)
            gu = part if gu is None else gu + part
        f = gu.shape[1] // 2
        act = (_silu(gu[:, :f]) * gu[:, f:]).astype(BF16)
        for c0 in range(0, half, k_chunk):
            request_rows(point)
            point += 1
            y_lo = jnp.dot(act, wout_bf[:, c0:c0 + k_chunk], preferred_element_type=F32)
            y_hi = jnp.dot(act, wout_bf[:, half + c0:half + c0 + k_chunk], preferred_element_type=F32)
            y_ref[:, c0:c0 + k_chunk] = _pack_bf16_pairs(jnp.concatenate([y_lo, y_hi], axis=1))
        assert issued[0] == blk

        @pl.when(i == nb - 1)
        def _():
            _wait_row_gather(h_hbm, bufs[(cur + 1) % 3], sem.at[(cur + 1) % 3], blk)
            _wait_row_gather(h_hbm, x_new, sem.at[ahead], blk)

    for c in range(3):
        @pl.when(i % 3 == c)
        def _(c=c):
            run(c)


def moe_experts(h, slot_tok, block_expert, w_in, w_out, layer):
    half = h.shape[1]
    d = 2 * half
    f2 = w_in.shape[-1]
    f = f2 // 2
    nb = block_expert.shape[0]
    blk = slot_tok.shape[0] // nb
    idx = slot_tok.reshape(nb, 1, blk)
    grid_spec = pltpu.PrefetchScalarGridSpec(
        num_scalar_prefetch=1,
        grid=(nb,),
        in_specs=[pl.BlockSpec((1, 1, blk), lambda i, be: (i, 0, 0), memory_space=pltpu.SMEM),
                  pl.BlockSpec((1, 1, blk), lambda i, be: (jnp.minimum(i + 1, nb - 1), 0, 0),
                               memory_space=pltpu.SMEM),
                  pl.BlockSpec((1, 1, blk), lambda i, be: (jnp.minimum(i + 2, nb - 1), 0, 0),
                               memory_space=pltpu.SMEM),
                  pl.BlockSpec(memory_space=pl.ANY),
                  pl.BlockSpec((1, 1, d, f2), lambda i, be: (layer, be[i], 0, 0)),
                  pl.BlockSpec((1, 1, f, d), lambda i, be: (layer, be[i], 0, 0))],
        out_specs=pl.BlockSpec((blk, half), lambda i, be: (i, 0)),
        scratch_shapes=[pltpu.VMEM((blk, half), jnp.uint32),
                        pltpu.VMEM((blk, half), jnp.uint32),
                        pltpu.VMEM((blk, half), jnp.uint32),
                        pltpu.SemaphoreType.DMA((3,)),
                        pltpu.VMEM((d, f2), BF16),
                        pltpu.VMEM((f, d), BF16)],
    )
    return pl.pallas_call(
        functools.partial(_expert_kernel, nb=nb, cast_rows=256, k_chunk=256),
        grid_spec=grid_spec,
        out_shape=jax.ShapeDtypeStruct((nb * blk, half), jnp.uint32),
        compiler_params=_cparams("arbitrary"),
        name="moe_experts",
    )(block_expert, idx, idx, idx, h, w_in, w_out)


def _combine_kernel(pos_ref, pos_next_ref, yb_hbm, x_ref, route_ref, gate_ref, o_ref, ybuf, sem, *, nt):
    i = pl.program_id(0)
    slot = i % 2
    n_rows = ybuf.shape[1]
    tm = n_rows // MOE_TOP_K

    @pl.when(i == 0)
    def _():
        _issue_row_gather(pos_ref, yb_hbm, ybuf.at[0], sem.at[0], n_rows)

    @pl.when(i + 1 < nt)
    def _():
        _issue_row_gather(pos_next_ref, yb_hbm, ybuf.at[1 - slot], sem.at[1 - slot], n_rows)

    _wait_row_gather(yb_hbm, ybuf.at[slot], sem.at[slot], n_rows)
    route = route_ref[...]
    w1, w2 = route[:, 2:3], route[:, 3:4]
    y1, y2 = ybuf[slot, :tm, :], ybuf[slot, tm:, :]
    half = y1.shape[1]
    gate = gate_ref[0]
    o_ref[:, :half] = x_ref[:, :half] + gate[:, :half] * (w1 * _unpack_lo(y1) + w2 * _unpack_lo(y2))
    o_ref[:, half:] = x_ref[:, half:] + gate[:, half:] * (w1 * _unpack_hi(y1) + w2 * _unpack_hi(y2))


def moe_combine(yb, pos, xs, route, mods, layer, *, tm, seq, batch, out_rows):
    t, d = xs.shape
    pos_t = pos.reshape(t // tm, tm, MOE_TOP_K).transpose(0, 2, 1).reshape(t // tm, 1, MOE_TOP_K * tm)
    nt = out_rows // tm

    def seg(i):
        return jnp.minimum((i * tm) // seq, batch)

    return pl.pallas_call(
        functools.partial(_combine_kernel, nt=nt),
        grid=(nt,),
        in_specs=[pl.BlockSpec((1, 1, MOE_TOP_K * tm), lambda i: (i, 0, 0), memory_space=pltpu.SMEM),
                  pl.BlockSpec((1, 1, MOE_TOP_K * tm), lambda i: (jnp.minimum(i + 1, nt - 1), 0, 0),
                               memory_space=pltpu.SMEM),
                  pl.BlockSpec(memory_space=pl.ANY),
                  pl.BlockSpec((tm, d), lambda i: (i, 0)),
                  pl.BlockSpec((tm, LANES), lambda i: (i, 0)),
                  pl.BlockSpec((1, 1, d), lambda i: ((layer * MOD_ROWS + seg(i)) * 6 + 5, 0, 0))],
        out_specs=pl.BlockSpec((tm, d), lambda i: (i, 0)),
        out_shape=jax.ShapeDtypeStruct((out_rows, d), F32),
        scratch_shapes=[pltpu.VMEM((2, MOE_TOP_K * tm, d // 2), jnp.uint32), pltpu.SemaphoreType.DMA((2,))],
        compiler_params=_cparams("arbitrary"),
        name="moe_combine",
    )(pos_t, pos_t, yb, xs, route, mods)


def _dispatch_plan(route, counts, n_experts, blk):
    t = route.shape[0]
    nk = t * MOE_TOP_K
    padded = (counts + blk - 1) // blk * blk
    pad_end = jnp.cumsum(padded)
    pad_start = pad_end - padded
    nb = -(-(nk + n_experts * (blk - 1)) // blk)
    e = route[:, :MOE_TOP_K].astype(jnp.int32)
    rank = route[:, 2 * MOE_TOP_K:3 * MOE_TOP_K].astype(jnp.int32)
    onehot = e[..., None] == jnp.arange(n_experts, dtype=jnp.int32)
    dest = jnp.sum(jnp.where(onehot, pad_start, 0), axis=-1) + rank
    tok = jnp.arange(nk, dtype=jnp.int32) // MOE_TOP_K
    slot_tok = jnp.zeros((nb * blk,), jnp.int32).at[dest.reshape(nk)].set(tok, unique_indices=True)
    first_row = jnp.arange(nb, dtype=jnp.int32) * blk
    block_expert = jnp.minimum(jnp.sum(pad_end[None, :] <= first_row[:, None], axis=1), n_experts - 1)
    return slot_tok, block_expert.astype(jnp.int32), dest


def _axial_angles(seq, dim):
    n_freq = dim // 4
    freqs = ROPE_THETA ** (-jnp.arange(n_freq, dtype=F32) / n_freq)
    tok = jnp.arange(seq, dtype=jnp.int32)
    row = (tok // GRID_W).astype(F32)
    col = (tok % GRID_W).astype(F32)
    return jnp.concatenate([row[:, None] * freqs, col[:, None] * freqs], axis=-1)


def _stream_table(lat, ctx_fill, batch, n_ctx_rows):
    ctx_rows = jnp.broadcast_to(ctx_fill[None, :], (n_ctx_rows, lat.shape[1]))
    return jnp.concatenate([jnp.tile(lat, (batch, 1)), ctx_rows], axis=0)


def _rope_tables_128(seq, batch, n_ctx_rows):
    ang = _axial_angles(seq, HEAD_DIM)
    cos, sin = jnp.cos(ang), jnp.sin(ang)
    c = jnp.concatenate([cos, cos], axis=-1)
    s = jnp.concatenate([-sin, sin], axis=-1)
    return (_stream_table(c, jnp.ones((HEAD_DIM,), F32), batch, n_ctx_rows),
            _stream_table(s, jnp.zeros((HEAD_DIM,), F32), batch, n_ctx_rows))


def _rope_tables_64(seq, batch, n_ctx_rows):
    ang = _axial_angles(seq, MLA_ROPE)
    cos, sin = jnp.cos(ang), jnp.sin(ang)
    half = MLA_ROPE // 2
    z = jnp.zeros((seq, LANES - MLA_ROPE), F32)
    zh = jnp.zeros((seq, half), F32)
    c = jnp.concatenate([cos, cos, z], axis=-1)
    s1 = jnp.concatenate([-sin, zh, z], axis=-1)
    s2 = jnp.concatenate([zh, sin, z], axis=-1)
    ones_pad = jnp.concatenate([jnp.ones((MLA_ROPE,), F32), jnp.zeros((LANES - MLA_ROPE,), F32)])
    zeros = jnp.zeros((LANES,), F32)
    return (_stream_table(c, ones_pad, batch, n_ctx_rows),
            _stream_table(s1, zeros, batch, n_ctx_rows),
            _stream_table(s2, zeros, batch, n_ctx_rows))


def _na_bias_table(rpb, kr):
    col = jnp.arange(GRID_W, dtype=jnp.int32)
    col_start = jnp.clip(col - NA_COLS // 2, 0, GRID_W - NA_COLS)
    col_in = (col[None, :] >= col_start[:, None]) & (col[None, :] < col_start[:, None] + NA_COLS)
    dc = jnp.clip(col[None, :] - col[:, None] + NA_COLS - 1, 0, 2 * NA_COLS - 2)
    h = rpb.shape[0]
    by_col = jnp.zeros((h, 2 * NA_ROWS - 1, GRID_W, GRID_W), F32)
    for cc in range(2 * NA_COLS - 1):
        by_col = jnp.where(dc[None, None] == cc, rpb[:, :, cc, None, None].astype(F32), by_col)
    by_col = jnp.where(col_in[None, None], by_col, NEG_INF)
    per_shift = [by_col[:, NA_ROWS - 1 - s:NA_ROWS - 1 - s + kr] for s in range(kr)]
    bias = jnp.stack(per_shift, axis=1)
    return bias.transpose(0, 1, 3, 2, 4).reshape(h, kr, GRID_W, kr * GRID_W)


def _pick_tile(*extents):
    for tm in (512, 256, 128):
        if all(e % tm == 0 for e in extents):
            return tm
    raise ValueError("row extents must be multiples of 128")


def kernel(x, c, ctx, c_ctx, mod_w, mod_b, norm_g, gqa_w_qkv, gqa_q_norm, gqa_k_norm, gqa_w_o, na_w_qkv, na_q_norm, na_k_norm, na_rpb, na_w_o, mla_w_down, mla_q_lora_norm, mla_w_uq, mla_kv_lora_norm, mla_w_ukv, mla_q_norm, mla_k_norm, mla_w_o, moe_w_group, moe_b_group, moe_w_expert, moe_b_expert, moe_w_in, moe_w_out):
    batch, seq, d = x.shape
    n_ctx = ctx.shape[1]
    depth = mod_w.shape[0]
    heads = d // HEAD_DIM
    kv_heads = heads // 4
    n_lat = batch * seq
    n_ctx_rows = batch * n_ctx
    assert batch + 1 <= MOD_ROWS and n_lat % n_ctx == 0 and seq % GRID_W == 0
    tm = _pick_tile(seq, n_ctx_rows)
    tq = min(FLASH_TQ, seq)
    tk = min(FLASH_TK, seq)
    assert seq % tq == 0 and seq % (2 * tk) == 0
    tn_d = min(512, d)
    dims = dict(tm=tm, seq=seq, batch=batch)

    xs = jnp.concatenate([x.reshape(n_lat, d), ctx.reshape(n_ctx_rows, d)], axis=0)

    cond = jnp.zeros((MOD_ROWS, d), F32).at[:batch].set(c).at[batch].set(c_ctx)
    mods = adaln_all(cond, mod_w, mod_b).reshape(depth * MOD_ROWS * 6, 1, d)

    cos_a, sin_a = _rope_tables_128(seq, batch, n_ctx_rows)
    c64, s64a, s64b = _rope_tables_64(seq, batch, n_ctx_rows)

    n_groups = moe_w_group.shape[-1]
    epg = moe_w_expert.shape[-1]
    n_experts = n_groups * epg
    assert n_groups + n_experts <= LANES

    for i in range(depth):
        kind, j = i % 3, i // 3
        if kind == 0:
            scale = HEAD_DIM ** -0.5 * LOG2E
            nq, nkv = heads * HEAD_DIM, kv_heads * HEAD_DIM
            gain = jnp.concatenate([jnp.tile(gqa_q_norm[j] * scale, heads), jnp.tile(gqa_k_norm[j], kv_heads),
                                    jnp.ones((nkv,), F32)]).reshape(1, -1)
            qkv = qkv_project(xs, mods, i, norm_g[i, 0], gqa_w_qkv[j].astype(BF16), gain, cos_a, sin_a,
                              n_norm_cols=nq + nkv, tn=min(512, nkv), rope=True, **dims)
            cols = dict(dk=HEAD_DIM, q_col0=0, k_col0=heads, v_col0=heads + kv_heads, kv_group=heads // kv_heads)
            o_ctx = ctx_attention(qkv, qkv, qkv, batch=batch, seq=seq, ctx=n_ctx, heads=heads, **cols)
            o = dense_attention(qkv, qkv, qkv, o_ctx, batch=batch, seq=seq, ctx=n_ctx, heads=heads, tq=tq, tk=tk,
                                **cols)
            w_o = gqa_w_o[j]
        elif kind == 1:
            scale = HEAD_DIM ** -0.5 * LOG2E
            nq = heads * HEAD_DIM
            gain = jnp.concatenate([jnp.tile(na_q_norm[j] * scale, heads), jnp.tile(na_k_norm[j], heads),
                                    jnp.ones((nq,), F32)]).reshape(1, -1)
            qkv = qkv_project(xs, mods, i, norm_g[i, 0], na_w_qkv[j].astype(BF16), gain, cos_a, sin_a,
                              n_norm_cols=2 * nq, tn=tn_d, rope=False, **dims)
            rows = seq // GRID_W
            tb = _na_bias_table(na_rpb[j], min(NA_ROWS, rows)) * LOG2E
            o_ctx = ctx_attention(qkv, qkv, qkv, batch=batch, seq=seq, ctx=n_ctx, heads=heads, dk=HEAD_DIM,
                                  q_col0=0, k_col0=heads, v_col0=2 * heads, kv_group=1)
            o = na_attention(qkv, tb, o_ctx, batch=batch, seq=seq, ctx=n_ctx, heads=heads)
            w_o = na_w_o[j]
        else:
            scale = MLA_QK ** -0.5 * LOG2E
            q_lora = mla_q_lora_norm.shape[-1]
            kv_lora = mla_kv_lora_norm.shape[-1]
            assert q_lora % kv_lora == 0 and (q_lora + kv_lora) % LANES == 0
            n_down = q_lora + kv_lora + LANES
            w_down = jnp.pad(mla_w_down[j], ((0, 0), (0, n_down - mla_w_down.shape[-1]))).astype(BF16)
            cfull = mla_down(xs, mods, i, norm_g[i, 0], w_down, **dims)
            pad = MLA_HEAD_PAD - MLA_QK
            w_uq = jnp.pad(mla_w_uq[j].reshape(q_lora, heads, MLA_QK), ((0, 0), (0, 0), (0, pad)))
            w_uq = w_uq.reshape(q_lora, heads * MLA_HEAD_PAD).astype(BF16)
            q_gain = jnp.tile(jnp.pad(mla_q_norm[j] * scale, (0, pad)), heads).reshape(1, -1)
            qa = mla_q_project(cfull, mla_q_lora_norm[j], w_uq, q_gain, c64, s64a, s64b, tm=tm)
            gain_n = mla_k_norm[j, :MLA_NOPE].reshape(1, LANES)
            gain_t = jnp.pad(mla_k_norm[j, MLA_NOPE:], (0, LANES - MLA_ROPE)).reshape(1, LANES)
            ka, va = mla_kv_project(cfull, mla_kv_lora_norm[j], mla_w_ukv[j].astype(BF16), gain_n, gain_t,
                                    c64, s64a, s64b, tm=tm, q_lora=q_lora)
            cols = dict(dk=MLA_HEAD_PAD, q_col0=0, k_col0=0, v_col0=0, kv_group=1)
            o_ctx = ctx_attention(qa, ka, va, batch=batch, seq=seq, ctx=n_ctx, heads=heads, **cols)
            o = dense_attention(qa, ka, va, o_ctx, batch=batch, seq=seq, ctx=n_ctx, heads=heads, tq=tq, tk=tk, **cols)
            w_o = mla_w_o[j]

        w_r = jnp.concatenate([moe_w_group[i], moe_w_expert[i].transpose(1, 0, 2).reshape(d, n_experts)], axis=1)
        w_r = jnp.pad(w_r, ((0, 0), (0, LANES - w_r.shape[1])))
        b_r = jnp.pad(jnp.concatenate([moe_b_group[i], moe_b_expert[i].reshape(-1)]),
                      (0, LANES - n_groups - n_experts)).reshape(1, LANES)
        xs, h2, route, lane_counts = proj_residual_router(o, w_o.astype(BF16), xs, mods, i, norm_g[i, 1], w_r, b_r,
                                                          n_groups=n_groups, epg=epg, **dims)
        counts = lane_counts[0, n_groups:n_groups + n_experts].astype(jnp.int32)
        slot_tok, block_expert, pos = _dispatch_plan(route, counts, n_experts, MOE_BLOCK)
        yb = moe_experts(h2, slot_tok, block_expert, moe_w_in, moe_w_out, i)
        out_rows = n_lat if i == depth - 1 else n_lat + n_ctx_rows
        xs = moe_combine(yb, pos, xs, route, mods, i, tm=min(COMBINE_TM, tm), seq=seq, batch=batch,
                         out_rows=out_rows)

    return xs.reshape(batch, seq, d)
```
